```python
import math
import jax
import jax.numpy as jnp
from jax import lax
import numpy as np

D_MODEL = 2048
BATCH = 2
SEQ = 8192
DEPTH = 2

ATT_HEADS = 8
ATT_KV_HEADS = 2
ATT_HEAD_DIM = 128
IDX_HEADS = 8
IDX_DIM = 64
TOPK_MAX = 256
Q_BLOCK = 128
REL_BUCKETS = 32
REL_MAX_DIST = 128
RWKV_HEAD = 64
RWKV_HEADS = 16
DECAY_RANK = 64
ICLR_RANK = 64
GATE_RANK = 160
D_FF = 5632
N_EXPERTS = 8
TOP_K_EXPERTS = 2
D_FF_EXPERT = 7168
PLE_DIM = 256
RMS_EPS = 1e-6
GN_EPS = 64e-5

ATT_WIDTH = ATT_HEADS * ATT_HEAD_DIM
KV_WIDTH = ATT_KV_HEADS * ATT_HEAD_DIM
RWKV_WIDTH = RWKV_HEADS * RWKV_HEAD
RWKV_SIZES = (RWKV_WIDTH, RWKV_WIDTH, RWKV_WIDTH, DECAY_RANK, ICLR_RANK, GATE_RANK)
RWKV_SLAB = 3 * RWKV_WIDTH + DECAY_RANK + ICLR_RANK + GATE_RANK
IN_SIZES = (ATT_WIDTH, KV_WIDTH, KV_WIDTH, IDX_HEADS * IDX_DIM, IDX_DIM, IDX_HEADS, RWKV_SLAB, D_MODEL, D_MODEL)
IN_WIDTH = ATT_WIDTH + 2 * KV_WIDTH + IDX_HEADS * IDX_DIM + IDX_DIM + IDX_HEADS + RWKV_SLAB + 2 * D_MODEL
N_DENSE = (DEPTH + 1) // 2
N_MOE = DEPTH // 2

kernel_name = 'hybrid_dsa_rwkv7_gated_moe'


def _split(z, sizes):
    return jnp.split(z, np.cumsum(sizes)[:-1].tolist(), axis=-1)


def rms_norm(x, g):
    xf = x.astype(jnp.float32)
    y = xf * lax.rsqrt(jnp.mean(xf * xf, axis=-1, keepdims=True) + RMS_EPS)
    return (y * g.astype(jnp.float32)).astype(x.dtype)


def rel_bucket(dist):
    max_exact = REL_BUCKETS // 2
    dist = jnp.maximum(dist, 0)
    d_f = jnp.maximum(dist, 1).astype(jnp.float32)
    large = max_exact + (jnp.log(d_f / max_exact) / math.log(REL_MAX_DIST / max_exact)
                         * (REL_BUCKETS - max_exact)).astype(jnp.int32)
    large = jnp.minimum(large, REL_BUCKETS - 1)
    return jnp.where(dist < max_exact, dist, large)


def dsa_attention(q, k, v, qi, ki, wi, rel_bias):
    B, S = q.shape[0], q.shape[1]
    n_sel = min(TOPK_MAX, S // 4)
    nb = S // Q_BLOCK
    G = ATT_HEADS // ATT_KV_HEADS
    idx_scale = (IDX_HEADS ** -0.5) * (IDX_DIM ** -0.5)
    key_pos = jnp.arange(S, dtype=jnp.int32)

    def to_blocks(z):
        return z.reshape((B, nb, Q_BLOCK) + z.shape[2:]).swapaxes(0, 1)

    def block(args):
        qb, qib, wib, start = args
        t_pos = start + jnp.arange(Q_BLOCK, dtype=jnp.int32)
        dots = jnp.einsum('bthd,bsd->bths', qib, ki, preferred_element_type=jnp.float32)
        score = jnp.einsum('bth,bths->bts', wib.astype(jnp.float32), jax.nn.relu(dots)) * idx_scale
        causal = key_pos[None, :] <= t_pos[:, None]
        score = jnp.where(causal[None], score, -jnp.inf)
        _, idx = lax.top_k(score, n_sel)
        valid = idx <= t_pos[None, :, None]
        k_sel = jax.vmap(lambda kb, ib: kb[ib])(k, idx)
        v_sel = jax.vmap(lambda vb, ib: vb[ib])(v, idx)
        qg = qb.reshape(B, Q_BLOCK, ATT_KV_HEADS, G, ATT_HEAD_DIM)
        logits = jnp.einsum('btcgd,btjcd->btcgj', qg, k_sel,
                            preferred_element_type=jnp.float32) * (ATT_HEAD_DIM ** -0.5)
        bias = rel_bias[rel_bucket(t_pos[None, :, None] - idx)]
        bias = bias.reshape(B, Q_BLOCK, n_sel, ATT_KV_HEADS, G).transpose(0, 1, 3, 4, 2)
        logits = jnp.where(valid[:, :, None, None, :], logits + bias.astype(jnp.float32), -jnp.inf)
        probs = jax.nn.softmax(logits, axis=-1).astype(v.dtype)
        out = jnp.einsum('btcgj,btjcd->btcgd', probs, v_sel)
        return out.reshape(B, Q_BLOCK, ATT_WIDTH)

    starts = jnp.arange(nb, dtype=jnp.int32) * Q_BLOCK
    out = lax.map(block, (to_blocks(q), to_blocks(qi), to_blocks(wi), starts))
    return out.swapaxes(0, 1).reshape(B, S, ATT_WIDTH)


def token_shift(z):
    return jnp.pad(z, ((0, 0), (1, 0), (0, 0)))[:, :-1]


def rwkv7_mix(slab, mu, w0, w_up, a0, a_up, g_up, k_k, k_a, r_k, gn_w, gn_b):
    B, S = slab.shape[0], slab.shape[1]
    xs = slab + (token_shift(slab) - slab) * mu
    r, k, v, xw, xa, xg = _split(xs, RWKV_SIZES)
    d = (w0 + jnp.tanh(xw) @ w_up).astype(jnp.float32)
    decay = jnp.exp(-math.exp(-0.5) * jax.nn.sigmoid(d))
    a = jax.nn.sigmoid((a0 + xa @ a_up).astype(jnp.float32))
    g = jax.nn.sigmoid(xg) @ g_up

    def heads(z):
        return z.astype(jnp.float32).reshape(B, S, RWKV_HEADS, RWKV_HEAD)

    kk = heads(k * k_k)
    kk = kk * lax.rsqrt(jnp.maximum(jnp.sum(kk * kk, axis=-1, keepdims=True), 1e-24))
    kt = heads(k.astype(jnp.float32) * (1.0 + (a - 1.0) * k_a.astype(jnp.float32)))
    r_h = heads(r)
    v_h = heads(v)
    w_h = heads(decay)
    b_h = kk * heads(a)

    def step(state, inp):
        r_t, w_t, k_t, v_t, kk_t, b_t = inp
        sk = jnp.einsum('bhvk,bhk->bhv', state, kk_t)
        state = (state * w_t[:, :, None, :] - sk[..., None] * b_t[:, :, None, :]
                 + v_t[..., None] * k_t[:, :, None, :])
        y = jnp.einsum('bhvk,bhk->bhv', state, r_t)
        return state, y

    s0 = jnp.zeros((B, RWKV_HEADS, RWKV_HEAD, RWKV_HEAD), jnp.float32)
    xs_seq = (r_h.swapaxes(0, 1), w_h.swapaxes(0, 1), kt.swapaxes(0, 1),
              v_h.swapaxes(0, 1), kk.swapaxes(0, 1), b_h.swapaxes(0, 1))
    _, y = lax.scan(step, s0, xs_seq)
    y = y.swapaxes(0, 1)
    mean = jnp.mean(y, axis=-1, keepdims=True)
    var = jnp.mean(jnp.square(y - mean), axis=-1, keepdims=True)
    y = ((y - mean) * lax.rsqrt(var + GN_EPS)).reshape(B, S, RWKV_WIDTH)
    y = y * gn_w.astype(jnp.float32) + gn_b.astype(jnp.float32)
    bonus = jnp.sum(r_h * kt * r_k.astype(jnp.float32).reshape(RWKV_HEADS, RWKV_HEAD),
                    axis=-1, keepdims=True) * v_h
    y = (y + bonus.reshape(B, S, RWKV_WIDTH)) * g.astype(jnp.float32)
    return y.astype(slab.dtype)


def swiglu(x, w1, w3, w2):
    return (jax.nn.silu(x @ w1) * (x @ w3)) @ w2


def moe_swiglu(x, router, w1, w3, w2):
    B, S, D = x.shape
    xf = x.reshape(B * S, D)
    logits = (xf @ router).astype(jnp.float32)
    top_vals, top_idx = lax.top_k(logits, TOP_K_EXPERTS)
    gates = jax.nn.softmax(top_vals, axis=-1)
    combine = jnp.sum(jax.nn.one_hot(top_idx, N_EXPERTS, dtype=jnp.float32) * gates[..., None], axis=1)
    out = jnp.zeros((B * S, D), jnp.float32)
    for e in range(N_EXPERTS):
        out = out + combine[:, e:e + 1] * swiglu(xf, w1[e], w3[e], w2[e]).astype(jnp.float32)
    return out.astype(x.dtype).reshape(B, S, D)


def setup_inputs(seed: int = 0):
    key = jax.random.key(seed)
    ks = iter(jax.random.split(key, 40))

    def nrm(shape, scale):
        return jax.random.normal(next(ks), shape, jnp.float32) * scale

    def gain(shape):
        return 1.0 + nrm(shape, 0.02)

    L = DEPTH
    return {
        'x': nrm((BATCH, SEQ, D_MODEL), 1.0),
        'p': nrm((DEPTH, BATCH, SEQ, PLE_DIM), 1.0),
        'w_in': nrm((L, D_MODEL, IN_WIDTH), D_MODEL ** -0.5),
        'att_up': nrm((L, ATT_WIDTH, D_MODEL), ATT_WIDTH ** -0.5),
        'rwkv_up': nrm((L, RWKV_WIDTH, D_MODEL), RWKV_WIDTH ** -0.5),
        'w_out': nrm((L, D_MODEL, D_MODEL), D_MODEL ** -0.5),
        'rel_bias': nrm((REL_BUCKETS, ATT_HEADS), 0.5),
        'rwkv_mu': jax.random.uniform(next(ks), (L, RWKV_SLAB), jnp.float32),
        'rwkv_w0': nrm((L, RWKV_WIDTH), 1.0),
        'rwkv_w_up': nrm((L, DECAY_RANK, RWKV_WIDTH), DECAY_RANK ** -0.5),
        'rwkv_a0': nrm((L, RWKV_WIDTH), 0.5),
        'rwkv_a_up': nrm((L, ICLR_RANK, RWKV_WIDTH), ICLR_RANK ** -0.5),
        'rwkv_g_up': nrm((L, GATE_RANK, RWKV_WIDTH), GATE_RANK ** -0.5),
        'rwkv_k_k': 0.85 + nrm((L, RWKV_WIDTH), 0.05),
        'rwkv_k_a': 1.0 + nrm((L, RWKV_WIDTH), 0.05),
        'rwkv_r_k': nrm((L, RWKV_WIDTH), 0.1),
        'rwkv_gn_w': gain((L, RWKV_WIDTH)),
        'rwkv_gn_b': nrm((L, RWKV_WIDTH), 0.02),
        'norm_mix': gain((L, D_MODEL)),
        'norm_ffn': gain((L, D_MODEL)),
        'norm_ple': gain((L, D_MODEL)),
        'ple_proj': nrm((L, PLE_DIM, D_MODEL), PLE_DIM ** -0.5),
        'ple_gate': nrm((L, D_MODEL, D_MODEL), D_MODEL ** -0.5),
        'ffn_w1': nrm((N_DENSE, D_MODEL, D_FF), D_MODEL ** -0.5),
        'ffn_w3': nrm((N_DENSE, D_MODEL, D_FF), D_MODEL ** -0.5),
        'ffn_w2': nrm((N_DENSE, D_FF, D_MODEL), D_FF ** -0.5),
        'moe_router': nrm((N_MOE, D_MODEL, N_EXPERTS), D_MODEL ** -0.5),
        'moe_w1': nrm((N_MOE, N_EXPERTS, D_MODEL, D_FF_EXPERT), D_MODEL ** -0.5),
        'moe_w3': nrm((N_MOE, N_EXPERTS, D_MODEL, D_FF_EXPERT), D_MODEL ** -0.5),
        'moe_w2': nrm((N_MOE, N_EXPERTS, D_FF_EXPERT, D_MODEL), D_FF_EXPERT ** -0.5),
        'final_norm': gain((D_MODEL,)),
    }


def reference(x, p, w_in, att_up, rwkv_up, w_out, rel_bias, rwkv_mu, rwkv_w0, rwkv_w_up,
              rwkv_a0, rwkv_a_up, rwkv_g_up, rwkv_k_k, rwkv_k_a, rwkv_r_k, rwkv_gn_w, rwkv_gn_b,
              norm_mix, norm_ffn, norm_ple, ple_proj, ple_gate, ffn_w1, ffn_w3, ffn_w2,
              moe_router, moe_w1, moe_w3, moe_w2, final_norm):
    B, S = x.shape[0], x.shape[1]
    h = x
    for i in range(DEPTH):
        a = rms_norm(h, norm_mix[i])
        q, k, v, qi, ki, wi, slab, gate_a, gate_b = _split(a @ w_in[i], IN_SIZES)
        att = dsa_attention(q.reshape(B, S, ATT_HEADS, ATT_HEAD_DIM),
                            k.reshape(B, S, ATT_KV_HEADS, ATT_HEAD_DIM),
                            v.reshape(B, S, ATT_KV_HEADS, ATT_HEAD_DIM),
                            qi.reshape(B, S, IDX_HEADS, IDX_DIM), ki, wi, rel_bias)
        rw = rwkv7_mix(slab, rwkv_mu[i], rwkv_w0[i], rwkv_w_up[i], rwkv_a0[i], rwkv_a_up[i],
                       rwkv_g_up[i], rwkv_k_k[i], rwkv_k_a[i], rwkv_r_k[i], rwkv_gn_w[i], rwkv_gn_b[i])
        merged = jax.nn.sigmoid(gate_a) * (att @ att_up[i]) + jax.nn.sigmoid(gate_b) * (rw @ rwkv_up[i])
        h = h + merged @ w_out[i]
        f_in = rms_norm(h, norm_ffn[i])
        if i % 2 == 0:
            f = swiglu(f_in, ffn_w1[i // 2], ffn_w3[i // 2], ffn_w2[i // 2])
        else:
            f = moe_swiglu(f_in, moe_router[i // 2], moe_w1[i // 2], moe_w3[i // 2], moe_w2[i // 2])
        h = h + f
        ple_g = jax.nn.sigmoid(rms_norm(h, norm_ple[i]) @ ple_gate[i])
        h = h + ple_g * (p[i] @ ple_proj[i])
    return rms_norm(h, final_norm)
```

```python
import functools
import math

import jax
import jax.numpy as jnp
import numpy as np
from jax import lax
from jax.experimental import pallas as pl
from jax.experimental.pallas import tpu as pltpu

D_MODEL = 2048
DEPTH = 2
ATT_HEADS = 8
ATT_KV_HEADS = 2
ATT_HEAD_DIM = 128
IDX_HEADS = 8
IDX_DIM = 64
TOPK_MAX = 256
REL_BUCKETS = 32
REL_MAX_DIST = 128
RWKV_HEAD = 64
RWKV_HEADS = 16
DECAY_RANK = 64
ICLR_RANK = 64
GATE_RANK = 160
D_FF = 5632
N_EXPERTS = 8
TOP_K_EXPERTS = 2
D_FF_EXPERT = 7168
PLE_DIM = 256
RMS_EPS = 1e-6
GN_EPS = 64e-5

ATT_WIDTH = ATT_HEADS * ATT_HEAD_DIM
KV_WIDTH = ATT_KV_HEADS * ATT_HEAD_DIM
RWKV_WIDTH = RWKV_HEADS * RWKV_HEAD
RWKV_SIZES = (RWKV_WIDTH, RWKV_WIDTH, RWKV_WIDTH, DECAY_RANK, ICLR_RANK, GATE_RANK)
RWKV_SLAB = 3 * RWKV_WIDTH + DECAY_RANK + ICLR_RANK + GATE_RANK
IN_SIZES = (ATT_WIDTH, KV_WIDTH, KV_WIDTH, IDX_HEADS * IDX_DIM, IDX_DIM, IDX_HEADS, RWKV_SLAB, D_MODEL, D_MODEL)

LANES = 128
VMEM_LIMIT = 56 * 1024 * 1024

Q_BLOCK = 128
KEY_CHUNK = 512
BLOCKS_PER_CHUNK = KEY_CHUNK // Q_BLOCK
MASKED_LOGIT = -1e30
GQA_GROUP = ATT_HEADS // ATT_KV_HEADS

ZB_WIDTH = ATT_WIDTH + 2 * KV_WIDTH + IDX_HEADS * IDX_DIM + 2 * IDX_DIM
ZF_SLAB = LANES
ZF_SLAB_PAD = 3456
ZF_GATE_A = ZF_SLAB + ZF_SLAB_PAD
ZF_GATE_B = ZF_GATE_A + D_MODEL
ZF_WIDTH = ZF_GATE_B + D_MODEL

_F32 = jnp.float32
_BF16 = jnp.bfloat16
_NT = (((1,), (1,)), ((), ()))


def _split(z, sizes):
    return jnp.split(z, np.cumsum(sizes)[:-1].tolist(), axis=-1)


def _sigmoid(x):
    return 1.0 / (1.0 + jnp.exp(-x))


def _rms_to_bf16(x, g):
    y = x * lax.rsqrt(jnp.mean(x * x, axis=-1, keepdims=True) + RMS_EPS)
    return (y * g).astype(_BF16)


def _params(*sem):
    return pltpu.CompilerParams(dimension_semantics=sem, vmem_limit_bytes=VMEM_LIMIT)


def _norm_mm_kernel(x_ref, g_ref, w_ref, o_ref, xn_ref):
    @pl.when(pl.program_id(1) == 0)
    def _():
        xn_ref[...] = _rms_to_bf16(x_ref[...], g_ref[...])

    o_ref[...] = jnp.dot(xn_ref[...], w_ref[...], preferred_element_type=_F32).astype(o_ref.dtype)


def norm_matmul(x, g, w, out_dtype, tm, tn):
    T, D = x.shape
    N = w.shape[1]
    return pl.pallas_call(
        _norm_mm_kernel,
        grid=(T // tm, N // tn),
        in_specs=[pl.BlockSpec((tm, D), lambda i, j: (i, 0)),
                  pl.BlockSpec((1, D), lambda i, j: (0, 0)),
                  pl.BlockSpec((D, tn), lambda i, j: (0, j))],
        out_specs=pl.BlockSpec((tm, tn), lambda i, j: (i, j)),
        out_shape=jax.ShapeDtypeStruct((T, N), out_dtype),
        scratch_shapes=[pltpu.VMEM((tm, D), _BF16)],
        compiler_params=_params("parallel", "arbitrary"),
        name="norm_matmul",
    )(x, g.reshape(1, D), w)


def _merge_kernel(att_ref, rw_ref, ga_ref, gb_ref, wa_ref, wr_ref, o_ref):
    a = jnp.dot(att_ref[...], wa_ref[...], preferred_element_type=_F32)
    r = jnp.dot(rw_ref[...], wr_ref[...], preferred_element_type=_F32)
    o_ref[...] = (_sigmoid(ga_ref[...]) * a + _sigmoid(gb_ref[...]) * r).astype(o_ref.dtype)


def gated_merge(att, rw, zf, wa, wr, tm, tn):
    T, K = att.shape
    N = wa.shape[1]
    ja, jb = ZF_GATE_A // tn, ZF_GATE_B // tn
    return pl.pallas_call(
        _merge_kernel,
        grid=(T // tm, N // tn),
        in_specs=[pl.BlockSpec((tm, K), lambda i, j: (i, 0)),
                  pl.BlockSpec((tm, K), lambda i, j: (i, 0)),
                  pl.BlockSpec((tm, tn), lambda i, j: (i, ja + j)),
                  pl.BlockSpec((tm, tn), lambda i, j: (i, jb + j)),
                  pl.BlockSpec((K, tn), lambda i, j: (0, j)),
                  pl.BlockSpec((K, tn), lambda i, j: (0, j))],
        out_specs=pl.BlockSpec((tm, tn), lambda i, j: (i, j)),
        out_shape=jax.ShapeDtypeStruct((T, N), _BF16),
        compiler_params=_params("parallel", "arbitrary"),
        name="gated_merge",
    )(att, rw, zf, zf, wa, wr)


def _mm_res_kernel(x_ref, w_ref, h_ref, o_ref):
    o_ref[...] = h_ref[...] + jnp.dot(x_ref[...], w_ref[...], preferred_element_type=_F32)


def matmul_residual(x, w, h, tm, tn):
    T, K = x.shape
    N = w.shape[1]
    return pl.pallas_call(
        _mm_res_kernel,
        grid=(T // tm, N // tn),
        in_specs=[pl.BlockSpec((tm, K), lambda i, j: (i, 0)),
                  pl.BlockSpec((K, tn), lambda i, j: (0, j)),
                  pl.BlockSpec((tm, tn), lambda i, j: (i, j))],
        out_specs=pl.BlockSpec((tm, tn), lambda i, j: (i, j)),
        out_shape=jax.ShapeDtypeStruct((T, N), _F32),
        compiler_params=_params("parallel", "arbitrary"),
        name="matmul_residual",
    )(x, w, h)


def _ffn_kernel(h_ref, g_ref, w1_ref, w3_ref, w2_ref, o_ref, xn_ref):
    f = pl.program_id(1)

    @pl.when(f == 0)
    def _():
        xn_ref[...] = _rms_to_bf16(h_ref[...], g_ref[...])
        o_ref[...] = h_ref[...]

    xn = xn_ref[...]
    a = jnp.dot(xn, w1_ref[...], preferred_element_type=_F32)
    b = jnp.dot(xn, w3_ref[...], preferred_element_type=_F32)
    mid = (a * _sigmoid(a) * b).astype(_BF16)
    o_ref[...] += jnp.dot(mid, w2_ref[...], preferred_element_type=_F32)


def ffn_residual(h, g, w1, w3, w2, tm, tf):
    T, D = h.shape
    F = w1.shape[1]
    return pl.pallas_call(
        _ffn_kernel,
        grid=(T // tm, F // tf),
        in_specs=[pl.BlockSpec((tm, D), lambda i, f: (i, 0)),
                  pl.BlockSpec((1, D), lambda i, f: (0, 0)),
                  pl.BlockSpec((D, tf), lambda i, f: (0, f)),
                  pl.BlockSpec((D, tf), lambda i, f: (0, f)),
                  pl.BlockSpec((tf, D), lambda i, f: (f, 0))],
        out_specs=pl.BlockSpec((tm, D), lambda i, f: (i, 0)),
        out_shape=jax.ShapeDtypeStruct((T, D), _F32),
        scratch_shapes=[pltpu.VMEM((tm, D), _BF16)],
        compiler_params=_params("parallel", "arbitrary"),
        name="ffn_residual",
    )(h, g.reshape(1, D), w1, w3, w2)


def _ple_kernel(h_ref, g_ref, p_ref, wg_ref, wp_ref, fg_ref, o_ref, *, final):
    h = h_ref[...]
    xn = _rms_to_bf16(h, g_ref[...])
    gate = _sigmoid(jnp.dot(xn, wg_ref[...], preferred_element_type=_F32))
    pp = jnp.dot(p_ref[...].astype(_BF16), wp_ref[...], preferred_element_type=_F32)
    out = h + gate * pp
    if final:
        out = out * lax.rsqrt(jnp.mean(out * out, axis=-1, keepdims=True) + RMS_EPS) * fg_ref[...]
    o_ref[...] = out


def ple_residual(h, g, p, wg, wp, final_g, final, tm):
    T, D = h.shape
    P = p.shape[1]
    return pl.pallas_call(
        functools.partial(_ple_kernel, final=final),
        grid=(T // tm,),
        in_specs=[pl.BlockSpec((tm, D), lambda i: (i, 0)),
                  pl.BlockSpec((1, D), lambda i: (0, 0)),
                  pl.BlockSpec((tm, P), lambda i: (i, 0)),
                  pl.BlockSpec((D, D), lambda i: (0, 0)),
                  pl.BlockSpec((P, D), lambda i: (0, 0)),
                  pl.BlockSpec((1, D), lambda i: (0, 0))],
        out_specs=pl.BlockSpec((tm, D), lambda i: (i, 0)),
        out_shape=jax.ShapeDtypeStruct((T, D), _F32),
        compiler_params=_params("parallel"),
        name="ple_residual",
    )(h, g.reshape(1, D), p, wg, wp, final_g.reshape(1, D))


def rel_bucket(dist):
    max_exact = REL_BUCKETS // 2
    dist = jnp.maximum(dist, 0)
    d_f = jnp.maximum(dist, 1).astype(_F32)
    large = max_exact + (jnp.log(d_f / max_exact) / math.log(REL_MAX_DIST / max_exact)
                         * (REL_BUCKETS - max_exact)).astype(jnp.int32)
    large = jnp.minimum(large, REL_BUCKETS - 1)
    return jnp.where(dist < max_exact, dist, large)


def _sortable(x):
    b = lax.bitcast_convert_type(x, jnp.int32)
    return b ^ ((b >> 31) & jnp.int32(0x7FFFFFFF))


_INT_MIN = -2 ** 31
_KEY_NEG_INF = 0x807FFFFF - 2 ** 32


def _dsa_kernel(q_ref, qi_ref, wi_ref, k_ref, v_ref, ki_ref, bprev_ref, bdiag_ref, bfar_ref, o_ref,
                keys_ref, qs_ref, qim_ref, m_ref, l_ref, acc_ref, *, n_sel, idx_bits):
    qb = pl.program_id(1)
    t0 = qb * Q_BLOCK
    n_chunks = qb // BLOCKS_PER_CHUNK + 1
    t_idx = t0 + lax.broadcasted_iota(jnp.int32, (Q_BLOCK, 1), 0)
    lane_blk = lax.broadcasted_iota(jnp.int32, (1, Q_BLOCK), 1)
    lane_chunk = lax.broadcasted_iota(jnp.int32, (1, KEY_CHUNK), 1)

    for hh in range(ATT_HEADS):
        g, h = divmod(hh, GQA_GROUP)
        qs_ref[g, h * Q_BLOCK:(h + 1) * Q_BLOCK, :] = q_ref[:, hh * ATT_HEAD_DIM:(hh + 1) * ATT_HEAD_DIM]
    low = lax.broadcasted_iota(jnp.int32, (Q_BLOCK, LANES), 1) < IDX_DIM
    for hh in range(IDX_HEADS):
        tile = qi_ref[:, (hh // 2) * LANES:(hh // 2 + 1) * LANES]
        keep = low if hh % 2 == 0 else jnp.logical_not(low)
        qim_ref[hh] = jnp.where(keep, tile, jnp.zeros_like(tile))
    m_ref[...] = jnp.full(m_ref.shape, MASKED_LOGIT, _F32)
    l_ref[...] = jnp.zeros(l_ref.shape, _F32)
    acc_ref[...] = jnp.zeros(acc_ref.shape, _F32)

    w_all = wi_ref[...]
    idx_scale = (IDX_HEADS ** -0.5) * (IDX_DIM ** -0.5)

    def score_body(c, carry):
        ks = pl.multiple_of(c * KEY_CHUNK, KEY_CHUNK)
        kic = ki_ref[pl.ds(ks, KEY_CHUNK), :]
        sc = jnp.zeros((Q_BLOCK, KEY_CHUNK), _F32)
        for hh in range(IDX_HEADS):
            d = lax.dot_general(qim_ref[hh], kic, _NT, preferred_element_type=_F32)
            sc = sc + w_all[:, IDX_DIM + hh:IDX_DIM + hh + 1] * jnp.maximum(d, 0.0)
        sc = sc * idx_scale
        sc = jnp.where(sc == 0.0, 0.0, sc)
        key = jnp.where(ks + lane_chunk <= t_idx, _sortable(sc), _KEY_NEG_INF)
        for j in range(BLOCKS_PER_CHUNK):
            keys_ref[c * BLOCKS_PER_CHUNK + j] = key[:, j * Q_BLOCK:(j + 1) * Q_BLOCK]
        return carry

    lax.fori_loop(0, n_chunks, score_body, 0)

    def count(pred):
        def body(c, acc):
            for j in range(BLOCKS_PER_CHUNK):
                blk = c * BLOCKS_PER_CHUNK + j
                acc = acc + jnp.where(pred(keys_ref[blk], blk * Q_BLOCK + lane_blk), 1.0, 0.0)
            return acc

        acc = lax.fori_loop(0, n_chunks, body, jnp.zeros((Q_BLOCK, Q_BLOCK), _F32))
        return jnp.sum(acc, axis=1, keepdims=True)

    def value_bit(i, tb):
        cand = tb | lax.shift_left(jnp.int32(1), 31 - i)
        cand_signed = cand ^ jnp.int32(_INT_MIN)
        cnt = count(lambda key, sidx: key >= cand_signed)
        return jnp.where(cnt >= n_sel, cand, tb)

    tau = lax.fori_loop(0, 32, value_bit, jnp.zeros((Q_BLOCK, 1), jnp.int32)) ^ jnp.int32(_INT_MIN)
    need = n_sel - count(lambda key, sidx: key > tau)

    def index_bit(i, x):
        cand = x | lax.shift_left(jnp.int32(1), idx_bits - 1 - i)
        cnt = count(lambda key, sidx: jnp.logical_and(key == tau, sidx < cand))
        return jnp.where(cnt < need, cand, x)

    last_tie = lax.fori_loop(0, idx_bits, index_bit, jnp.zeros((Q_BLOCK, 1), jnp.int32))

    def attend(key, k_tile, v_tile, s0, lane, bias, enable=None):
        sidx = s0 + lane
        sel = jnp.logical_or(key > tau, jnp.logical_and(key == tau, sidx <= last_tie))
        sel = jnp.logical_and(sel, key > _KEY_NEG_INF)
        if enable is not None:
            sel = jnp.logical_and(sel, jnp.zeros_like(sidx) + enable > 0)
        for g in range(ATT_KV_HEADS):
            s = lax.dot_general(qs_ref[g], k_tile[:, g * ATT_HEAD_DIM:(g + 1) * ATT_HEAD_DIM], _NT,
                                preferred_element_type=_F32) * (ATT_HEAD_DIM ** -0.5)
            ps, alphas = [], []
            for h in range(GQA_GROUP):
                rows = slice(h * Q_BLOCK, (h + 1) * Q_BLOCK)
                sh = jnp.where(sel, s[rows] + bias(g * GQA_GROUP + h), MASKED_LOGIT)
                m_old = m_ref[g, rows]
                m_new = jnp.maximum(m_old, jnp.max(sh, axis=1, keepdims=True))
                alpha = jnp.exp(m_old - m_new)
                p = jnp.exp(sh - m_new)
                l_ref[g, rows] = alpha * l_ref[g, rows] + jnp.sum(p, axis=1, keepdims=True)
                m_ref[g, rows] = m_new
                ps.append(p.astype(_BF16))
                alphas.append(alpha)
            pv = jnp.dot(jnp.concatenate(ps, axis=0), v_tile[:, g * ATT_HEAD_DIM:(g + 1) * ATT_HEAD_DIM],
                         preferred_element_type=_F32)
            acc_ref[g] = acc_ref[g] * jnp.concatenate(alphas, axis=0) + pv

    def far_bias(hh):
        return bfar_ref[hh]

    n_far = jnp.maximum(qb - 1, 0) // BLOCKS_PER_CHUNK

    def far_body(c, carry):
        ks = pl.multiple_of(c * KEY_CHUNK, KEY_CHUNK)
        key = jnp.concatenate([keys_ref[c * BLOCKS_PER_CHUNK + j] for j in range(BLOCKS_PER_CHUNK)], axis=1)
        attend(key, k_ref[pl.ds(ks, KEY_CHUNK), :], v_ref[pl.ds(ks, KEY_CHUNK), :], ks, lane_chunk, far_bias)
        return carry

    lax.fori_loop(0, n_far, far_body, 0)

    def block_step(j, bias, enable=None):
        ks = pl.multiple_of(j * Q_BLOCK, Q_BLOCK)
        attend(keys_ref[j], k_ref[pl.ds(ks, Q_BLOCK), :], v_ref[pl.ds(ks, Q_BLOCK), :], ks, lane_blk, bias, enable)

    def tail_body(j, carry):
        block_step(j, far_bias)
        return carry

    lax.fori_loop(n_far * BLOCKS_PER_CHUNK, qb - 1, tail_body, 0)
    block_step(jnp.maximum(qb - 1, 0), lambda hh: bprev_ref[hh], enable=jnp.minimum(qb, 1))
    block_step(qb, lambda hh: bdiag_ref[hh])

    for hh in range(ATT_HEADS):
        g, h = divmod(hh, GQA_GROUP)
        rows = slice(h * Q_BLOCK, (h + 1) * Q_BLOCK)
        o_ref[:, hh * ATT_HEAD_DIM:(hh + 1) * ATT_HEAD_DIM] = (acc_ref[g, rows] / l_ref[g, rows]).astype(o_ref.dtype)


def dsa_attention(zb, zf, rel_bias, B, S):
    nq = S // Q_BLOCK
    n_sel = min(TOPK_MAX, S // 4)
    tl = jnp.arange(Q_BLOCK, dtype=jnp.int32)[:, None]
    sr = jnp.arange(Q_BLOCK, dtype=jnp.int32)[None, :]
    bprev = rel_bias[rel_bucket(tl + Q_BLOCK - sr)].transpose(2, 0, 1).astype(_F32)
    bdiag = rel_bias[rel_bucket(tl - sr)].transpose(2, 0, 1).astype(_F32)
    bfar = rel_bias[rel_bucket(jnp.int32(REL_MAX_DIST + 1))].astype(_F32)
    kernel = functools.partial(_dsa_kernel, n_sel=float(n_sel), idx_bits=int(S - 1).bit_length())
    c_k = ATT_WIDTH // KV_WIDTH
    c_qi = (ATT_WIDTH + 2 * KV_WIDTH) // (IDX_HEADS * IDX_DIM)
    c_ki = (ATT_WIDTH + 2 * KV_WIDTH + IDX_HEADS * IDX_DIM) // LANES
    return pl.pallas_call(
        kernel,
        grid=(B, nq),
        in_specs=[pl.BlockSpec((Q_BLOCK, ATT_WIDTH), lambda b, i: (b * nq + i, 0)),
                  pl.BlockSpec((Q_BLOCK, IDX_HEADS * IDX_DIM), lambda b, i: (b * nq + i, c_qi)),
                  pl.BlockSpec((Q_BLOCK, LANES), lambda b, i: (b * nq + i, 0)),
                  pl.BlockSpec((S, KV_WIDTH), lambda b, i: (b, c_k)),
                  pl.BlockSpec((S, KV_WIDTH), lambda b, i: (b, c_k + 1)),
                  pl.BlockSpec((S, LANES), lambda b, i: (b, c_ki)),
                  pl.BlockSpec((ATT_HEADS, Q_BLOCK, Q_BLOCK), lambda b, i: (0, 0, 0)),
                  pl.BlockSpec((ATT_HEADS, Q_BLOCK, Q_BLOCK), lambda b, i: (0, 0, 0)),
                  pl.BlockSpec(memory_space=pltpu.SMEM)],
        out_specs=pl.BlockSpec((Q_BLOCK, ATT_WIDTH), lambda b, i: (b * nq + i, 0)),
        out_shape=jax.ShapeDtypeStruct((B * S, ATT_WIDTH), _BF16),
        scratch_shapes=[pltpu.VMEM((nq, Q_BLOCK, Q_BLOCK), jnp.int32),
                        pltpu.VMEM((ATT_KV_HEADS, GQA_GROUP * Q_BLOCK, ATT_HEAD_DIM), _BF16),
                        pltpu.VMEM((IDX_HEADS, Q_BLOCK, LANES), _BF16),
                        pltpu.VMEM((ATT_KV_HEADS, GQA_GROUP * Q_BLOCK, 1), _F32),
                        pltpu.VMEM((ATT_KV_HEADS, GQA_GROUP * Q_BLOCK, 1), _F32),
                        pltpu.VMEM((ATT_KV_HEADS, GQA_GROUP * Q_BLOCK, ATT_HEAD_DIM), _F32)],
        compiler_params=_params("parallel", "arbitrary"),
        name="dsa_attention",
    )(zb, zb, zf, zb, zb, zb, bprev, bdiag, bfar)


def token_shift(z):
    return jnp.pad(z, ((0, 0), (1, 0), (0, 0)))[:, :-1]


def rwkv7_mix(slab, mu, w0, w_up, a0, a_up, g_up, k_k, k_a, r_k, gn_w, gn_b):
    B, S = slab.shape[0], slab.shape[1]
    xs = slab + (token_shift(slab) - slab) * mu
    r, k, v, xw, xa, xg = _split(xs, RWKV_SIZES)
    d = (w0 + jnp.tanh(xw) @ w_up).astype(_F32)
    decay = jnp.exp(-math.exp(-0.5) * jax.nn.sigmoid(d))
    a = jax.nn.sigmoid((a0 + xa @ a_up).astype(_F32))
    g = jax.nn.sigmoid(xg) @ g_up

    def heads(z):
        return z.astype(_F32).reshape(B, S, RWKV_HEADS, RWKV_HEAD)

    kk = heads(k * k_k)
    kk = kk * lax.rsqrt(jnp.maximum(jnp.sum(kk * kk, axis=-1, keepdims=True), 1e-24))
    kt = heads(k.astype(_F32) * (1.0 + (a - 1.0) * k_a.astype(_F32)))
    r_h = heads(r)
    v_h = heads(v)
    w_h = heads(decay)
    b_h = kk * heads(a)

    def step(state, inp):
        r_t, w_t, k_t, v_t, kk_t, b_t = inp
        sk = jnp.einsum('bhvk,bhk->bhv', state, kk_t)
        state = (state * w_t[:, :, None, :] - sk[..., None] * b_t[:, :, None, :]
                 + v_t[..., None] * k_t[:, :, None, :])
        y = jnp.einsum('bhvk,bhk->bhv', state, r_t)
        return state, y

    s0 = jnp.zeros((B, RWKV_HEADS, RWKV_HEAD, RWKV_HEAD), _F32)
    xs_seq = (r_h.swapaxes(0, 1), w_h.swapaxes(0, 1), kt.swapaxes(0, 1),
              v_h.swapaxes(0, 1), kk.swapaxes(0, 1), b_h.swapaxes(0, 1))
    _, y = lax.scan(step, s0, xs_seq)
    y = y.swapaxes(0, 1)
    mean = jnp.mean(y, axis=-1, keepdims=True)
    var = jnp.mean(jnp.square(y - mean), axis=-1, keepdims=True)
    y = ((y - mean) * lax.rsqrt(var + GN_EPS)).reshape(B, S, RWKV_WIDTH)
    y = y * gn_w.astype(_F32) + gn_b.astype(_F32)
    bonus = jnp.sum(r_h * kt * r_k.astype(_F32).reshape(RWKV_HEADS, RWKV_HEAD),
                    axis=-1, keepdims=True) * v_h
    y = (y + bonus.reshape(B, S, RWKV_WIDTH)) * g.astype(_F32)
    return y.astype(slab.dtype)


def swiglu(x, w1, w3, w2):
    return (jax.nn.silu(x @ w1) * (x @ w3)) @ w2


def moe_swiglu(xf, router, w1, w3, w2):
    logits = (xf @ router).astype(_F32)
    top_vals, top_idx = lax.top_k(logits, TOP_K_EXPERTS)
    gates = jax.nn.softmax(top_vals, axis=-1)
    combine = jnp.sum(jax.nn.one_hot(top_idx, N_EXPERTS, dtype=_F32) * gates[..., None], axis=1)
    out = jnp.zeros(xf.shape, _F32)
    for e in range(N_EXPERTS):
        out = out + combine[:, e:e + 1] * swiglu(xf, w1[e], w3[e], w2[e]).astype(_F32)
    return out


def rms_norm(x, g):
    y = x * lax.rsqrt(jnp.mean(x * x, axis=-1, keepdims=True) + RMS_EPS)
    return y * g


def _pack_w_in(w):
    wq, wk, wv, wqi, wki, wwi, wslab, wga, wgb = _split(w, IN_SIZES)
    D = w.shape[0]
    wb = jnp.concatenate([wq, wk, wv, wqi, wki, wki], axis=1)
    wf = jnp.concatenate([wki, wwi, jnp.zeros((D, LANES - IDX_DIM - IDX_HEADS), w.dtype),
                          wslab, jnp.zeros((D, ZF_SLAB_PAD - RWKV_SLAB), w.dtype), wga, wgb], axis=1)
    return wb.astype(_BF16), wf.astype(_BF16)


def kernel(x, p, w_in, att_up, rwkv_up, w_out, rel_bias, rwkv_mu, rwkv_w0, rwkv_w_up,
           rwkv_a0, rwkv_a_up, rwkv_g_up, rwkv_k_k, rwkv_k_a, rwkv_r_k, rwkv_gn_w, rwkv_gn_b,
           norm_mix, norm_ffn, norm_ple, ple_proj, ple_gate, ffn_w1, ffn_w3, ffn_w2,
           moe_router, moe_w1, moe_w3, moe_w2, final_norm):
    B, S, D = x.shape
    T = B * S
    depth = w_in.shape[0]
    h = x.reshape(T, D)
    for i in range(depth):
        wb, wf = _pack_w_in(w_in[i])
        zb = norm_matmul(h, norm_mix[i], wb, _BF16, tm=512, tn=ZB_WIDTH)
        zf = norm_matmul(h, norm_mix[i], wf, _F32, tm=512, tn=1536)
        att = dsa_attention(zb, zf, rel_bias, B, S)
        slab = zf[:, ZF_SLAB:ZF_SLAB + RWKV_SLAB].reshape(B, S, RWKV_SLAB)
        rw = rwkv7_mix(slab, rwkv_mu[i], rwkv_w0[i], rwkv_w_up[i], rwkv_a0[i], rwkv_a_up[i],
                       rwkv_g_up[i], rwkv_k_k[i], rwkv_k_a[i], rwkv_r_k[i], rwkv_gn_w[i], rwkv_gn_b[i])
        merged = gated_merge(att, rw.reshape(T, RWKV_WIDTH).astype(_BF16), zf,
                             att_up[i].astype(_BF16), rwkv_up[i].astype(_BF16), tm=512, tn=512)
        h = matmul_residual(merged, w_out[i].astype(_BF16), h, tm=512, tn=1024)
        if i % 2 == 0:
            j = i // 2
            h = ffn_residual(h, norm_ffn[i], ffn_w1[j].astype(_BF16), ffn_w3[j].astype(_BF16),
                             ffn_w2[j].astype(_BF16), tm=512, tf=512)
        else:
            j = i // 2
            h = h + moe_swiglu(rms_norm(h, norm_ffn[i]), moe_router[j], moe_w1[j], moe_w3[j], moe_w2[j])
        h = ple_residual(h, norm_ple[i], p[i].reshape(T, PLE_DIM), ple_gate[i].astype(_BF16),
                         ple_proj[i].astype(_BF16), final_norm, final=(i == depth - 1), tm=256)
    return h.reshape(B, S, D)
```

```python
import functools
import math

import jax
import jax.numpy as jnp
import numpy as np
from jax import lax
from jax.experimental import pallas as pl
from jax.experimental.pallas import tpu as pltpu

D_MODEL = 2048
DEPTH = 2
ATT_HEADS = 8
ATT_KV_HEADS = 2
ATT_HEAD_DIM = 128
IDX_HEADS = 8
IDX_DIM = 64
TOPK_MAX = 256
REL_BUCKETS = 32
REL_MAX_DIST = 128
RWKV_HEAD = 64
RWKV_HEADS = 16
DECAY_RANK = 64
ICLR_RANK = 64
GATE_RANK = 160
D_FF = 5632
N_EXPERTS = 8
TOP_K_EXPERTS = 2
D_FF_EXPERT = 7168
PLE_DIM = 256
RMS_EPS = 1e-6
GN_EPS = 64e-5

ATT_WIDTH = ATT_HEADS * ATT_HEAD_DIM
KV_WIDTH = ATT_KV_HEADS * ATT_HEAD_DIM
RWKV_WIDTH = RWKV_HEADS * RWKV_HEAD
RWKV_SIZES = (RWKV_WIDTH, RWKV_WIDTH, RWKV_WIDTH, DECAY_RANK, ICLR_RANK, GATE_RANK)
RWKV_SLAB = 3 * RWKV_WIDTH + DECAY_RANK + ICLR_RANK + GATE_RANK
IN_SIZES = (ATT_WIDTH, KV_WIDTH, KV_WIDTH, IDX_HEADS * IDX_DIM, IDX_DIM, IDX_HEADS, RWKV_SLAB, D_MODEL, D_MODEL)

LANES = 128
VMEM_LIMIT = 56 * 1024 * 1024

Q_BLOCK = 128
KEY_CHUNK = 512
BLOCKS_PER_CHUNK = KEY_CHUNK // Q_BLOCK
MASKED_LOGIT = -1e30
GQA_GROUP = ATT_HEADS // ATT_KV_HEADS

ZB_WIDTH = ATT_WIDTH + 2 * KV_WIDTH + IDX_HEADS * IDX_DIM + 2 * IDX_DIM
ZF_SLAB_PAD = 3456
ZF_WIDX = ZF_SLAB_PAD
ZF_GATE_A = ZF_WIDX + LANES
ZF_GATE_B = ZF_GATE_A + D_MODEL
ZF_WIDTH = ZF_GATE_B + D_MODEL

RWKV_CHUNK = 64
HEAD_PAIRS = RWKV_HEADS // 2
MOE_TM = 512
MOE_TF = 512

_F32 = jnp.float32
_BF16 = jnp.bfloat16
_NT = (((1,), (1,)), ((), ()))
_TN = (((0,), (0,)), ((), ()))


def _split(z, sizes):
    return jnp.split(z, np.cumsum(sizes)[:-1].tolist(), axis=-1)


def _sigmoid(x):
    return 1.0 / (1.0 + jnp.exp(-x))


def _rms_to_bf16(x, g):
    y = x * lax.rsqrt(jnp.mean(x * x, axis=-1, keepdims=True) + RMS_EPS)
    return (y * g).astype(_BF16)


def _params(*sem):
    return pltpu.CompilerParams(dimension_semantics=sem, vmem_limit_bytes=VMEM_LIMIT)


def _norm_mm_kernel(x_ref, g_ref, w_ref, o_ref, xn_ref):
    @pl.when(pl.program_id(1) == 0)
    def _():
        xn_ref[...] = _rms_to_bf16(x_ref[...], g_ref[...])

    o_ref[...] = jnp.dot(xn_ref[...], w_ref[...], preferred_element_type=_F32).astype(o_ref.dtype)


def norm_matmul(x, g, w, out_dtype, tm, tn):
    T, D = x.shape
    N = w.shape[1]
    return pl.pallas_call(
        _norm_mm_kernel,
        grid=(T // tm, N // tn),
        in_specs=[pl.BlockSpec((tm, D), lambda i, j: (i, 0)),
                  pl.BlockSpec((1, D), lambda i, j: (0, 0)),
                  pl.BlockSpec((D, tn), lambda i, j: (0, j))],
        out_specs=pl.BlockSpec((tm, tn), lambda i, j: (i, j)),
        out_shape=jax.ShapeDtypeStruct((T, N), out_dtype),
        scratch_shapes=[pltpu.VMEM((tm, D), _BF16)],
        compiler_params=_params("parallel", "arbitrary"),
        name="norm_matmul",
    )(x, g.reshape(1, D), w)


def _merge_kernel(att_ref, rw_ref, ga_ref, gb_ref, wa_ref, wr_ref, o_ref):
    a = jnp.dot(att_ref[...], wa_ref[...], preferred_element_type=_F32)
    r = jnp.dot(rw_ref[...], wr_ref[...], preferred_element_type=_F32)
    o_ref[...] = (_sigmoid(ga_ref[...]) * a + _sigmoid(gb_ref[...]) * r).astype(o_ref.dtype)


def gated_merge(att, rw, zf, wa, wr, tm, tn):
    T, K = att.shape
    N = wa.shape[1]
    ja, jb = ZF_GATE_A // tn, ZF_GATE_B // tn
    return pl.pallas_call(
        _merge_kernel,
        grid=(T // tm, N // tn),
        in_specs=[pl.BlockSpec((tm, K), lambda i, j: (i, 0)),
                  pl.BlockSpec((tm, K), lambda i, j: (i, 0)),
                  pl.BlockSpec((tm, tn), lambda i, j: (i, ja + j)),
                  pl.BlockSpec((tm, tn), lambda i, j: (i, jb + j)),
                  pl.BlockSpec((K, tn), lambda i, j: (0, j)),
                  pl.BlockSpec((K, tn), lambda i, j: (0, j))],
        out_specs=pl.BlockSpec((tm, tn), lambda i, j: (i, j)),
        out_shape=jax.ShapeDtypeStruct((T, N), _BF16),
        compiler_params=_params("parallel", "arbitrary"),
        name="gated_merge",
    )(att, rw, zf, zf, wa, wr)


def _mm_res_kernel(x_ref, w_ref, h_ref, o_ref):
    o_ref[...] = h_ref[...] + jnp.dot(x_ref[...], w_ref[...], preferred_element_type=_F32)


def matmul_residual(x, w, h, tm, tn):
    T, K = x.shape
    N = w.shape[1]
    return pl.pallas_call(
        _mm_res_kernel,
        grid=(T // tm, N // tn),
        in_specs=[pl.BlockSpec((tm, K), lambda i, j: (i, 0)),
                  pl.BlockSpec((K, tn), lambda i, j: (0, j)),
                  pl.BlockSpec((tm, tn), lambda i, j: (i, j))],
        out_specs=pl.BlockSpec((tm, tn), lambda i, j: (i, j)),
        out_shape=jax.ShapeDtypeStruct((T, N), _F32),
        compiler_params=_params("parallel", "arbitrary"),
        name="matmul_residual",
    )(x, w, h)


def _ffn_kernel(h_ref, g_ref, w1_ref, w3_ref, w2_ref, o_ref, xn_ref):
    f = pl.program_id(1)

    @pl.when(f == 0)
    def _():
        xn_ref[...] = _rms_to_bf16(h_ref[...], g_ref[...])
        o_ref[...] = h_ref[...]

    xn = xn_ref[...]
    a = jnp.dot(xn, w1_ref[...], preferred_element_type=_F32)
    b = jnp.dot(xn, w3_ref[...], preferred_element_type=_F32)
    mid = (a * _sigmoid(a) * b).astype(_BF16)
    o_ref[...] += jnp.dot(mid, w2_ref[...], preferred_element_type=_F32)


def ffn_residual(h, g, w1, w3, w2, tm, tf):
    T, D = h.shape
    F = w1.shape[1]
    return pl.pallas_call(
        _ffn_kernel,
        grid=(T // tm, F // tf),
        in_specs=[pl.BlockSpec((tm, D), lambda i, f: (i, 0)),
                  pl.BlockSpec((1, D), lambda i, f: (0, 0)),
                  pl.BlockSpec((D, tf), lambda i, f: (0, f)),
                  pl.BlockSpec((D, tf), lambda i, f: (0, f)),
                  pl.BlockSpec((tf, D), lambda i, f: (f, 0))],
        out_specs=pl.BlockSpec((tm, D), lambda i, f: (i, 0)),
        out_shape=jax.ShapeDtypeStruct((T, D), _F32),
        scratch_shapes=[pltpu.VMEM((tm, D), _BF16)],
        compiler_params=_params("parallel", "arbitrary"),
        name="ffn_residual",
    )(h, g.reshape(1, D), w1, w3, w2)


def _ple_kernel(h_ref, g_ref, p_ref, wg_ref, wp_ref, fg_ref, o_ref, *, final):
    h = h_ref[...]
    xn = _rms_to_bf16(h, g_ref[...])
    gate = _sigmoid(jnp.dot(xn, wg_ref[...], preferred_element_type=_F32))
    pp = jnp.dot(p_ref[...].astype(_BF16), wp_ref[...], preferred_element_type=_F32)
    out = h + gate * pp
    if final:
        out = out * lax.rsqrt(jnp.mean(out * out, axis=-1, keepdims=True) + RMS_EPS) * fg_ref[...]
    o_ref[...] = out


def ple_residual(h, g, p, wg, wp, final_g, final, tm):
    T, D = h.shape
    P = p.shape[1]
    return pl.pallas_call(
        functools.partial(_ple_kernel, final=final),
        grid=(T // tm,),
        in_specs=[pl.BlockSpec((tm, D), lambda i: (i, 0)),
                  pl.BlockSpec((1, D), lambda i: (0, 0)),
                  pl.BlockSpec((tm, P), lambda i: (i, 0)),
                  pl.BlockSpec((D, D), lambda i: (0, 0)),
                  pl.BlockSpec((P, D), lambda i: (0, 0)),
                  pl.BlockSpec((1, D), lambda i: (0, 0))],
        out_specs=pl.BlockSpec((tm, D), lambda i: (i, 0)),
        out_shape=jax.ShapeDtypeStruct((T, D), _F32),
        compiler_params=_params("parallel"),
        name="ple_residual",
    )(h, g.reshape(1, D), p, wg, wp, final_g.reshape(1, D))


def rel_bucket(dist):
    max_exact = REL_BUCKETS // 2
    dist = jnp.maximum(dist, 0)
    d_f = jnp.maximum(dist, 1).astype(_F32)
    large = max_exact + (jnp.log(d_f / max_exact) / math.log(REL_MAX_DIST / max_exact)
                         * (REL_BUCKETS - max_exact)).astype(jnp.int32)
    large = jnp.minimum(large, REL_BUCKETS - 1)
    return jnp.where(dist < max_exact, dist, large)


def _sortable(x):
    b = lax.bitcast_convert_type(x, jnp.int32)
    return b ^ ((b >> 31) & jnp.int32(0x7FFFFFFF))


_INT_MIN = -2 ** 31
_KEY_NEG_INF = 0x807FFFFF - 2 ** 32


def _dsa_kernel(q_ref, qi_ref, wi_ref, k_ref, v_ref, ki_ref, bprev_ref, bdiag_ref, bfar_ref, o_ref,
                keys_ref, qs_ref, qim_ref, m_ref, l_ref, acc_ref, *, n_sel, idx_bits):
    qb = pl.program_id(1)
    t0 = qb * Q_BLOCK
    n_chunks = qb // BLOCKS_PER_CHUNK + 1
    t_idx = t0 + lax.broadcasted_iota(jnp.int32, (Q_BLOCK, 1), 0)
    lane_blk = lax.broadcasted_iota(jnp.int32, (1, Q_BLOCK), 1)
    lane_chunk = lax.broadcasted_iota(jnp.int32, (1, KEY_CHUNK), 1)

    for hh in range(ATT_HEADS):
        g, h = divmod(hh, GQA_GROUP)
        qs_ref[g, h * Q_BLOCK:(h + 1) * Q_BLOCK, :] = q_ref[:, hh * ATT_HEAD_DIM:(hh + 1) * ATT_HEAD_DIM]
    low = lax.broadcasted_iota(jnp.int32, (Q_BLOCK, LANES), 1) < IDX_DIM
    for hh in range(IDX_HEADS):
        tile = qi_ref[:, (hh // 2) * LANES:(hh // 2 + 1) * LANES]
        keep = low if hh % 2 == 0 else jnp.logical_not(low)
        qim_ref[hh] = jnp.where(keep, tile, jnp.zeros_like(tile))
    m_ref[...] = jnp.full(m_ref.shape, MASKED_LOGIT, _F32)
    l_ref[...] = jnp.zeros(l_ref.shape, _F32)
    acc_ref[...] = jnp.zeros(acc_ref.shape, _F32)

    w_all = wi_ref[...]
    idx_scale = (IDX_HEADS ** -0.5) * (IDX_DIM ** -0.5)

    def score_body(c, carry):
        ks = pl.multiple_of(c * KEY_CHUNK, KEY_CHUNK)
        kic = ki_ref[pl.ds(ks, KEY_CHUNK), :]
        sc = jnp.zeros((Q_BLOCK, KEY_CHUNK), _F32)
        for hh in range(IDX_HEADS):
            d = lax.dot_general(qim_ref[hh], kic, _NT, preferred_element_type=_F32)
            sc = sc + w_all[:, IDX_DIM + hh:IDX_DIM + hh + 1] * jnp.maximum(d, 0.0)
        sc = sc * idx_scale
        sc = jnp.where(sc == 0.0, 0.0, sc)
        key = jnp.where(ks + lane_chunk <= t_idx, _sortable(sc), _KEY_NEG_INF)
        for j in range(BLOCKS_PER_CHUNK):
            keys_ref[c * BLOCKS_PER_CHUNK + j] = key[:, j * Q_BLOCK:(j + 1) * Q_BLOCK]
        return carry

    lax.fori_loop(0, n_chunks, score_body, 0)

    def count(pred):
        def body(c, acc):
            for j in range(BLOCKS_PER_CHUNK):
                blk = c * BLOCKS_PER_CHUNK + j
                acc = acc + jnp.where(pred(keys_ref[blk], blk * Q_BLOCK + lane_blk), 1.0, 0.0)
            return acc

        acc = lax.fori_loop(0, n_chunks, body, jnp.zeros((Q_BLOCK, Q_BLOCK), _F32))
        return jnp.sum(acc, axis=1, keepdims=True)

    def value_bit(i, tb):
        cand = tb | lax.shift_left(jnp.int32(1), 31 - i)
        cand_signed = cand ^ jnp.int32(_INT_MIN)
        cnt = count(lambda key, sidx: key >= cand_signed)
        return jnp.where(cnt >= n_sel, cand, tb)

    tau = lax.fori_loop(0, 32, value_bit, jnp.zeros((Q_BLOCK, 1), jnp.int32)) ^ jnp.int32(_INT_MIN)
    need = n_sel - count(lambda key, sidx: key > tau)

    def index_bit(i, x):
        cand = x | lax.shift_left(jnp.int32(1), idx_bits - 1 - i)
        cnt = count(lambda key, sidx: jnp.logical_and(key == tau, sidx < cand))
        return jnp.where(cnt < need, cand, x)

    last_tie = lax.fori_loop(0, idx_bits, index_bit, jnp.zeros((Q_BLOCK, 1), jnp.int32))

    def attend(key, k_tile, v_tile, s0, lane, bias, enable=None):
        sidx = s0 + lane
        sel = jnp.logical_or(key > tau, jnp.logical_and(key == tau, sidx <= last_tie))
        sel = jnp.logical_and(sel, key > _KEY_NEG_INF)
        if enable is not None:
            sel = jnp.logical_and(sel, jnp.zeros_like(sidx) + enable > 0)
        for g in range(ATT_KV_HEADS):
            s = lax.dot_general(qs_ref[g], k_tile[:, g * ATT_HEAD_DIM:(g + 1) * ATT_HEAD_DIM], _NT,
                                preferred_element_type=_F32) * (ATT_HEAD_DIM ** -0.5)
            ps, alphas = [], []
            for h in range(GQA_GROUP):
                rows = slice(h * Q_BLOCK, (h + 1) * Q_BLOCK)
                sh = jnp.where(sel, s[rows] + bias(g * GQA_GROUP + h), MASKED_LOGIT)
                m_old = m_ref[g, rows]
                m_new = jnp.maximum(m_old, jnp.max(sh, axis=1, keepdims=True))
                alpha = jnp.exp(m_old - m_new)
                p = jnp.exp(sh - m_new)
                l_ref[g, rows] = alpha * l_ref[g, rows] + jnp.sum(p, axis=1, keepdims=True)
                m_ref[g, rows] = m_new
                ps.append(p.astype(_BF16))
                alphas.append(alpha)
            pv = jnp.dot(jnp.concatenate(ps, axis=0), v_tile[:, g * ATT_HEAD_DIM:(g + 1) * ATT_HEAD_DIM],
                         preferred_element_type=_F32)
            acc_ref[g] = acc_ref[g] * jnp.concatenate(alphas, axis=0) + pv

    def far_bias(hh):
        return bfar_ref[hh]

    n_far = jnp.maximum(qb - 1, 0) // BLOCKS_PER_CHUNK

    def far_body(c, carry):
        ks = pl.multiple_of(c * KEY_CHUNK, KEY_CHUNK)
        key = jnp.concatenate([keys_ref[c * BLOCKS_PER_CHUNK + j] for j in range(BLOCKS_PER_CHUNK)], axis=1)
        attend(key, k_ref[pl.ds(ks, KEY_CHUNK), :], v_ref[pl.ds(ks, KEY_CHUNK), :], ks, lane_chunk, far_bias)
        return carry

    lax.fori_loop(0, n_far, far_body, 0)

    def block_step(j, bias, enable=None):
        ks = pl.multiple_of(j * Q_BLOCK, Q_BLOCK)
        attend(keys_ref[j], k_ref[pl.ds(ks, Q_BLOCK), :], v_ref[pl.ds(ks, Q_BLOCK), :], ks, lane_blk, bias, enable)

    def tail_body(j, carry):
        block_step(j, far_bias)
        return carry

    lax.fori_loop(n_far * BLOCKS_PER_CHUNK, qb - 1, tail_body, 0)
    block_step(jnp.maximum(qb - 1, 0), lambda hh: bprev_ref[hh], enable=jnp.minimum(qb, 1))
    block_step(qb, lambda hh: bdiag_ref[hh])

    for hh in range(ATT_HEADS):
        g, h = divmod(hh, GQA_GROUP)
        rows = slice(h * Q_BLOCK, (h + 1) * Q_BLOCK)
        o_ref[:, hh * ATT_HEAD_DIM:(hh + 1) * ATT_HEAD_DIM] = (acc_ref[g, rows] / l_ref[g, rows]).astype(o_ref.dtype)


def dsa_attention(zb, zf, rel_bias, B, S):
    nq = S // Q_BLOCK
    n_sel = min(TOPK_MAX, S // 4)
    tl = jnp.arange(Q_BLOCK, dtype=jnp.int32)[:, None]
    sr = jnp.arange(Q_BLOCK, dtype=jnp.int32)[None, :]
    bprev = rel_bias[rel_bucket(tl + Q_BLOCK - sr)].transpose(2, 0, 1).astype(_F32)
    bdiag = rel_bias[rel_bucket(tl - sr)].transpose(2, 0, 1).astype(_F32)
    bfar = rel_bias[rel_bucket(jnp.int32(REL_MAX_DIST + 1))].astype(_F32)
    kernel = functools.partial(_dsa_kernel, n_sel=float(n_sel), idx_bits=int(S - 1).bit_length())
    c_k = ATT_WIDTH // KV_WIDTH
    c_qi = (ATT_WIDTH + 2 * KV_WIDTH) // (IDX_HEADS * IDX_DIM)
    c_ki = (ATT_WIDTH + 2 * KV_WIDTH + IDX_HEADS * IDX_DIM) // LANES
    return pl.pallas_call(
        kernel,
        grid=(B, nq),
        in_specs=[pl.BlockSpec((Q_BLOCK, ATT_WIDTH), lambda b, i: (b * nq + i, 0)),
                  pl.BlockSpec((Q_BLOCK, IDX_HEADS * IDX_DIM), lambda b, i: (b * nq + i, c_qi)),
                  pl.BlockSpec((Q_BLOCK, LANES), lambda b, i: (b * nq + i, ZF_WIDX // LANES)),
                  pl.BlockSpec((S, KV_WIDTH), lambda b, i: (b, c_k)),
                  pl.BlockSpec((S, KV_WIDTH), lambda b, i: (b, c_k + 1)),
                  pl.BlockSpec((S, LANES), lambda b, i: (b, c_ki)),
                  pl.BlockSpec((ATT_HEADS, Q_BLOCK, Q_BLOCK), lambda b, i: (0, 0, 0)),
                  pl.BlockSpec((ATT_HEADS, Q_BLOCK, Q_BLOCK), lambda b, i: (0, 0, 0)),
                  pl.BlockSpec(memory_space=pltpu.SMEM)],
        out_specs=pl.BlockSpec((Q_BLOCK, ATT_WIDTH), lambda b, i: (b * nq + i, 0)),
        out_shape=jax.ShapeDtypeStruct((B * S, ATT_WIDTH), _BF16),
        scratch_shapes=[pltpu.VMEM((nq, Q_BLOCK, Q_BLOCK), jnp.int32),
                        pltpu.VMEM((ATT_KV_HEADS, GQA_GROUP * Q_BLOCK, ATT_HEAD_DIM), _BF16),
                        pltpu.VMEM((IDX_HEADS, Q_BLOCK, LANES), _BF16),
                        pltpu.VMEM((ATT_KV_HEADS, GQA_GROUP * Q_BLOCK, 1), _F32),
                        pltpu.VMEM((ATT_KV_HEADS, GQA_GROUP * Q_BLOCK, 1), _F32),
                        pltpu.VMEM((ATT_KV_HEADS, GQA_GROUP * Q_BLOCK, ATT_HEAD_DIM), _F32)],
        compiler_params=_params("parallel", "arbitrary"),
        name="dsa_attention",
    )(zb, zb, zf, zb, zb, zb, bprev, bdiag, bfar)


def _split_dot(x, w):
    hi = x.astype(_BF16)
    lo = (x - hi.astype(_F32)).astype(_BF16)
    return jnp.dot(hi, w, preferred_element_type=_F32) + jnp.dot(lo, w, preferred_element_type=_F32)


def _split_dot_left(w, x):
    hi = x.astype(_BF16)
    lo = (x - hi.astype(_F32)).astype(_BF16)
    return jnp.dot(w, hi, preferred_element_type=_F32) + jnp.dot(w, lo, preferred_element_type=_F32)


def _rwkv_kernel(slab_ref, mu_ref, vec_ref, wup_ref, aup_ref, gup_ref, o_ref, xs_ref, state_ref):
    C = RWKV_CHUNK
    W = RWKV_WIDTH
    c = pl.program_id(1)

    @pl.when(c == 0)
    def _():
        xs_ref[0:8, :] = jnp.zeros((8, ZF_SLAB_PAD), _F32)
        state_ref[...] = jnp.zeros(state_ref.shape, _F32)

    slab = slab_ref[...]
    xs_ref[8:8 + C, :] = slab
    shifted = xs_ref[7:7 + C, :]
    xs = slab + (shifted - slab) * mu_ref[...]
    xs_ref[0:8, :] = slab[C - 8:C, :]

    r = xs[:, 0:W]
    k = xs[:, W:2 * W]
    v = xs[:, 2 * W:3 * W]
    lora = xs[:, 3 * W:3 * W + LANES]
    xg = xs[:, 3 * W + LANES:3 * W + 3 * LANES]
    w0, a0, k_k, k_a, r_k, gn_w, gn_b = (vec_ref[i:i + 1, :] for i in range(7))
    d = w0 + jnp.dot(jnp.tanh(lora).astype(_BF16), wup_ref[...], preferred_element_type=_F32)
    logw = -math.exp(-0.5) * _sigmoid(d)
    a = _sigmoid(a0 + jnp.dot(lora.astype(_BF16), aup_ref[...], preferred_element_type=_F32))
    g = jnp.dot(_sigmoid(xg).astype(_BF16), gup_ref[...], preferred_element_type=_F32)
    kk_raw = k * k_k
    kt = k * (1.0 + (a - 1.0) * k_a)
    bonus_raw = r * kt * r_k

    ti = lax.broadcasted_iota(jnp.int32, (C, C), 0)
    tj = lax.broadcasted_iota(jnp.int32, (C, C), 1)
    cum = _split_dot_left(jnp.where(tj <= ti, 1.0, 0.0).astype(_BF16), logw)
    cum_end = cum[C - 1:C, :]
    g_incl = jnp.exp(cum)
    g_excl = jnp.exp(cum - logw)
    g_inv = jnp.exp(-cum)
    g_end = jnp.exp(cum_end - cum)
    g_total = jnp.exp(cum_end)

    lo = lax.broadcasted_iota(jnp.int32, (C, LANES), 1) < RWKV_HEAD
    ri = lax.broadcasted_iota(jnp.int32, (LANES, LANES), 0)
    ci = lax.broadcasted_iota(jnp.int32, (LANES, LANES), 1)
    same_head = (ri // RWKV_HEAD) == (ci // RWKV_HEAD)
    ones_bd = jnp.where(same_head, 1.0, 0.0).astype(_BF16)
    eye = ri == ci
    same_blk = (ri // C) == (ci // C)
    strict = jnp.logical_and(same_blk, ci < ri)
    incl = jnp.logical_and(same_blk, ci <= ri)

    def stack(x):
        return jnp.concatenate([jnp.where(lo, x, 0.0), jnp.where(lo, 0.0, x)], axis=0)

    for p in range(HEAD_PAIRS):
        sl = slice(p * LANES, (p + 1) * LANES)
        r_p, v_p, kt_p = r[:, sl], v[:, sl], kt[:, sl]
        kkr = kk_raw[:, sl]
        kk = kkr * lax.rsqrt(jnp.maximum(_split_dot(kkr * kkr, ones_bd), 1e-24))
        b_p = kk * a[:, sl]
        rg = r_p * g_incl[:, sl]
        kq = kk * g_excl[:, sl]
        bi = b_p * g_inv[:, sl]
        ki = kt_p * g_inv[:, sl]
        kdec = kt_p * g_end[:, sl]
        bdec = b_p * g_end[:, sl]

        lhs4 = jnp.concatenate([stack(kq), stack(rg)], axis=0).astype(_BF16)
        rhs4 = jnp.concatenate([bi, bi, ki, ki], axis=0).astype(_BF16)
        gm = lax.dot_general(lhs4, rhs4, _NT, preferred_element_type=_F32)
        a_bd = jnp.where(strict, gm[:2 * C, :2 * C], 0.0)
        bm_bd = jnp.where(strict, gm[:2 * C, 2 * C:], 0.0)
        pb_bd = jnp.where(incl, gm[2 * C:, :2 * C], 0.0)
        pk_bd = jnp.where(incl, gm[2 * C:, 2 * C:], 0.0)

        inv = jnp.where(eye, 1.0, 0.0) - jnp.where((ri // 2) == (ci // 2), a_bd, 0.0)
        s = 2
        while s < C:
            off = jnp.logical_and((ri // (2 * s)) == (ci // (2 * s)), (ri // s) != (ci // s))
            a_off = jnp.where(off, a_bd, 0.0).astype(_BF16)
            tm = jnp.dot(inv.astype(_BF16), a_off, preferred_element_type=_F32)
            inv = inv - jnp.dot(tm.astype(_BF16), inv.astype(_BF16), preferred_element_type=_F32)
            s *= 2

        h_bd = state_ref[p]
        v_st = stack(v_p).astype(_BF16)
        hw = jnp.dot(lhs4, h_bd.astype(_BF16), preferred_element_type=_F32)
        bv = jnp.dot(jnp.concatenate([bm_bd, pk_bd], axis=0).astype(_BF16), v_st, preferred_element_type=_F32)
        w_st = hw[:2 * C] + bv[:2 * C]
        u_st = jnp.dot(inv.astype(_BF16), w_st.astype(_BF16), preferred_element_type=_F32)
        y_st = hw[2 * C:] + bv[2 * C:] - jnp.dot(pb_bd.astype(_BF16), u_st.astype(_BF16),
                                                preferred_element_type=_F32)
        y = y_st[:C] + y_st[C:]
        u = u_st[:C] + u_st[C:]

        m_t = jnp.concatenate([kdec, -bdec], axis=0).astype(_BF16)
        n_t = jnp.concatenate([v_p, u], axis=0).astype(_BF16)
        dh = lax.dot_general(m_t, n_t, _TN, preferred_element_type=_F32)
        gt_col = jnp.sum(jnp.where(eye, jnp.broadcast_to(g_total[:, sl], (LANES, LANES)), 0.0),
                         axis=1, keepdims=True)
        state_ref[p] = jnp.where(same_head, gt_col * h_bd + dh, 0.0)

        mean = _split_dot(y, ones_bd) * (1.0 / RWKV_HEAD)
        yc = y - mean
        var = _split_dot(yc * yc, ones_bd) * (1.0 / RWKV_HEAD)
        yn = yc * lax.rsqrt(var + GN_EPS) * gn_w[:, sl] + gn_b[:, sl]
        bonus = _split_dot(bonus_raw[:, sl], ones_bd) * v_p
        o_ref[:, sl] = ((yn + bonus) * g[:, sl]).astype(o_ref.dtype)


def rwkv7_mix(zf, mu, w0, w_up, a0, a_up, g_up, k_k, k_a, r_k, gn_w, gn_b, B, S):
    C = RWKV_CHUNK
    nc = S // C
    W = RWKV_WIDTH
    mu_p = jnp.pad(mu, (0, ZF_SLAB_PAD - RWKV_SLAB)).reshape(1, ZF_SLAB_PAD)
    vecs = jnp.stack([w0, a0, k_k, k_a, r_k, gn_w, gn_b, jnp.zeros_like(w0)]).astype(_F32)
    wup = jnp.concatenate([w_up, jnp.zeros((LANES - DECAY_RANK, W), w_up.dtype)], axis=0).astype(_BF16)
    aup = jnp.concatenate([jnp.zeros((DECAY_RANK, W), a_up.dtype), a_up], axis=0).astype(_BF16)
    gup = jnp.concatenate([g_up, jnp.zeros((2 * LANES - GATE_RANK, W), g_up.dtype)], axis=0).astype(_BF16)
    return pl.pallas_call(
        _rwkv_kernel,
        grid=(B, nc),
        in_specs=[pl.BlockSpec((C, ZF_SLAB_PAD), lambda b, c: (b * nc + c, 0)),
                  pl.BlockSpec((1, ZF_SLAB_PAD), lambda b, c: (0, 0)),
                  pl.BlockSpec((8, W), lambda b, c: (0, 0)),
                  pl.BlockSpec((LANES, W), lambda b, c: (0, 0)),
                  pl.BlockSpec((LANES, W), lambda b, c: (0, 0)),
                  pl.BlockSpec((2 * LANES, W), lambda b, c: (0, 0))],
        out_specs=pl.BlockSpec((C, W), lambda b, c: (b * nc + c, 0)),
        out_shape=jax.ShapeDtypeStruct((B * S, W), _BF16),
        scratch_shapes=[pltpu.VMEM((C + 8, ZF_SLAB_PAD), _F32),
                        pltpu.VMEM((HEAD_PAIRS, LANES, LANES), _F32)],
        compiler_params=_params("parallel", "arbitrary"),
        name="rwkv7_mix",
    )(zf, mu_p, vecs, wup, aup, gup)


def _router_kernel(h_ref, g_ref, wr_ref, xn_ref, route_ref):
    xn = _rms_to_bf16(h_ref[...], g_ref[...])
    xn_ref[...] = xn
    logits = jnp.dot(xn, wr_ref[...], preferred_element_type=_F32)
    lane = lax.broadcasted_iota(jnp.int32, logits.shape, 1).astype(_F32)
    l1 = jnp.where(lane < N_EXPERTS, logits, -jnp.inf)
    m1 = jnp.max(l1, axis=1, keepdims=True)
    i1 = jnp.min(jnp.where(l1 == m1, lane, float(LANES)), axis=1, keepdims=True)
    l2 = jnp.where(lane == i1, -jnp.inf, l1)
    m2 = jnp.max(l2, axis=1, keepdims=True)
    i2 = jnp.min(jnp.where(l2 == m2, lane, float(LANES)), axis=1, keepdims=True)
    e2 = jnp.exp(m2 - m1)
    g1 = 1.0 / (1.0 + e2)
    g2 = e2 / (1.0 + e2)
    route_ref[...] = jnp.where(lane == 0.0, i1, jnp.where(lane == 1.0, i2,
                               jnp.where(lane == 2.0, g1, jnp.where(lane == 3.0, g2, 0.0))))


def moe_route(h, g, router, tm):
    T, D = h.shape
    wr = jnp.pad(router, ((0, 0), (0, LANES - N_EXPERTS))).astype(_BF16)
    return pl.pallas_call(
        _router_kernel,
        grid=(T // tm,),
        in_specs=[pl.BlockSpec((tm, D), lambda i: (i, 0)),
                  pl.BlockSpec((1, D), lambda i: (0, 0)),
                  pl.BlockSpec((D, LANES), lambda i: (0, 0))],
        out_specs=[pl.BlockSpec((tm, D), lambda i: (i, 0)),
                   pl.BlockSpec((tm, LANES), lambda i: (i, 0))],
        out_shape=[jax.ShapeDtypeStruct((T, D), _BF16), jax.ShapeDtypeStruct((T, LANES), _F32)],
        compiler_params=_params("parallel"),
        name="moe_route",
    )(h, g.reshape(1, D), wr)


def _moe_ffn_kernel(te_ref, nv_ref, x_ref, w1_ref, w3_ref, w2_ref, o_ref):
    n = pl.program_id(0)
    f = pl.program_id(1)

    @pl.when(jnp.logical_and(n >= nv_ref[0], f == 0))
    def _():
        o_ref[...] = jnp.zeros(o_ref.shape, o_ref.dtype)

    @pl.when(n < nv_ref[0])
    def _():
        x = x_ref[...]
        a = jnp.dot(x, w1_ref[...], preferred_element_type=_F32)
        b = jnp.dot(x, w3_ref[...], preferred_element_type=_F32)
        mid = (a * _sigmoid(a) * b).astype(_BF16)
        contrib = jnp.dot(mid, w2_ref[...], preferred_element_type=_F32)

        @pl.when(f == 0)
        def _():
            o_ref[...] = contrib

        @pl.when(f > 0)
        def _():
            o_ref[...] += contrib


def moe_grouped_ffn(xs, tile_expert, n_valid, w1, w3, w2):
    R, D = xs.shape
    F = w1.shape[2]
    nf = F // MOE_TF

    def f_idx(n, f, nv):
        return jnp.where(n < nv[0], f, nf - 1)

    return pl.pallas_call(
        _moe_ffn_kernel,
        grid_spec=pltpu.PrefetchScalarGridSpec(
            num_scalar_prefetch=2,
            grid=(R // MOE_TM, nf),
            in_specs=[pl.BlockSpec((MOE_TM, D), lambda n, f, te, nv: (n, 0)),
                      pl.BlockSpec((None, D, MOE_TF), lambda n, f, te, nv: (te[n], 0, f_idx(n, f, nv))),
                      pl.BlockSpec((None, D, MOE_TF), lambda n, f, te, nv: (te[n], 0, f_idx(n, f, nv))),
                      pl.BlockSpec((None, MOE_TF, D), lambda n, f, te, nv: (te[n], f_idx(n, f, nv), 0))],
            out_specs=pl.BlockSpec((MOE_TM, D), lambda n, f, te, nv: (n, 0))),
        out_shape=jax.ShapeDtypeStruct((R, D), _F32),
        compiler_params=_params("parallel", "arbitrary"),
        name="moe_grouped_ffn",
    )(tile_expert, n_valid, xs, w1, w3, w2)


def moe_residual(h, g, router, w1, w3, w2):
    T, D = h.shape
    E = N_EXPERTS
    xn, route = moe_route(h, g, router, tm=512)
    top = route[:, 0:TOP_K_EXPERTS].astype(jnp.int32)
    gates = route[:, TOP_K_EXPERTS:2 * TOP_K_EXPERTS]
    ef = top.T.reshape(TOP_K_EXPERTS * T)
    onehot = (ef[:, None] == jnp.arange(E, dtype=jnp.int32)[None, :]).astype(jnp.int32)
    csum = jnp.cumsum(onehot, axis=0)
    rank = jnp.take_along_axis(csum, ef[:, None], axis=1)[:, 0] - 1
    counts = csum[-1]
    padded = ((counts + MOE_TM - 1) // MOE_TM) * MOE_TM
    pends = jnp.cumsum(padded)
    pstarts = pends - padded
    starts = jnp.cumsum(counts) - counts
    pos = pstarts[ef] + rank
    R = TOP_K_EXPERTS * T + E * MOE_TM
    n_tiles = R // MOE_TM
    tile_expert = jnp.searchsorted(pends, jnp.arange(n_tiles, dtype=jnp.int32) * MOE_TM, side='right')
    tile_expert = jnp.minimum(tile_expert, E - 1).astype(jnp.int32)
    n_valid = (pends[-1:] // MOE_TM).astype(jnp.int32)
    order = jnp.argsort(ef, stable=True).astype(jnp.int32)
    rho = jnp.arange(R, dtype=jnp.int32)
    e_row = tile_expert[rho // MOE_TM]
    off = rho - pstarts[e_row]
    src = order[jnp.clip(starts[e_row] + off, 0, TOP_K_EXPERTS * T - 1)] % T
    xs = jnp.take(xn, src, axis=0)
    ys = moe_grouped_ffn(xs, tile_expert, n_valid, w1, w3, w2)
    mix = sum(gates[:, slot:slot + 1] * jnp.take(ys, pos[slot * T:(slot + 1) * T], axis=0)
              for slot in range(TOP_K_EXPERTS))
    return h + mix


def _pack_w_in(w):
    wq, wk, wv, wqi, wki, wwi, wslab, wga, wgb = _split(w, IN_SIZES)
    D = w.shape[0]
    wb = jnp.concatenate([wq, wk, wv, wqi, wki, wki], axis=1)
    wf = jnp.concatenate([wslab, jnp.zeros((D, ZF_SLAB_PAD - RWKV_SLAB), w.dtype),
                          wki, wwi, jnp.zeros((D, LANES - IDX_DIM - IDX_HEADS), w.dtype), wga, wgb], axis=1)
    return wb.astype(_BF16), wf.astype(_BF16)


def kernel(x, p, w_in, att_up, rwkv_up, w_out, rel_bias, rwkv_mu, rwkv_w0, rwkv_w_up,
           rwkv_a0, rwkv_a_up, rwkv_g_up, rwkv_k_k, rwkv_k_a, rwkv_r_k, rwkv_gn_w, rwkv_gn_b,
           norm_mix, norm_ffn, norm_ple, ple_proj, ple_gate, ffn_w1, ffn_w3, ffn_w2,
           moe_router, moe_w1, moe_w3, moe_w2, final_norm):
    B, S, D = x.shape
    T = B * S
    depth = w_in.shape[0]
    h = x.reshape(T, D)
    for i in range(depth):
        wb, wf = _pack_w_in(w_in[i])
        zb = norm_matmul(h, norm_mix[i], wb, _BF16, tm=512, tn=ZB_WIDTH)
        zf = norm_matmul(h, norm_mix[i], wf, _F32, tm=512, tn=1536)
        att = dsa_attention(zb, zf, rel_bias, B, S)
        rw = rwkv7_mix(zf, rwkv_mu[i], rwkv_w0[i], rwkv_w_up[i], rwkv_a0[i], rwkv_a_up[i], rwkv_g_up[i],
                       rwkv_k_k[i], rwkv_k_a[i], rwkv_r_k[i], rwkv_gn_w[i], rwkv_gn_b[i], B, S)
        merged = gated_merge(att, rw, zf, att_up[i].astype(_BF16), rwkv_up[i].astype(_BF16), tm=512, tn=512)
        h = matmul_residual(merged, w_out[i].astype(_BF16), h, tm=512, tn=1024)
        if i % 2 == 0:
            j = i // 2
            h = ffn_residual(h, norm_ffn[i], ffn_w1[j].astype(_BF16), ffn_w3[j].astype(_BF16),
                             ffn_w2[j].astype(_BF16), tm=512, tf=512)
        else:
            j = i // 2
            h = moe_residual(h, norm_ffn[i], moe_router[j], moe_w1[j].astype(_BF16), moe_w3[j].astype(_BF16),
                             moe_w2[j].astype(_BF16))
        h = ple_residual(h, norm_ple[i], p[i].reshape(T, PLE_DIM), ple_gate[i].astype(_BF16),
                         ple_proj[i].astype(_BF16), final_norm, final=(i == depth - 1), tm=256)
    return h.reshape(B, S, D)
```

```python
import functools
import math

import jax
import jax.numpy as jnp
import numpy as np
from jax import lax
from jax.experimental import pallas as pl
from jax.experimental.pallas import tpu as pltpu

D_MODEL = 2048
DEPTH = 2
ATT_HEADS = 8
ATT_KV_HEADS = 2
ATT_HEAD_DIM = 128
IDX_HEADS = 8
IDX_DIM = 64
TOPK_MAX = 256
REL_BUCKETS = 32
REL_MAX_DIST = 128
RWKV_HEAD = 64
RWKV_HEADS = 16
DECAY_RANK = 64
ICLR_RANK = 64
GATE_RANK = 160
D_FF = 5632
N_EXPERTS = 8
TOP_K_EXPERTS = 2
D_FF_EXPERT = 7168
PLE_DIM = 256
RMS_EPS = 1e-6
GN_EPS = 64e-5

ATT_WIDTH = ATT_HEADS * ATT_HEAD_DIM
KV_WIDTH = ATT_KV_HEADS * ATT_HEAD_DIM
RWKV_WIDTH = RWKV_HEADS * RWKV_HEAD
RWKV_SIZES = (RWKV_WIDTH, RWKV_WIDTH, RWKV_WIDTH, DECAY_RANK, ICLR_RANK, GATE_RANK)
RWKV_SLAB = 3 * RWKV_WIDTH + DECAY_RANK + ICLR_RANK + GATE_RANK
IN_SIZES = (ATT_WIDTH, KV_WIDTH, KV_WIDTH, IDX_HEADS * IDX_DIM, IDX_DIM, IDX_HEADS, RWKV_SLAB, D_MODEL, D_MODEL)

LANES = 128
VMEM_LIMIT = 56 * 1024 * 1024

Q_BLOCK = 128
KEY_CHUNK = 512
BLOCKS_PER_CHUNK = KEY_CHUNK // Q_BLOCK
MASKED_LOGIT = -1e30
GQA_GROUP = ATT_HEADS // ATT_KV_HEADS

ZB_WIDTH = ATT_WIDTH + 2 * KV_WIDTH + IDX_HEADS * IDX_DIM + 2 * IDX_DIM
ZF_SLAB_PAD = 3456
ZF_WIDX = ZF_SLAB_PAD
ZF_GATE_A = ZF_WIDX + LANES
ZF_GATE_B = ZF_GATE_A + D_MODEL
ZF_WIDTH = ZF_GATE_B + D_MODEL

RWKV_CHUNK = 64
HEAD_PAIRS = RWKV_HEADS // 2
MOE_TM = 512
MOE_TF = 1024

_F32 = jnp.float32
_BF16 = jnp.bfloat16
_NT = (((1,), (1,)), ((), ()))
_TN = (((0,), (0,)), ((), ()))


def _split(z, sizes):
    return jnp.split(z, np.cumsum(sizes)[:-1].tolist(), axis=-1)


def _sigmoid(x):
    return 1.0 / (1.0 + jnp.exp(-x))


def _rms_to_bf16(x, g):
    y = x * lax.rsqrt(jnp.mean(x * x, axis=-1, keepdims=True) + RMS_EPS)
    return (y * g).astype(_BF16)


def _params(*sem):
    return pltpu.CompilerParams(dimension_semantics=sem, vmem_limit_bytes=VMEM_LIMIT)


def _norm_mm_kernel(x_ref, g_ref, w_ref, o_ref, xn_ref):
    @pl.when(pl.program_id(1) == 0)
    def _():
        xn_ref[...] = _rms_to_bf16(x_ref[...], g_ref[...])

    o_ref[...] = jnp.dot(xn_ref[...], w_ref[...], preferred_element_type=_F32).astype(o_ref.dtype)


def norm_matmul(x, g, w, out_dtype, tm, tn):
    T, D = x.shape
    N = w.shape[1]
    return pl.pallas_call(
        _norm_mm_kernel,
        grid=(T // tm, N // tn),
        in_specs=[pl.BlockSpec((tm, D), lambda i, j: (i, 0)),
                  pl.BlockSpec((1, D), lambda i, j: (0, 0)),
                  pl.BlockSpec((D, tn), lambda i, j: (0, j))],
        out_specs=pl.BlockSpec((tm, tn), lambda i, j: (i, j)),
        out_shape=jax.ShapeDtypeStruct((T, N), out_dtype),
        scratch_shapes=[pltpu.VMEM((tm, D), _BF16)],
        compiler_params=_params("parallel", "arbitrary"),
        name="norm_matmul",
    )(x, g.reshape(1, D), w)


def _merge_kernel(att_ref, rw_ref, ga_ref, gb_ref, wa_ref, wr_ref, o_ref):
    a = jnp.dot(att_ref[...], wa_ref[...], preferred_element_type=_F32)
    r = jnp.dot(rw_ref[...], wr_ref[...], preferred_element_type=_F32)
    o_ref[...] = (_sigmoid(ga_ref[...]) * a + _sigmoid(gb_ref[...]) * r).astype(o_ref.dtype)


def gated_merge(att, rw, zf, wa, wr, tm, tn):
    T, K = att.shape
    N = wa.shape[1]
    ja, jb = ZF_GATE_A // tn, ZF_GATE_B // tn
    return pl.pallas_call(
        _merge_kernel,
        grid=(T // tm, N // tn),
        in_specs=[pl.BlockSpec((tm, K), lambda i, j: (i, 0)),
                  pl.BlockSpec((tm, K), lambda i, j: (i, 0)),
                  pl.BlockSpec((tm, tn), lambda i, j: (i, ja + j)),
                  pl.BlockSpec((tm, tn), lambda i, j: (i, jb + j)),
                  pl.BlockSpec((K, tn), lambda i, j: (0, j)),
                  pl.BlockSpec((K, tn), lambda i, j: (0, j))],
        out_specs=pl.BlockSpec((tm, tn), lambda i, j: (i, j)),
        out_shape=jax.ShapeDtypeStruct((T, N), _BF16),
        compiler_params=_params("parallel", "arbitrary"),
        name="gated_merge",
    )(att, rw, zf, zf, wa, wr)


def _mm_res_kernel(x_ref, w_ref, h_ref, o_ref):
    o_ref[...] = h_ref[...] + jnp.dot(x_ref[...], w_ref[...], preferred_element_type=_F32)


def matmul_residual(x, w, h, tm, tn):
    T, K = x.shape
    N = w.shape[1]
    return pl.pallas_call(
        _mm_res_kernel,
        grid=(T // tm, N // tn),
        in_specs=[pl.BlockSpec((tm, K), lambda i, j: (i, 0)),
                  pl.BlockSpec((K, tn), lambda i, j: (0, j)),
                  pl.BlockSpec((tm, tn), lambda i, j: (i, j))],
        out_specs=pl.BlockSpec((tm, tn), lambda i, j: (i, j)),
        out_shape=jax.ShapeDtypeStruct((T, N), _F32),
        compiler_params=_params("parallel", "arbitrary"),
        name="matmul_residual",
    )(x, w, h)


def _ffn_kernel(h_ref, g_ref, w1_ref, w3_ref, w2_ref, o_ref, xn_ref):
    f = pl.program_id(1)

    @pl.when(f == 0)
    def _():
        xn_ref[...] = _rms_to_bf16(h_ref[...], g_ref[...])
        o_ref[...] = h_ref[...]

    xn = xn_ref[...]
    a = jnp.dot(xn, w1_ref[...], preferred_element_type=_F32)
    b = jnp.dot(xn, w3_ref[...], preferred_element_type=_F32)
    mid = (a * _sigmoid(a) * b).astype(_BF16)
    o_ref[...] += jnp.dot(mid, w2_ref[...], preferred_element_type=_F32)


def ffn_residual(h, g, w1, w3, w2, tm, tf):
    T, D = h.shape
    F = w1.shape[1]
    return pl.pallas_call(
        _ffn_kernel,
        grid=(T // tm, F // tf),
        in_specs=[pl.BlockSpec((tm, D), lambda i, f: (i, 0)),
                  pl.BlockSpec((1, D), lambda i, f: (0, 0)),
                  pl.BlockSpec((D, tf), lambda i, f: (0, f)),
                  pl.BlockSpec((D, tf), lambda i, f: (0, f)),
                  pl.BlockSpec((tf, D), lambda i, f: (f, 0))],
        out_specs=pl.BlockSpec((tm, D), lambda i, f: (i, 0)),
        out_shape=jax.ShapeDtypeStruct((T, D), _F32),
        scratch_shapes=[pltpu.VMEM((tm, D), _BF16)],
        compiler_params=_params("parallel", "arbitrary"),
        name="ffn_residual",
    )(h, g.reshape(1, D), w1, w3, w2)


def _ple_kernel(h_ref, g_ref, p_ref, wg_ref, wp_ref, fg_ref, o_ref, *, final):
    h = h_ref[...]
    xn = _rms_to_bf16(h, g_ref[...])
    gate = _sigmoid(jnp.dot(xn, wg_ref[...], preferred_element_type=_F32))
    pp = jnp.dot(p_ref[...].astype(_BF16), wp_ref[...], preferred_element_type=_F32)
    out = h + gate * pp
    if final:
        out = out * lax.rsqrt(jnp.mean(out * out, axis=-1, keepdims=True) + RMS_EPS) * fg_ref[...]
    o_ref[...] = out


def ple_residual(h, g, p, wg, wp, final_g, final, tm):
    T, D = h.shape
    P = p.shape[1]
    return pl.pallas_call(
        functools.partial(_ple_kernel, final=final),
        grid=(T // tm,),
        in_specs=[pl.BlockSpec((tm, D), lambda i: (i, 0)),
                  pl.BlockSpec((1, D), lambda i: (0, 0)),
                  pl.BlockSpec((tm, P), lambda i: (i, 0)),
                  pl.BlockSpec((D, D), lambda i: (0, 0)),
                  pl.BlockSpec((P, D), lambda i: (0, 0)),
                  pl.BlockSpec((1, D), lambda i: (0, 0))],
        out_specs=pl.BlockSpec((tm, D), lambda i: (i, 0)),
        out_shape=jax.ShapeDtypeStruct((T, D), _F32),
        compiler_params=_params("parallel"),
        name="ple_residual",
    )(h, g.reshape(1, D), p, wg, wp, final_g.reshape(1, D))


def rel_bucket(dist):
    max_exact = REL_BUCKETS // 2
    dist = jnp.maximum(dist, 0)
    d_f = jnp.maximum(dist, 1).astype(_F32)
    large = max_exact + (jnp.log(d_f / max_exact) / math.log(REL_MAX_DIST / max_exact)
                         * (REL_BUCKETS - max_exact)).astype(jnp.int32)
    large = jnp.minimum(large, REL_BUCKETS - 1)
    return jnp.where(dist < max_exact, dist, large)


def _sortable(x):
    b = lax.bitcast_convert_type(x, jnp.int32)
    return b ^ ((b >> 31) & jnp.int32(0x7FFFFFFF))


_INT_MIN = -2 ** 31
_KEY_NEG_INF = 0x807FFFFF - 2 ** 32


def _dsa_kernel(q_ref, qi_ref, wi_ref, k_ref, v_ref, ki_ref, bprev_ref, bdiag_ref, bfar_ref, o_ref,
                keys_ref, qs_ref, qim_ref, m_ref, acc_ref, *, n_sel, idx_bits):
    qb = pl.program_id(1)
    t0 = qb * Q_BLOCK
    n_chunks = qb // BLOCKS_PER_CHUNK + 1
    t_idx = t0 + lax.broadcasted_iota(jnp.int32, (Q_BLOCK, 1), 0)
    lane_blk = lax.broadcasted_iota(jnp.int32, (1, Q_BLOCK), 1)
    lane_chunk = lax.broadcasted_iota(jnp.int32, (1, KEY_CHUNK), 1)

    for hh in range(ATT_HEADS):
        g, h = divmod(hh, GQA_GROUP)
        qs_ref[g, h * Q_BLOCK:(h + 1) * Q_BLOCK, :] = q_ref[:, hh * ATT_HEAD_DIM:(hh + 1) * ATT_HEAD_DIM]
    low = lax.broadcasted_iota(jnp.int32, (Q_BLOCK, LANES), 1) < IDX_DIM
    for hh in range(IDX_HEADS):
        tile = qi_ref[:, (hh // 2) * LANES:(hh // 2 + 1) * LANES]
        keep = low if hh % 2 == 0 else jnp.logical_not(low)
        qim_ref[hh] = jnp.where(keep, tile, jnp.zeros_like(tile))
    m_ref[...] = jnp.full(m_ref.shape, MASKED_LOGIT, _F32)
    acc_ref[...] = jnp.zeros(acc_ref.shape, _F32)

    w_all = wi_ref[...]
    idx_scale = (IDX_HEADS ** -0.5) * (IDX_DIM ** -0.5)

    def score_body(c, carry):
        ks = pl.multiple_of(c * KEY_CHUNK, KEY_CHUNK)
        kic = ki_ref[pl.ds(ks, KEY_CHUNK), :]
        sc = jnp.zeros((Q_BLOCK, KEY_CHUNK), _F32)
        for hh in range(IDX_HEADS):
            d = lax.dot_general(qim_ref[hh], kic, _NT, preferred_element_type=_F32)
            sc = sc + w_all[:, IDX_DIM + hh:IDX_DIM + hh + 1] * jnp.maximum(d, 0.0)
        sc = sc * idx_scale
        sc = jnp.where(sc == 0.0, 0.0, sc)
        key = jnp.where(ks + lane_chunk <= t_idx, _sortable(sc), _KEY_NEG_INF)
        for j in range(BLOCKS_PER_CHUNK):
            keys_ref[c * BLOCKS_PER_CHUNK + j] = key[:, j * Q_BLOCK:(j + 1) * Q_BLOCK]
        return carry

    lax.fori_loop(0, n_chunks, score_body, 0)

    def count(pred):
        def body(c, acc):
            for j in range(BLOCKS_PER_CHUNK):
                blk = c * BLOCKS_PER_CHUNK + j
                acc = acc + jnp.where(pred(keys_ref[blk], blk * Q_BLOCK + lane_blk), 1.0, 0.0)
            return acc

        acc = lax.fori_loop(0, n_chunks, body, jnp.zeros((Q_BLOCK, Q_BLOCK), _F32))
        return jnp.sum(acc, axis=1, keepdims=True)

    def value_bit(i, tb):
        cand = tb | lax.shift_left(jnp.int32(1), 31 - i)
        cand_signed = cand ^ jnp.int32(_INT_MIN)
        cnt = count(lambda key, sidx: key >= cand_signed)
        return jnp.where(cnt >= n_sel, cand, tb)

    tau = lax.fori_loop(0, 32, value_bit, jnp.zeros((Q_BLOCK, 1), jnp.int32)) ^ jnp.int32(_INT_MIN)
    n_ge = count(lambda key, sidx: key >= tau)
    tied = jnp.logical_and(n_ge > n_sel, tau > _KEY_NEG_INF)

    def tie_search():
        need = n_sel - count(lambda key, sidx: key > tau)

        def index_bit(i, x):
            cand = x | lax.shift_left(jnp.int32(1), idx_bits - 1 - i)
            cnt = count(lambda key, sidx: jnp.logical_and(key == tau, sidx < cand))
            return jnp.where(cnt < need, cand, x)

        return lax.fori_loop(0, idx_bits, index_bit, jnp.zeros((Q_BLOCK, 1), jnp.int32))

    last_tie = lax.cond(jnp.max(jnp.where(tied, 1.0, 0.0)) > 0.0, tie_search,
                        lambda: jnp.full((Q_BLOCK, 1), 2 ** idx_bits, jnp.int32))

    def attend(key, k_tile, v_tile, s0, lane, bias_ref, enable=None):
        width = key.shape[1]
        sidx = s0 + lane
        sel = jnp.logical_or(key > tau, jnp.logical_and(key == tau, sidx <= last_tie))
        sel = jnp.logical_and(sel, key > _KEY_NEG_INF)
        if enable is not None:
            sel = jnp.logical_and(sel, jnp.zeros_like(sidx) + enable > 0)
        ones = jnp.ones((width, ATT_HEAD_DIM), _BF16)
        for g in range(ATT_KV_HEADS):
            cols = slice(g * ATT_HEAD_DIM, (g + 1) * ATT_HEAD_DIM)
            s = lax.dot_general(qs_ref[g], k_tile[:, cols], _NT, preferred_element_type=_F32)
            s = (s * (ATT_HEAD_DIM ** -0.5)).reshape(GQA_GROUP, Q_BLOCK, width)
            sh = jnp.where(sel[None], s + bias_ref[g * GQA_GROUP:(g + 1) * GQA_GROUP], MASKED_LOGIT)
            m_old = m_ref[g]
            m_new = jnp.maximum(m_old, jnp.max(sh, axis=2, keepdims=True))
            m_ref[g] = m_new
            p = jnp.exp(sh - m_new).reshape(GQA_GROUP * Q_BLOCK, width).astype(_BF16)
            alpha = jnp.exp(m_old - m_new).reshape(GQA_GROUP * Q_BLOCK, 1)
            pv = jnp.dot(p, jnp.concatenate([v_tile[:, cols], ones], axis=1), preferred_element_type=_F32)
            acc_ref[g] = acc_ref[g] * alpha + pv

    n_far = jnp.maximum(qb - 1, 0) // BLOCKS_PER_CHUNK

    def far_body(c, carry):
        ks = pl.multiple_of(c * KEY_CHUNK, KEY_CHUNK)
        key = jnp.concatenate([keys_ref[c * BLOCKS_PER_CHUNK + j] for j in range(BLOCKS_PER_CHUNK)], axis=1)
        attend(key, k_ref[pl.ds(ks, KEY_CHUNK), :], v_ref[pl.ds(ks, KEY_CHUNK), :], ks, lane_chunk, bfar_ref)
        return carry

    lax.fori_loop(0, n_far, far_body, 0)

    def block_step(j, bias_ref, enable=None):
        ks = pl.multiple_of(j * Q_BLOCK, Q_BLOCK)
        attend(keys_ref[j], k_ref[pl.ds(ks, Q_BLOCK), :], v_ref[pl.ds(ks, Q_BLOCK), :], ks, lane_blk, bias_ref,
               enable)

    def tail_body(j, carry):
        block_step(j, bfar_ref)
        return carry

    lax.fori_loop(n_far * BLOCKS_PER_CHUNK, qb - 1, tail_body, 0)
    block_step(jnp.maximum(qb - 1, 0), bprev_ref, enable=jnp.minimum(qb, 1))
    block_step(qb, bdiag_ref)

    for hh in range(ATT_HEADS):
        g, h = divmod(hh, GQA_GROUP)
        rows = slice(h * Q_BLOCK, (h + 1) * Q_BLOCK)
        o_ref[:, hh * ATT_HEAD_DIM:(hh + 1) * ATT_HEAD_DIM] = (
            acc_ref[g, rows, 0:ATT_HEAD_DIM] / acc_ref[g, rows, ATT_HEAD_DIM:2 * ATT_HEAD_DIM]).astype(o_ref.dtype)


def dsa_attention(zb, zf, rel_bias, B, S):
    nq = S // Q_BLOCK
    n_sel = min(TOPK_MAX, S // 4)
    tl = jnp.arange(Q_BLOCK, dtype=jnp.int32)[:, None]
    sr = jnp.arange(Q_BLOCK, dtype=jnp.int32)[None, :]
    bprev = rel_bias[rel_bucket(tl + Q_BLOCK - sr)].transpose(2, 0, 1).astype(_F32)
    bdiag = rel_bias[rel_bucket(tl - sr)].transpose(2, 0, 1).astype(_F32)
    bfar = rel_bias[rel_bucket(jnp.int32(REL_MAX_DIST + 1))].astype(_F32).reshape(ATT_HEADS, 1, 1)
    kernel = functools.partial(_dsa_kernel, n_sel=float(n_sel), idx_bits=int(S - 1).bit_length())
    c_k = ATT_WIDTH // KV_WIDTH
    c_qi = (ATT_WIDTH + 2 * KV_WIDTH) // (IDX_HEADS * IDX_DIM)
    c_ki = (ATT_WIDTH + 2 * KV_WIDTH + IDX_HEADS * IDX_DIM) // LANES
    return pl.pallas_call(
        kernel,
        grid=(B, nq),
        in_specs=[pl.BlockSpec((Q_BLOCK, ATT_WIDTH), lambda b, i: (b * nq + i, 0)),
                  pl.BlockSpec((Q_BLOCK, IDX_HEADS * IDX_DIM), lambda b, i: (b * nq + i, c_qi)),
                  pl.BlockSpec((Q_BLOCK, LANES), lambda b, i: (b * nq + i, ZF_WIDX // LANES)),
                  pl.BlockSpec((S, KV_WIDTH), lambda b, i: (b, c_k)),
                  pl.BlockSpec((S, KV_WIDTH), lambda b, i: (b, c_k + 1)),
                  pl.BlockSpec((S, LANES), lambda b, i: (b, c_ki)),
                  pl.BlockSpec((ATT_HEADS, Q_BLOCK, Q_BLOCK), lambda b, i: (0, 0, 0)),
                  pl.BlockSpec((ATT_HEADS, Q_BLOCK, Q_BLOCK), lambda b, i: (0, 0, 0)),
                  pl.BlockSpec((ATT_HEADS, 1, 1), lambda b, i: (0, 0, 0))],
        out_specs=pl.BlockSpec((Q_BLOCK, ATT_WIDTH), lambda b, i: (b * nq + i, 0)),
        out_shape=jax.ShapeDtypeStruct((B * S, ATT_WIDTH), _BF16),
        scratch_shapes=[pltpu.VMEM((nq, Q_BLOCK, Q_BLOCK), jnp.int32),
                        pltpu.VMEM((ATT_KV_HEADS, GQA_GROUP * Q_BLOCK, ATT_HEAD_DIM), _BF16),
                        pltpu.VMEM((IDX_HEADS, Q_BLOCK, LANES), _BF16),
                        pltpu.VMEM((ATT_KV_HEADS, GQA_GROUP, Q_BLOCK, 1), _F32),
                        pltpu.VMEM((ATT_KV_HEADS, GQA_GROUP * Q_BLOCK, 2 * ATT_HEAD_DIM), _F32)],
        compiler_params=_params("parallel", "arbitrary"),
        name="dsa_attention",
    )(zb, zb, zf, zb, zb, zb, bprev, bdiag, bfar)


def _split_dot(x, w):
    hi = x.astype(_BF16)
    lo = (x - hi.astype(_F32)).astype(_BF16)
    return jnp.dot(hi, w, preferred_element_type=_F32) + jnp.dot(lo, w, preferred_element_type=_F32)


def _split_dot_left(w, x):
    hi = x.astype(_BF16)
    lo = (x - hi.astype(_F32)).astype(_BF16)
    return jnp.dot(w, hi, preferred_element_type=_F32) + jnp.dot(w, lo, preferred_element_type=_F32)


def _rwkv_kernel(slab_ref, mu_ref, vec_ref, wup_ref, aup_ref, gup_ref, o_ref, xs_ref, state_ref):
    C = RWKV_CHUNK
    W = RWKV_WIDTH
    c = pl.program_id(1)

    @pl.when(c == 0)
    def _():
        xs_ref[0:8, :] = jnp.zeros((8, ZF_SLAB_PAD), _F32)
        state_ref[...] = jnp.zeros(state_ref.shape, _F32)

    slab = slab_ref[...]
    xs_ref[8:8 + C, :] = slab
    shifted = xs_ref[7:7 + C, :]
    xs = slab + (shifted - slab) * mu_ref[...]
    xs_ref[0:8, :] = slab[C - 8:C, :]

    r = xs[:, 0:W]
    k = xs[:, W:2 * W]
    v = xs[:, 2 * W:3 * W]
    lora = xs[:, 3 * W:3 * W + LANES]
    xg = xs[:, 3 * W + LANES:3 * W + 3 * LANES]
    w0, a0, k_k, k_a, r_k, gn_w, gn_b = (vec_ref[i:i + 1, :] for i in range(7))
    d = w0 + jnp.dot(jnp.tanh(lora).astype(_BF16), wup_ref[...], preferred_element_type=_F32)
    logw = -math.exp(-0.5) * _sigmoid(d)
    a = _sigmoid(a0 + jnp.dot(lora.astype(_BF16), aup_ref[...], preferred_element_type=_F32))
    g = jnp.dot(_sigmoid(xg).astype(_BF16), gup_ref[...], preferred_element_type=_F32)
    kk_raw = k * k_k
    kt = k * (1.0 + (a - 1.0) * k_a)
    bonus_raw = r * kt * r_k

    ti = lax.broadcasted_iota(jnp.int32, (C, C), 0)
    tj = lax.broadcasted_iota(jnp.int32, (C, C), 1)
    cum = _split_dot_left(jnp.where(tj <= ti, 1.0, 0.0).astype(_BF16), logw)
    cum_end = cum[C - 1:C, :]
    g_incl = jnp.exp(cum)
    g_excl = jnp.exp(cum - logw)
    g_inv = jnp.exp(-cum)
    g_end = jnp.exp(cum_end - cum)
    g_total = jnp.exp(cum_end)

    lo = lax.broadcasted_iota(jnp.int32, (C, LANES), 1) < RWKV_HEAD
    ri = lax.broadcasted_iota(jnp.int32, (LANES, LANES), 0)
    ci = lax.broadcasted_iota(jnp.int32, (LANES, LANES), 1)
    same_head = (ri // RWKV_HEAD) == (ci // RWKV_HEAD)
    ones_bd = jnp.where(same_head, 1.0, 0.0).astype(_BF16)
    eye = ri == ci
    same_blk = (ri // C) == (ci // C)
    strict = jnp.logical_and(same_blk, ci < ri)
    incl = jnp.logical_and(same_blk, ci <= ri)

    def stack(x):
        return jnp.concatenate([jnp.where(lo, x, 0.0), jnp.where(lo, 0.0, x)], axis=0)

    def dot(x, y):
        return jnp.dot(x.astype(_BF16), y.astype(_BF16), preferred_element_type=_F32)

    pairs = range(HEAD_PAIRS)
    sls = [slice(p * LANES, (p + 1) * LANES) for p in pairs]
    h_bd = [state_ref[p] for p in pairs]
    kk = [kk_raw[:, sl] * lax.rsqrt(jnp.maximum(_split_dot(kk_raw[:, sl] * kk_raw[:, sl], ones_bd), 1e-24))
          for sl in sls]
    b = [kk[p] * a[:, sls[p]] for p in pairs]
    lhs4 = [jnp.concatenate([stack(kk[p] * g_excl[:, sls[p]]), stack(r[:, sls[p]] * g_incl[:, sls[p]])],
                            axis=0).astype(_BF16) for p in pairs]
    rhs4 = []
    for p in pairs:
        bi = b[p] * g_inv[:, sls[p]]
        ki = kt[:, sls[p]] * g_inv[:, sls[p]]
        rhs4.append(jnp.concatenate([bi, bi, ki, ki], axis=0).astype(_BF16))
    gm = [lax.dot_general(lhs4[p], rhs4[p], _NT, preferred_element_type=_F32) for p in pairs]
    a_bd = [jnp.where(strict, gm[p][:2 * C, :2 * C], 0.0) for p in pairs]
    bp_bd = [jnp.concatenate([jnp.where(strict, gm[p][:2 * C, 2 * C:], 0.0),
                              jnp.where(incl, gm[p][2 * C:, 2 * C:], 0.0)], axis=0) for p in pairs]
    pb_bd = [jnp.where(incl, gm[p][2 * C:, :2 * C], 0.0) for p in pairs]

    inv = [jnp.where(eye, 1.0, 0.0) - jnp.where((ri // 2) == (ci // 2), a_bd[p], 0.0) for p in pairs]
    s = 2
    while s < C:
        off = jnp.logical_and((ri // (2 * s)) == (ci // (2 * s)), (ri // s) != (ci // s))
        tm = [dot(inv[p], jnp.where(off, a_bd[p], 0.0)) for p in pairs]
        inv = [inv[p] - dot(tm[p], inv[p]) for p in pairs]
        s *= 2

    v_st = [stack(v[:, sl]) for sl in sls]
    hw = [dot(lhs4[p], h_bd[p]) for p in pairs]
    bv = [dot(bp_bd[p], v_st[p]) for p in pairs]
    u_st = [dot(inv[p], hw[p][:2 * C] + bv[p][:2 * C]) for p in pairs]
    y_st = [hw[p][2 * C:] + bv[p][2 * C:] - dot(pb_bd[p], u_st[p]) for p in pairs]
    y = [y_st[p][:C] + y_st[p][C:] for p in pairs]
    u = [u_st[p][:C] + u_st[p][C:] for p in pairs]

    for p in pairs:
        sl = sls[p]
        m_t = jnp.concatenate([kt[:, sl] * g_end[:, sl], -(b[p] * g_end[:, sl])], axis=0).astype(_BF16)
        n_t = jnp.concatenate([v[:, sl], u[p]], axis=0).astype(_BF16)
        dh = lax.dot_general(m_t, n_t, _TN, preferred_element_type=_F32)
        gt_col = jnp.sum(jnp.where(eye, jnp.broadcast_to(g_total[:, sl], (LANES, LANES)), 0.0),
                         axis=1, keepdims=True)
        state_ref[p] = jnp.where(same_head, gt_col * h_bd[p] + dh, 0.0)

    mean = [_split_dot(y[p], ones_bd) * (1.0 / RWKV_HEAD) for p in pairs]
    yc = [y[p] - mean[p] for p in pairs]
    var = [_split_dot(yc[p] * yc[p], ones_bd) * (1.0 / RWKV_HEAD) for p in pairs]
    for p in pairs:
        sl = sls[p]
        yn = yc[p] * lax.rsqrt(var[p] + GN_EPS) * gn_w[:, sl] + gn_b[:, sl]
        bonus = _split_dot(bonus_raw[:, sl], ones_bd) * v[:, sl]
        o_ref[:, sl] = ((yn + bonus) * g[:, sl]).astype(o_ref.dtype)


def rwkv7_mix(zf, mu, w0, w_up, a0, a_up, g_up, k_k, k_a, r_k, gn_w, gn_b, B, S):
    C = RWKV_CHUNK
    nc = S // C
    W = RWKV_WIDTH
    mu_p = jnp.pad(mu, (0, ZF_SLAB_PAD - RWKV_SLAB)).reshape(1, ZF_SLAB_PAD)
    vecs = jnp.stack([w0, a0, k_k, k_a, r_k, gn_w, gn_b, jnp.zeros_like(w0)]).astype(_F32)
    wup = jnp.concatenate([w_up, jnp.zeros((LANES - DECAY_RANK, W), w_up.dtype)], axis=0).astype(_BF16)
    aup = jnp.concatenate([jnp.zeros((DECAY_RANK, W), a_up.dtype), a_up], axis=0).astype(_BF16)
    gup = jnp.concatenate([g_up, jnp.zeros((2 * LANES - GATE_RANK, W), g_up.dtype)], axis=0).astype(_BF16)
    return pl.pallas_call(
        _rwkv_kernel,
        grid=(B, nc),
        in_specs=[pl.BlockSpec((C, ZF_SLAB_PAD), lambda b, c: (b * nc + c, 0)),
                  pl.BlockSpec((1, ZF_SLAB_PAD), lambda b, c: (0, 0)),
                  pl.BlockSpec((8, W), lambda b, c: (0, 0)),
                  pl.BlockSpec((LANES, W), lambda b, c: (0, 0)),
                  pl.BlockSpec((LANES, W), lambda b, c: (0, 0)),
                  pl.BlockSpec((2 * LANES, W), lambda b, c: (0, 0))],
        out_specs=pl.BlockSpec((C, W), lambda b, c: (b * nc + c, 0)),
        out_shape=jax.ShapeDtypeStruct((B * S, W), _BF16),
        scratch_shapes=[pltpu.VMEM((C + 8, ZF_SLAB_PAD), _F32),
                        pltpu.VMEM((HEAD_PAIRS, LANES, LANES), _F32)],
        compiler_params=_params("parallel", "arbitrary"),
        name="rwkv7_mix",
    )(zf, mu_p, vecs, wup, aup, gup)


def _router_kernel(h_ref, g_ref, wr_ref, xn_ref, route_ref):
    xn = _rms_to_bf16(h_ref[...], g_ref[...])
    xn_ref[...] = xn
    logits = jnp.dot(xn, wr_ref[...], preferred_element_type=_F32)
    lane = lax.broadcasted_iota(jnp.int32, logits.shape, 1).astype(_F32)
    l1 = jnp.where(lane < N_EXPERTS, logits, -jnp.inf)
    m1 = jnp.max(l1, axis=1, keepdims=True)
    i1 = jnp.min(jnp.where(l1 == m1, lane, float(LANES)), axis=1, keepdims=True)
    l2 = jnp.where(lane == i1, -jnp.inf, l1)
    m2 = jnp.max(l2, axis=1, keepdims=True)
    i2 = jnp.min(jnp.where(l2 == m2, lane, float(LANES)), axis=1, keepdims=True)
    e2 = jnp.exp(m2 - m1)
    g1 = 1.0 / (1.0 + e2)
    g2 = e2 / (1.0 + e2)
    route_ref[...] = jnp.where(lane == 0.0, i1, jnp.where(lane == 1.0, i2,
                               jnp.where(lane == 2.0, g1, jnp.where(lane == 3.0, g2, 0.0))))


def moe_route(h, g, router, tm):
    T, D = h.shape
    wr = jnp.pad(router, ((0, 0), (0, LANES - N_EXPERTS))).astype(_BF16)
    return pl.pallas_call(
        _router_kernel,
        grid=(T // tm,),
        in_specs=[pl.BlockSpec((tm, D), lambda i: (i, 0)),
                  pl.BlockSpec((1, D), lambda i: (0, 0)),
                  pl.BlockSpec((D, LANES), lambda i: (0, 0))],
        out_specs=[pl.BlockSpec((tm, D), lambda i: (i, 0)),
                   pl.BlockSpec((tm, LANES), lambda i: (i, 0))],
        out_shape=[jax.ShapeDtypeStruct((T, D), _BF16), jax.ShapeDtypeStruct((T, LANES), _F32)],
        compiler_params=_params("parallel"),
        name="moe_route",
    )(h, g.reshape(1, D), wr)


def _moe_ffn_kernel(te_ref, nv_ref, x_ref, w1_ref, w3_ref, w2_ref, o_ref):
    n = pl.program_id(0)
    f = pl.program_id(1)

    @pl.when(jnp.logical_and(n >= nv_ref[0], f == 0))
    def _():
        o_ref[...] = jnp.zeros(o_ref.shape, o_ref.dtype)

    @pl.when(n < nv_ref[0])
    def _():
        x = x_ref[...]
        a = jnp.dot(x, w1_ref[...], preferred_element_type=_F32)
        b = jnp.dot(x, w3_ref[...], preferred_element_type=_F32)
        mid = (a * _sigmoid(a) * b).astype(_BF16)
        contrib = jnp.dot(mid, w2_ref[...], preferred_element_type=_F32)

        @pl.when(f == 0)
        def _():
            o_ref[...] = contrib

        @pl.when(f > 0)
        def _():
            o_ref[...] += contrib


def moe_grouped_ffn(xs, tile_expert, n_valid, w1, w3, w2):
    R, D = xs.shape
    F = w1.shape[2]
    nf = F // MOE_TF

    def f_idx(n, f, nv):
        return jnp.where(n < nv[0], f, nf - 1)

    return pl.pallas_call(
        _moe_ffn_kernel,
        grid_spec=pltpu.PrefetchScalarGridSpec(
            num_scalar_prefetch=2,
            grid=(R // MOE_TM, nf),
            in_specs=[pl.BlockSpec((MOE_TM, D), lambda n, f, te, nv: (n, 0)),
                      pl.BlockSpec((None, D, MOE_TF), lambda n, f, te, nv: (te[n], 0, f_idx(n, f, nv))),
                      pl.BlockSpec((None, D, MOE_TF), lambda n, f, te, nv: (te[n], 0, f_idx(n, f, nv))),
                      pl.BlockSpec((None, MOE_TF, D), lambda n, f, te, nv: (te[n], f_idx(n, f, nv), 0))],
            out_specs=pl.BlockSpec((MOE_TM, D), lambda n, f, te, nv: (n, 0))),
        out_shape=jax.ShapeDtypeStruct((R, D), _F32),
        compiler_params=_params("parallel", "arbitrary"),
        name="moe_grouped_ffn",
    )(tile_expert, n_valid, xs, w1, w3, w2)


def moe_residual(h, g, router, w1, w3, w2):
    T, D = h.shape
    E = N_EXPERTS
    xn, route = moe_route(h, g, router, tm=512)
    top = route[:, 0:TOP_K_EXPERTS].astype(jnp.int32)
    gates = route[:, TOP_K_EXPERTS:2 * TOP_K_EXPERTS]
    ef = top.T.reshape(TOP_K_EXPERTS * T)
    onehot = (ef[:, None] == jnp.arange(E, dtype=jnp.int32)[None, :]).astype(jnp.int32)
    csum = jnp.cumsum(onehot, axis=0)
    rank = jnp.take_along_axis(csum, ef[:, None], axis=1)[:, 0] - 1
    counts = csum[-1]
    padded = ((counts + MOE_TM - 1) // MOE_TM) * MOE_TM
    pends = jnp.cumsum(padded)
    pstarts = pends - padded
    starts = jnp.cumsum(counts) - counts
    pos = pstarts[ef] + rank
    R = TOP_K_EXPERTS * T + E * MOE_TM
    n_tiles = R // MOE_TM
    tile_expert = jnp.searchsorted(pends, jnp.arange(n_tiles, dtype=jnp.int32) * MOE_TM, side='right')
    tile_expert = jnp.minimum(tile_expert, E - 1).astype(jnp.int32)
    n_valid = (pends[-1:] // MOE_TM).astype(jnp.int32)
    order = jnp.argsort(ef, stable=True).astype(jnp.int32)
    rho = jnp.arange(R, dtype=jnp.int32)
    e_row = tile_expert[rho // MOE_TM]
    off = rho - pstarts[e_row]
    src = order[jnp.clip(starts[e_row] + off, 0, TOP_K_EXPERTS * T - 1)] % T
    xs = jnp.take(xn, src, axis=0)
    ys = moe_grouped_ffn(xs, tile_expert, n_valid, w1, w3, w2)
    mix = sum(gates[:, slot:slot + 1] * jnp.take(ys, pos[slot * T:(slot + 1) * T], axis=0)
              for slot in range(TOP_K_EXPERTS))
    return h + mix


def _pack_w_in(w):
    wq, wk, wv, wqi, wki, wwi, wslab, wga, wgb = _split(w, IN_SIZES)
    D = w.shape[0]
    wb = jnp.concatenate([wq, wk, wv, wqi, wki, wki], axis=1)
    wf = jnp.concatenate([wslab, jnp.zeros((D, ZF_SLAB_PAD - RWKV_SLAB), w.dtype),
                          wki, wwi, jnp.zeros((D, LANES - IDX_DIM - IDX_HEADS), w.dtype), wga, wgb], axis=1)
    return wb.astype(_BF16), wf.astype(_BF16)


def kernel(x, p, w_in, att_up, rwkv_up, w_out, rel_bias, rwkv_mu, rwkv_w0, rwkv_w_up,
           rwkv_a0, rwkv_a_up, rwkv_g_up, rwkv_k_k, rwkv_k_a, rwkv_r_k, rwkv_gn_w, rwkv_gn_b,
           norm_mix, norm_ffn, norm_ple, ple_proj, ple_gate, ffn_w1, ffn_w3, ffn_w2,
           moe_router, moe_w1, moe_w3, moe_w2, final_norm):
    B, S, D = x.shape
    T = B * S
    depth = w_in.shape[0]
    h = x.reshape(T, D)
    for i in range(depth):
        wb, wf = _pack_w_in(w_in[i])
        zb = norm_matmul(h, norm_mix[i], wb, _BF16, tm=512, tn=ZB_WIDTH)
        zf = norm_matmul(h, norm_mix[i], wf, _F32, tm=512, tn=1536)
        att = dsa_attention(zb, zf, rel_bias, B, S)
        rw = rwkv7_mix(zf, rwkv_mu[i], rwkv_w0[i], rwkv_w_up[i], rwkv_a0[i], rwkv_a_up[i], rwkv_g_up[i],
                       rwkv_k_k[i], rwkv_k_a[i], rwkv_r_k[i], rwkv_gn_w[i], rwkv_gn_b[i], B, S)
        merged = gated_merge(att, rw, zf, att_up[i].astype(_BF16), rwkv_up[i].astype(_BF16), tm=512, tn=512)
        h = matmul_residual(merged, w_out[i].astype(_BF16), h, tm=512, tn=1024)
        if i % 2 == 0:
            j = i // 2
            h = ffn_residual(h, norm_ffn[i], ffn_w1[j].astype(_BF16), ffn_w3[j].astype(_BF16),
                             ffn_w2[j].astype(_BF16), tm=512, tf=512)
        else:
            j = i // 2
            h = moe_residual(h, norm_ffn[i], moe_router[j], moe_w1[j].astype(_BF16), moe_w3[j].astype(_BF16),
                             moe_w2[j].astype(_BF16))
        h = ple_residual(h, norm_ple[i], p[i].reshape(T, PLE_DIM), ple_gate[i].astype(_BF16),
                         ple_proj[i].astype(_BF16), final_norm, final=(i == depth - 1), tm=256)
    return h.reshape(B, S, D)
```

```python
import functools
import math

import jax
import jax.numpy as jnp
import numpy as np
from jax import lax
from jax.experimental import pallas as pl
from jax.experimental.pallas import tpu as pltpu

D_MODEL = 2048
DEPTH = 2
ATT_HEADS = 8
ATT_KV_HEADS = 2
ATT_HEAD_DIM = 128
IDX_HEADS = 8
IDX_DIM = 64
TOPK_MAX = 256
REL_BUCKETS = 32
REL_MAX_DIST = 128
RWKV_HEAD = 64
RWKV_HEADS = 16
DECAY_RANK = 64
ICLR_RANK = 64
GATE_RANK = 160
D_FF = 5632
N_EXPERTS = 8
TOP_K_EXPERTS = 2
D_FF_EXPERT = 7168
PLE_DIM = 256
RMS_EPS = 1e-6
GN_EPS = 64e-5

ATT_WIDTH = ATT_HEADS * ATT_HEAD_DIM
KV_WIDTH = ATT_KV_HEADS * ATT_HEAD_DIM
RWKV_WIDTH = RWKV_HEADS * RWKV_HEAD
RWKV_SIZES = (RWKV_WIDTH, RWKV_WIDTH, RWKV_WIDTH, DECAY_RANK, ICLR_RANK, GATE_RANK)
RWKV_SLAB = 3 * RWKV_WIDTH + DECAY_RANK + ICLR_RANK + GATE_RANK
IN_SIZES = (ATT_WIDTH, KV_WIDTH, KV_WIDTH, IDX_HEADS * IDX_DIM, IDX_DIM, IDX_HEADS, RWKV_SLAB, D_MODEL, D_MODEL)

LANES = 128
VMEM_LIMIT = 56 * 1024 * 1024

Q_BLOCK = 128
KEY_CHUNK = 512
BLOCKS_PER_CHUNK = KEY_CHUNK // Q_BLOCK
BLOCKS_PER_GROUP = 32
MASKED_LOGIT = -1e30
LOG2_E = math.log2(math.e)
GQA_GROUP = ATT_HEADS // ATT_KV_HEADS

ZB_WIDTH = ATT_WIDTH + 2 * KV_WIDTH + IDX_HEADS * IDX_DIM + 2 * IDX_DIM
ZF_SLAB_PAD = 3456
ZF_WIDX = ZF_SLAB_PAD
ZF_GATE_A = ZF_WIDX + LANES
ZF_GATE_B = ZF_GATE_A + D_MODEL
ZF_WIDTH = ZF_GATE_B + D_MODEL

RWKV_CHUNK = 64
HEAD_PAIRS = RWKV_HEADS // 2
MOE_TM = 512
MOE_TF = 1024

_F32 = jnp.float32
_BF16 = jnp.bfloat16
_NT = (((1,), (1,)), ((), ()))
_TN = (((0,), (0,)), ((), ()))


def _split(z, sizes):
    return jnp.split(z, np.cumsum(sizes)[:-1].tolist(), axis=-1)


def _sigmoid(x):
    return 1.0 / (1.0 + jnp.exp(-x))


def _rms_to_bf16(x, g):
    y = x * lax.rsqrt(jnp.mean(x * x, axis=-1, keepdims=True) + RMS_EPS)
    return (y * g).astype(_BF16)


def _params(*sem):
    return pltpu.CompilerParams(dimension_semantics=sem, vmem_limit_bytes=VMEM_LIMIT)


def _norm_mm_kernel(x_ref, g_ref, w_ref, o_ref, xn_ref):
    @pl.when(pl.program_id(1) == 0)
    def _():
        xn_ref[...] = _rms_to_bf16(x_ref[...], g_ref[...])

    o_ref[...] = jnp.dot(xn_ref[...], w_ref[...], preferred_element_type=_F32).astype(o_ref.dtype)


def norm_matmul(x, g, w, out_dtype, tm, tn):
    T, D = x.shape
    N = w.shape[1]
    return pl.pallas_call(
        _norm_mm_kernel,
        grid=(T // tm, N // tn),
        in_specs=[pl.BlockSpec((tm, D), lambda i, j: (i, 0)),
                  pl.BlockSpec((1, D), lambda i, j: (0, 0)),
                  pl.BlockSpec((D, tn), lambda i, j: (0, j))],
        out_specs=pl.BlockSpec((tm, tn), lambda i, j: (i, j)),
        out_shape=jax.ShapeDtypeStruct((T, N), out_dtype),
        scratch_shapes=[pltpu.VMEM((tm, D), _BF16)],
        compiler_params=_params("parallel", "arbitrary"),
        name="norm_matmul",
    )(x, g.reshape(1, D), w)


def _merge_kernel(att_ref, rw_ref, ga_ref, gb_ref, wa_ref, wr_ref, o_ref):
    a = jnp.dot(att_ref[...], wa_ref[...], preferred_element_type=_F32)
    r = jnp.dot(rw_ref[...], wr_ref[...], preferred_element_type=_F32)
    o_ref[...] = (_sigmoid(ga_ref[...]) * a + _sigmoid(gb_ref[...]) * r).astype(o_ref.dtype)


def gated_merge(att, rw, zf, wa, wr, tm, tn):
    T, K = att.shape
    N = wa.shape[1]
    ja, jb = ZF_GATE_A // tn, ZF_GATE_B // tn
    return pl.pallas_call(
        _merge_kernel,
        grid=(T // tm, N // tn),
        in_specs=[pl.BlockSpec((tm, K), lambda i, j: (i, 0)),
                  pl.BlockSpec((tm, K), lambda i, j: (i, 0)),
                  pl.BlockSpec((tm, tn), lambda i, j: (i, ja + j)),
                  pl.BlockSpec((tm, tn), lambda i, j: (i, jb + j)),
                  pl.BlockSpec((K, tn), lambda i, j: (0, j)),
                  pl.BlockSpec((K, tn), lambda i, j: (0, j))],
        out_specs=pl.BlockSpec((tm, tn), lambda i, j: (i, j)),
        out_shape=jax.ShapeDtypeStruct((T, N), _BF16),
        compiler_params=_params("parallel", "arbitrary"),
        name="gated_merge",
    )(att, rw, zf, zf, wa, wr)


def _mm_res_kernel(x_ref, w_ref, h_ref, o_ref):
    o_ref[...] = h_ref[...] + jnp.dot(x_ref[...], w_ref[...], preferred_element_type=_F32)


def matmul_residual(x, w, h, tm, tn):
    T, K = x.shape
    N = w.shape[1]
    return pl.pallas_call(
        _mm_res_kernel,
        grid=(T // tm, N // tn),
        in_specs=[pl.BlockSpec((tm, K), lambda i, j: (i, 0)),
                  pl.BlockSpec((K, tn), lambda i, j: (0, j)),
                  pl.BlockSpec((tm, tn), lambda i, j: (i, j))],
        out_specs=pl.BlockSpec((tm, tn), lambda i, j: (i, j)),
        out_shape=jax.ShapeDtypeStruct((T, N), _F32),
        compiler_params=_params("parallel", "arbitrary"),
        name="matmul_residual",
    )(x, w, h)


def _ffn_kernel(h_ref, g_ref, w1_ref, w3_ref, w2_ref, o_ref, xn_ref):
    f = pl.program_id(1)

    @pl.when(f == 0)
    def _():
        xn_ref[...] = _rms_to_bf16(h_ref[...], g_ref[...])
        o_ref[...] = h_ref[...]

    xn = xn_ref[...]
    a = jnp.dot(xn, w1_ref[...], preferred_element_type=_F32)
    b = jnp.dot(xn, w3_ref[...], preferred_element_type=_F32)
    mid = (a * _sigmoid(a) * b).astype(_BF16)
    o_ref[...] += jnp.dot(mid, w2_ref[...], preferred_element_type=_F32)


def ffn_residual(h, g, w1, w3, w2, tm, tf):
    T, D = h.shape
    F = w1.shape[1]
    return pl.pallas_call(
        _ffn_kernel,
        grid=(T // tm, F // tf),
        in_specs=[pl.BlockSpec((tm, D), lambda i, f: (i, 0)),
                  pl.BlockSpec((1, D), lambda i, f: (0, 0)),
                  pl.BlockSpec((D, tf), lambda i, f: (0, f)),
                  pl.BlockSpec((D, tf), lambda i, f: (0, f)),
                  pl.BlockSpec((tf, D), lambda i, f: (f, 0))],
        out_specs=pl.BlockSpec((tm, D), lambda i, f: (i, 0)),
        out_shape=jax.ShapeDtypeStruct((T, D), _F32),
        scratch_shapes=[pltpu.VMEM((tm, D), _BF16)],
        compiler_params=_params("parallel", "arbitrary"),
        name="ffn_residual",
    )(h, g.reshape(1, D), w1, w3, w2)


def _ple_kernel(h_ref, g_ref, p_ref, wg_ref, wp_ref, fg_ref, o_ref, *, final):
    h = h_ref[...]
    xn = _rms_to_bf16(h, g_ref[...])
    gate = _sigmoid(jnp.dot(xn, wg_ref[...], preferred_element_type=_F32))
    pp = jnp.dot(p_ref[...].astype(_BF16), wp_ref[...], preferred_element_type=_F32)
    out = h + gate * pp
    if final:
        out = out * lax.rsqrt(jnp.mean(out * out, axis=-1, keepdims=True) + RMS_EPS) * fg_ref[...]
    o_ref[...] = out


def ple_residual(h, g, p, wg, wp, final_g, final, tm):
    T, D = h.shape
    P = p.shape[1]
    return pl.pallas_call(
        functools.partial(_ple_kernel, final=final),
        grid=(T // tm,),
        in_specs=[pl.BlockSpec((tm, D), lambda i: (i, 0)),
                  pl.BlockSpec((1, D), lambda i: (0, 0)),
                  pl.BlockSpec((tm, P), lambda i: (i, 0)),
                  pl.BlockSpec((D, D), lambda i: (0, 0)),
                  pl.BlockSpec((P, D), lambda i: (0, 0)),
                  pl.BlockSpec((1, D), lambda i: (0, 0))],
        out_specs=pl.BlockSpec((tm, D), lambda i: (i, 0)),
        out_shape=jax.ShapeDtypeStruct((T, D), _F32),
        compiler_params=_params("parallel"),
        name="ple_residual",
    )(h, g.reshape(1, D), p, wg, wp, final_g.reshape(1, D))


def rel_bucket(dist):
    max_exact = REL_BUCKETS // 2
    dist = jnp.maximum(dist, 0)
    d_f = jnp.maximum(dist, 1).astype(_F32)
    large = max_exact + (jnp.log(d_f / max_exact) / math.log(REL_MAX_DIST / max_exact)
                         * (REL_BUCKETS - max_exact)).astype(jnp.int32)
    large = jnp.minimum(large, REL_BUCKETS - 1)
    return jnp.where(dist < max_exact, dist, large)


def _sortable(x):
    b = lax.bitcast_convert_type(x, jnp.int32)
    return b ^ ((b >> 31) & jnp.int32(0x7FFFFFFF))


_INT_MIN = -2 ** 31
_KEY_NEG_INF = 0x807FFFFF - 2 ** 32


def _dsa_kernel(q_ref, qi_ref, wi_ref, k_ref, v_ref, ki_ref, bprev_ref, bdiag_ref, bfar_ref, o_ref,
                keys_ref, planes_ref, active_ref, ones_ref, qs_ref, qim_ref, m_ref, acc_ref, *, n_sel, idx_bits):
    qb = pl.program_id(1)
    t0 = qb * Q_BLOCK
    n_chunks = qb // BLOCKS_PER_CHUNK + 1
    t_idx = t0 + lax.broadcasted_iota(jnp.int32, (Q_BLOCK, 1), 0)
    lane_blk = lax.broadcasted_iota(jnp.int32, (1, Q_BLOCK), 1)
    lane_chunk = lax.broadcasted_iota(jnp.int32, (1, KEY_CHUNK), 1)

    for hh in range(ATT_HEADS):
        g, h = divmod(hh, GQA_GROUP)
        qs_ref[g, h * Q_BLOCK:(h + 1) * Q_BLOCK, :] = q_ref[:, hh * ATT_HEAD_DIM:(hh + 1) * ATT_HEAD_DIM]
    low = lax.broadcasted_iota(jnp.int32, (Q_BLOCK, LANES), 1) < IDX_DIM
    for hh in range(IDX_HEADS):
        tile = qi_ref[:, (hh // 2) * LANES:(hh // 2 + 1) * LANES]
        keep = low if hh % 2 == 0 else jnp.logical_not(low)
        qim_ref[hh] = jnp.where(keep, tile, jnp.zeros_like(tile))
    m_ref[...] = jnp.full(m_ref.shape, MASKED_LOGIT, _F32)

    @pl.when(qb == 0)
    def _():
        keys_ref[BLOCKS_PER_CHUNK:] = jnp.zeros((keys_ref.shape[0] - BLOCKS_PER_CHUNK, Q_BLOCK, Q_BLOCK), jnp.int32)
    acc_ref[...] = jnp.zeros(acc_ref.shape, _F32)

    w_all = wi_ref[...]
    idx_scale = (IDX_HEADS ** -0.5) * (IDX_DIM ** -0.5)

    def score_body(c, carry):
        ks = pl.multiple_of(c * KEY_CHUNK, KEY_CHUNK)
        kic = ki_ref[pl.ds(ks, KEY_CHUNK), :]
        sc = jnp.zeros((Q_BLOCK, KEY_CHUNK), _F32)
        for hh in range(IDX_HEADS):
            d = lax.dot_general(qim_ref[hh], kic, _NT, preferred_element_type=_F32)
            sc = sc + w_all[:, IDX_DIM + hh:IDX_DIM + hh + 1] * jnp.maximum(d, 0.0)
        sc = sc * idx_scale
        sc = jnp.where(sc == 0.0, 0.0, sc)
        key = jnp.where(ks + lane_chunk <= t_idx, _sortable(sc), _KEY_NEG_INF)
        for j in range(BLOCKS_PER_CHUNK):
            keys_ref[c * BLOCKS_PER_CHUNK + j] = key[:, j * Q_BLOCK:(j + 1) * Q_BLOCK]
        return carry

    lax.fori_loop(0, n_chunks, score_body, 0)

    def count(pred):
        def body(c, acc):
            for j in range(BLOCKS_PER_CHUNK):
                blk = c * BLOCKS_PER_CHUNK + j
                acc = acc + jnp.where(pred(keys_ref[blk], blk * Q_BLOCK + lane_blk), 1.0, 0.0)
            return acc

        acc = lax.fori_loop(0, n_chunks, body, jnp.zeros((Q_BLOCK, Q_BLOCK), _F32))
        return jnp.sum(acc, axis=1, keepdims=True)

    n_blocks = n_chunks * BLOCKS_PER_CHUNK
    n_groups = (n_blocks + BLOCKS_PER_GROUP - 1) // BLOCKS_PER_GROUP

    def transpose_group(g, carry):
        def rows(rg, carry_rows):
            r0 = pl.multiple_of(rg * 8, 8)
            x = [keys_ref[g * BLOCKS_PER_GROUP + j, pl.ds(r0, 8), :] for j in range(BLOCKS_PER_GROUP)]
            j, m = 16, 0x0000FFFF
            while j:
                k = 0
                while k < BLOCKS_PER_GROUP:
                    t = (x[k] ^ lax.shift_right_logical(x[k + j], jnp.int32(j))) & jnp.int32(m)
                    x[k] = x[k] ^ t
                    x[k + j] = x[k + j] ^ lax.shift_left(t, jnp.int32(j))
                    k = (k + j + 1) & ~j
                j >>= 1
                m ^= m << j
            planes_ref[g, 0, pl.ds(r0, 8), :] = ~x[0]
            for i in range(1, 32):
                planes_ref[g, i, pl.ds(r0, 8), :] = x[i]
            return carry_rows

        lax.fori_loop(0, Q_BLOCK // 8, rows, 0)
        n_live = jnp.minimum(n_blocks - g * BLOCKS_PER_GROUP, BLOCKS_PER_GROUP)
        active_ref[g] = lax.shift_left(jnp.full((Q_BLOCK, Q_BLOCK), -1, jnp.int32), BLOCKS_PER_GROUP - n_live)
        return carry

    lax.fori_loop(0, n_groups, transpose_group, 0)

    def radix_bit(i, carry):
        prefix, above = carry

        def tally(g, acc):
            ones = active_ref[g] & planes_ref[g, i]
            ones_ref[g] = ones
            return acc + lax.population_count(ones)

        cnt = lax.fori_loop(0, n_groups, tally, jnp.zeros((Q_BLOCK, Q_BLOCK), jnp.int32))
        with_bit = jnp.sum(cnt.astype(_F32), axis=1, keepdims=True)
        take = above + with_bit >= n_sel

        def narrow(g, carry_g):
            ones = ones_ref[g]
            active_ref[g] = jnp.where(take, ones, active_ref[g] ^ ones)
            return carry_g

        lax.fori_loop(0, n_groups, narrow, 0)
        prefix = jnp.where(take, prefix | lax.shift_left(jnp.int32(1), 31 - i), prefix)
        return prefix, jnp.where(take, above, above + with_bit)

    prefix, n_gt = lax.fori_loop(0, 32, radix_bit, (jnp.zeros((Q_BLOCK, 1), jnp.int32),
                                                    jnp.zeros((Q_BLOCK, 1), _F32)))
    tau = prefix ^ jnp.int32(_INT_MIN)

    def tally_equal(g, acc):
        return acc + lax.population_count(active_ref[g])

    n_eq = jnp.sum(lax.fori_loop(0, n_groups, tally_equal, jnp.zeros((Q_BLOCK, Q_BLOCK), jnp.int32)).astype(_F32),
                   axis=1, keepdims=True)
    tied = jnp.logical_and(n_gt + n_eq > n_sel, tau > _KEY_NEG_INF)

    def tie_search():
        need = n_sel - count(lambda key, sidx: key > tau)

        def index_bit(i, x):
            cand = x | lax.shift_left(jnp.int32(1), idx_bits - 1 - i)
            cnt = count(lambda key, sidx: jnp.logical_and(key == tau, sidx < cand))
            return jnp.where(cnt < need, cand, x)

        return lax.fori_loop(0, idx_bits, index_bit, jnp.zeros((Q_BLOCK, 1), jnp.int32))

    last_tie = lax.cond(jnp.max(jnp.where(tied, 1.0, 0.0)) > 0.0, tie_search,
                        lambda: jnp.full((Q_BLOCK, 1), 2 ** idx_bits, jnp.int32))

    def attend(key, k_tile, v_tile, s0, lane, bias_ref, enable=None, const_bias=False):
        width = key.shape[1]
        sidx = s0 + lane
        sel = jnp.logical_or(key > tau, jnp.logical_and(key == tau, sidx <= last_tie))
        sel = jnp.logical_and(sel, key > _KEY_NEG_INF)
        if enable is not None:
            sel = jnp.logical_and(sel, jnp.zeros_like(sidx) + enable > 0)
        ones = jnp.ones((width, ATT_HEAD_DIM), _BF16)
        groups = range(ATT_KV_HEADS)
        cols = [slice(g * ATT_HEAD_DIM, (g + 1) * ATT_HEAD_DIM) for g in groups]

        def logits(g):
            return lax.dot_general(qs_ref[g], k_tile[:, cols[g]], _NT,
                                   preferred_element_type=_F32).reshape(GQA_GROUP, Q_BLOCK, width)

        def softmax(g, s):
            bias = bias_ref[g * GQA_GROUP:(g + 1) * GQA_GROUP]
            m_old = m_ref[g]
            if const_bias:
                sh = jnp.where(sel[None], s, MASKED_LOGIT)
                m_new = jnp.maximum(m_old, jnp.max(sh, axis=2, keepdims=True) + bias)
                shift = m_new - bias
            else:
                sh = jnp.where(sel[None], s + bias, MASKED_LOGIT)
                m_new = jnp.maximum(m_old, jnp.max(sh, axis=2, keepdims=True))
                shift = m_new
            m_ref[g] = m_new
            return (jnp.exp2(sh - shift).reshape(GQA_GROUP * Q_BLOCK, width).astype(_BF16),
                    jnp.exp2(m_old - m_new).reshape(GQA_GROUP * Q_BLOCK, 1))

        def accumulate(g, p, alpha):
            pv = jnp.dot(p, jnp.concatenate([v_tile[:, cols[g]], ones], axis=1), preferred_element_type=_F32)
            acc_ref[g] = acc_ref[g] * alpha + pv

        if width <= Q_BLOCK:
            s = [logits(g) for g in groups]
            pa = [softmax(g, s[g]) for g in groups]
            for g in groups:
                accumulate(g, *pa[g])
        else:
            for g in groups:
                accumulate(g, *softmax(g, logits(g)))

    n_far = jnp.maximum(qb - 1, 0) // BLOCKS_PER_CHUNK

    def far_body(c, carry):
        ks = pl.multiple_of(c * KEY_CHUNK, KEY_CHUNK)
        key = jnp.concatenate([keys_ref[c * BLOCKS_PER_CHUNK + j] for j in range(BLOCKS_PER_CHUNK)], axis=1)
        attend(key, k_ref[pl.ds(ks, KEY_CHUNK), :], v_ref[pl.ds(ks, KEY_CHUNK), :], ks, lane_chunk, bfar_ref,
               const_bias=True)
        return carry

    lax.fori_loop(0, n_far, far_body, 0)

    def block_step(j, bias_ref, enable=None, const_bias=False):
        ks = pl.multiple_of(j * Q_BLOCK, Q_BLOCK)
        attend(keys_ref[j], k_ref[pl.ds(ks, Q_BLOCK), :], v_ref[pl.ds(ks, Q_BLOCK), :], ks, lane_blk, bias_ref,
               enable, const_bias)

    def tail_body(j, carry):
        block_step(j, bfar_ref, const_bias=True)
        return carry

    lax.fori_loop(n_far * BLOCKS_PER_CHUNK, qb - 1, tail_body, 0)
    block_step(jnp.maximum(qb - 1, 0), bprev_ref, enable=jnp.minimum(qb, 1))
    block_step(qb, bdiag_ref)

    for hh in range(ATT_HEADS):
        g, h = divmod(hh, GQA_GROUP)
        rows = slice(h * Q_BLOCK, (h + 1) * Q_BLOCK)
        o_ref[:, hh * ATT_HEAD_DIM:(hh + 1) * ATT_HEAD_DIM] = (
            acc_ref[g, rows, 0:ATT_HEAD_DIM] / acc_ref[g, rows, ATT_HEAD_DIM:2 * ATT_HEAD_DIM]).astype(o_ref.dtype)


def dsa_attention(zb, zf, rel_bias, B, S):
    nq = S // Q_BLOCK
    n_grp = -(-nq // BLOCKS_PER_GROUP)
    n_sel = min(TOPK_MAX, S // 4)
    tl = jnp.arange(Q_BLOCK, dtype=jnp.int32)[:, None]
    sr = jnp.arange(Q_BLOCK, dtype=jnp.int32)[None, :]
    rel_bias = rel_bias.astype(_F32) * LOG2_E
    bprev = rel_bias[rel_bucket(tl + Q_BLOCK - sr)].transpose(2, 0, 1)
    bdiag = rel_bias[rel_bucket(tl - sr)].transpose(2, 0, 1)
    bfar = rel_bias[rel_bucket(jnp.int32(REL_MAX_DIST + 1))].reshape(ATT_HEADS, 1, 1)
    kernel = functools.partial(_dsa_kernel, n_sel=float(n_sel), idx_bits=int(S - 1).bit_length())
    c_k = ATT_WIDTH // KV_WIDTH
    c_qi = (ATT_WIDTH + 2 * KV_WIDTH) // (IDX_HEADS * IDX_DIM)
    c_ki = (ATT_WIDTH + 2 * KV_WIDTH + IDX_HEADS * IDX_DIM) // LANES
    return pl.pallas_call(
        kernel,
        grid=(B, nq),
        in_specs=[pl.BlockSpec((Q_BLOCK, ATT_WIDTH), lambda b, i: (b * nq + i, 0)),
                  pl.BlockSpec((Q_BLOCK, IDX_HEADS * IDX_DIM), lambda b, i: (b * nq + i, c_qi)),
                  pl.BlockSpec((Q_BLOCK, LANES), lambda b, i: (b * nq + i, ZF_WIDX // LANES)),
                  pl.BlockSpec((S, KV_WIDTH), lambda b, i: (b, c_k)),
                  pl.BlockSpec((S, KV_WIDTH), lambda b, i: (b, c_k + 1)),
                  pl.BlockSpec((S, LANES), lambda b, i: (b, c_ki)),
                  pl.BlockSpec((ATT_HEADS, Q_BLOCK, Q_BLOCK), lambda b, i: (0, 0, 0)),
                  pl.BlockSpec((ATT_HEADS, Q_BLOCK, Q_BLOCK), lambda b, i: (0, 0, 0)),
                  pl.BlockSpec((ATT_HEADS, 1, 1), lambda b, i: (0, 0, 0))],
        out_specs=pl.BlockSpec((Q_BLOCK, ATT_WIDTH), lambda b, i: (b * nq + i, 0)),
        out_shape=jax.ShapeDtypeStruct((B * S, ATT_WIDTH), _BF16),
        scratch_shapes=[pltpu.VMEM((n_grp * BLOCKS_PER_GROUP, Q_BLOCK, Q_BLOCK), jnp.int32),
                        pltpu.VMEM((n_grp, 32, Q_BLOCK, Q_BLOCK), jnp.int32),
                        pltpu.VMEM((n_grp, Q_BLOCK, Q_BLOCK), jnp.int32),
                        pltpu.VMEM((n_grp, Q_BLOCK, Q_BLOCK), jnp.int32),
                        pltpu.VMEM((ATT_KV_HEADS, GQA_GROUP * Q_BLOCK, ATT_HEAD_DIM), _BF16),
                        pltpu.VMEM((IDX_HEADS, Q_BLOCK, LANES), _BF16),
                        pltpu.VMEM((ATT_KV_HEADS, GQA_GROUP, Q_BLOCK, 1), _F32),
                        pltpu.VMEM((ATT_KV_HEADS, GQA_GROUP * Q_BLOCK, 2 * ATT_HEAD_DIM), _F32)],
        compiler_params=_params("parallel", "arbitrary"),
        name="dsa_attention",
    )(zb, zb, zf, zb, zb, zb, bprev, bdiag, bfar)


def _split_dot(x, w):
    hi = x.astype(_BF16)
    lo = (x - hi.astype(_F32)).astype(_BF16)
    return jnp.dot(hi, w, preferred_element_type=_F32) + jnp.dot(lo, w, preferred_element_type=_F32)


def _split_dot_left(w, x):
    hi = x.astype(_BF16)
    lo = (x - hi.astype(_F32)).astype(_BF16)
    return jnp.dot(w, hi, preferred_element_type=_F32) + jnp.dot(w, lo, preferred_element_type=_F32)


def _rwkv_kernel(slab_ref, mu_ref, vec_ref, wup_ref, aup_ref, gup_ref, o_ref, xs_ref, state_ref):
    C = RWKV_CHUNK
    W = RWKV_WIDTH
    c = pl.program_id(1)

    @pl.when(c == 0)
    def _():
        xs_ref[0:8, :] = jnp.zeros((8, ZF_SLAB_PAD), _F32)
        state_ref[...] = jnp.zeros(state_ref.shape, _F32)

    slab = slab_ref[...]
    xs_ref[8:8 + C, :] = slab
    shifted = xs_ref[7:7 + C, :]
    xs = slab + (shifted - slab) * mu_ref[...]
    xs_ref[0:8, :] = slab[C - 8:C, :]

    r = xs[:, 0:W]
    k = xs[:, W:2 * W]
    v = xs[:, 2 * W:3 * W]
    lora = xs[:, 3 * W:3 * W + LANES]
    xg = xs[:, 3 * W + LANES:3 * W + 3 * LANES]
    w0, a0, k_k, k_a, r_k, gn_w, gn_b = (vec_ref[i:i + 1, :] for i in range(7))
    d = w0 + jnp.dot(jnp.tanh(lora).astype(_BF16), wup_ref[...], preferred_element_type=_F32)
    logw = -math.exp(-0.5) * _sigmoid(d)
    a = _sigmoid(a0 + jnp.dot(lora.astype(_BF16), aup_ref[...], preferred_element_type=_F32))
    g = jnp.dot(_sigmoid(xg).astype(_BF16), gup_ref[...], preferred_element_type=_F32)
    kk_raw = k * k_k
    kt = k * (1.0 + (a - 1.0) * k_a)
    bonus_raw = r * kt * r_k

    ti = lax.broadcasted_iota(jnp.int32, (C, C), 0)
    tj = lax.broadcasted_iota(jnp.int32, (C, C), 1)
    cum = _split_dot_left(jnp.where(tj <= ti, 1.0, 0.0).astype(_BF16), logw)
    cum_end = cum[C - 1:C, :]
    g_incl = jnp.exp(cum)
    g_excl = jnp.exp(cum - logw)
    g_inv = jnp.exp(-cum)
    g_end = jnp.exp(cum_end - cum)
    g_total = jnp.exp(cum_end)

    lo = lax.broadcasted_iota(jnp.int32, (C, LANES), 1) < RWKV_HEAD
    ri = lax.broadcasted_iota(jnp.int32, (LANES, LANES), 0)
    ci = lax.broadcasted_iota(jnp.int32, (LANES, LANES), 1)
    same_head = (ri // RWKV_HEAD) == (ci // RWKV_HEAD)
    ones_bd = jnp.where(same_head, 1.0, 0.0).astype(_BF16)
    eye = ri == ci
    same_blk = (ri // C) == (ci // C)
    strict = jnp.logical_and(same_blk, ci < ri)
    incl = jnp.logical_and(same_blk, ci <= ri)

    def stack(x):
        return jnp.concatenate([jnp.where(lo, x, 0.0), jnp.where(lo, 0.0, x)], axis=0)

    def dot(x, y):
        return jnp.dot(x.astype(_BF16), y.astype(_BF16), preferred_element_type=_F32)

    pairs = range(HEAD_PAIRS)
    sls = [slice(p * LANES, (p + 1) * LANES) for p in pairs]
    h_bd = [state_ref[p] for p in pairs]
    kk = [kk_raw[:, sl] * lax.rsqrt(jnp.maximum(_split_dot(kk_raw[:, sl] * kk_raw[:, sl], ones_bd), 1e-24))
          for sl in sls]
    b = [kk[p] * a[:, sls[p]] for p in pairs]
    lhs4 = [jnp.concatenate([stack(kk[p] * g_excl[:, sls[p]]), stack(r[:, sls[p]] * g_incl[:, sls[p]])],
                            axis=0).astype(_BF16) for p in pairs]
    rhs4 = []
    for p in pairs:
        bi = b[p] * g_inv[:, sls[p]]
        ki = kt[:, sls[p]] * g_inv[:, sls[p]]
        rhs4.append(jnp.concatenate([bi, bi, ki, ki], axis=0).astype(_BF16))
    gm = [lax.dot_general(lhs4[p], rhs4[p], _NT, preferred_element_type=_F32) for p in pairs]
    a_bd = [jnp.where(strict, gm[p][:2 * C, :2 * C], 0.0) for p in pairs]
    bp_bd = [jnp.concatenate([jnp.where(strict, gm[p][:2 * C, 2 * C:], 0.0),
                              jnp.where(incl, gm[p][2 * C:, 2 * C:], 0.0)], axis=0) for p in pairs]
    pb_bd = [jnp.where(incl, gm[p][2 * C:, :2 * C], 0.0) for p in pairs]

    inv = [jnp.where(eye, 1.0, 0.0) - jnp.where((ri // 2) == (ci // 2), a_bd[p], 0.0) for p in pairs]
    s = 2
    while s < C:
        off = jnp.logical_and((ri // (2 * s)) == (ci // (2 * s)), (ri // s) != (ci // s))
        tm = [dot(inv[p], jnp.where(off, a_bd[p], 0.0)) for p in pairs]
        inv = [inv[p] - dot(tm[p], inv[p]) for p in pairs]
        s *= 2

    v_st = [stack(v[:, sl]) for sl in sls]
    hw = [dot(lhs4[p], h_bd[p]) for p in pairs]
    bv = [dot(bp_bd[p], v_st[p]) for p in pairs]
    u_st = [dot(inv[p], hw[p][:2 * C] + bv[p][:2 * C]) for p in pairs]
    y_st = [hw[p][2 * C:] + bv[p][2 * C:] - dot(pb_bd[p], u_st[p]) for p in pairs]
    y = [y_st[p][:C] + y_st[p][C:] for p in pairs]
    u = [u_st[p][:C] + u_st[p][C:] for p in pairs]

    for p in pairs:
        sl = sls[p]
        m_t = jnp.concatenate([kt[:, sl] * g_end[:, sl], -(b[p] * g_end[:, sl])], axis=0).astype(_BF16)
        n_t = jnp.concatenate([v[:, sl], u[p]], axis=0).astype(_BF16)
        dh = lax.dot_general(m_t, n_t, _TN, preferred_element_type=_F32)
        gt_col = jnp.sum(jnp.where(eye, jnp.broadcast_to(g_total[:, sl], (LANES, LANES)), 0.0),
                         axis=1, keepdims=True)
        state_ref[p] = jnp.where(same_head, gt_col * h_bd[p] + dh, 0.0)

    mean = [_split_dot(y[p], ones_bd) * (1.0 / RWKV_HEAD) for p in pairs]
    yc = [y[p] - mean[p] for p in pairs]
    var = [_split_dot(yc[p] * yc[p], ones_bd) * (1.0 / RWKV_HEAD) for p in pairs]
    for p in pairs:
        sl = sls[p]
        yn = yc[p] * lax.rsqrt(var[p] + GN_EPS) * gn_w[:, sl] + gn_b[:, sl]
        bonus = _split_dot(bonus_raw[:, sl], ones_bd) * v[:, sl]
        o_ref[:, sl] = ((yn + bonus) * g[:, sl]).astype(o_ref.dtype)


def rwkv7_mix(zf, mu, w0, w_up, a0, a_up, g_up, k_k, k_a, r_k, gn_w, gn_b, B, S):
    C = RWKV_CHUNK
    nc = S // C
    W = RWKV_WIDTH
    mu_p = jnp.pad(mu, (0, ZF_SLAB_PAD - RWKV_SLAB)).reshape(1, ZF_SLAB_PAD)
    vecs = jnp.stack([w0, a0, k_k, k_a, r_k, gn_w, gn_b, jnp.zeros_like(w0)]).astype(_F32)
    wup = jnp.concatenate([w_up, jnp.zeros((LANES - DECAY_RANK, W), w_up.dtype)], axis=0).astype(_BF16)
    aup = jnp.concatenate([jnp.zeros((DECAY_RANK, W), a_up.dtype), a_up], axis=0).astype(_BF16)
    gup = jnp.concatenate([g_up, jnp.zeros((2 * LANES - GATE_RANK, W), g_up.dtype)], axis=0).astype(_BF16)
    return pl.pallas_call(
        _rwkv_kernel,
        grid=(B, nc),
        in_specs=[pl.BlockSpec((C, ZF_SLAB_PAD), lambda b, c: (b * nc + c, 0)),
                  pl.BlockSpec((1, ZF_SLAB_PAD), lambda b, c: (0, 0)),
                  pl.BlockSpec((8, W), lambda b, c: (0, 0)),
                  pl.BlockSpec((LANES, W), lambda b, c: (0, 0)),
                  pl.BlockSpec((LANES, W), lambda b, c: (0, 0)),
                  pl.BlockSpec((2 * LANES, W), lambda b, c: (0, 0))],
        out_specs=pl.BlockSpec((C, W), lambda b, c: (b * nc + c, 0)),
        out_shape=jax.ShapeDtypeStruct((B * S, W), _BF16),
        scratch_shapes=[pltpu.VMEM((C + 8, ZF_SLAB_PAD), _F32),
                        pltpu.VMEM((HEAD_PAIRS, LANES, LANES), _F32)],
        compiler_params=_params("parallel", "arbitrary"),
        name="rwkv7_mix",
    )(zf, mu_p, vecs, wup, aup, gup)


def _router_kernel(h_ref, g_ref, wr_ref, xn_ref, route_ref):
    xn = _rms_to_bf16(h_ref[...], g_ref[...])
    xn_ref[...] = xn
    logits = jnp.dot(xn, wr_ref[...], preferred_element_type=_F32)
    lane = lax.broadcasted_iota(jnp.int32, logits.shape, 1).astype(_F32)
    l1 = jnp.where(lane < N_EXPERTS, logits, -jnp.inf)
    m1 = jnp.max(l1, axis=1, keepdims=True)
    i1 = jnp.min(jnp.where(l1 == m1, lane, float(LANES)), axis=1, keepdims=True)
    l2 = jnp.where(lane == i1, -jnp.inf, l1)
    m2 = jnp.max(l2, axis=1, keepdims=True)
    i2 = jnp.min(jnp.where(l2 == m2, lane, float(LANES)), axis=1, keepdims=True)
    e2 = jnp.exp(m2 - m1)
    g1 = 1.0 / (1.0 + e2)
    g2 = e2 / (1.0 + e2)
    route_ref[...] = jnp.where(lane == 0.0, i1, jnp.where(lane == 1.0, i2,
                               jnp.where(lane == 2.0, g1, jnp.where(lane == 3.0, g2, 0.0))))


def moe_route(h, g, router, tm):
    T, D = h.shape
    wr = jnp.pad(router, ((0, 0), (0, LANES - N_EXPERTS))).astype(_BF16)
    return pl.pallas_call(
        _router_kernel,
        grid=(T // tm,),
        in_specs=[pl.BlockSpec((tm, D), lambda i: (i, 0)),
                  pl.BlockSpec((1, D), lambda i: (0, 0)),
                  pl.BlockSpec((D, LANES), lambda i: (0, 0))],
        out_specs=[pl.BlockSpec((tm, D), lambda i: (i, 0)),
                   pl.BlockSpec((tm, LANES), lambda i: (i, 0))],
        out_shape=[jax.ShapeDtypeStruct((T, D), _BF16), jax.ShapeDtypeStruct((T, LANES), _F32)],
        compiler_params=_params("parallel"),
        name="moe_route",
    )(h, g.reshape(1, D), wr)


def _moe_ffn_kernel(te_ref, nv_ref, x_ref, w1_ref, w3_ref, w2_ref, o_ref):
    n = pl.program_id(0)
    f = pl.program_id(1)

    @pl.when(jnp.logical_and(n >= nv_ref[0], f == 0))
    def _():
        o_ref[...] = jnp.zeros(o_ref.shape, o_ref.dtype)

    @pl.when(n < nv_ref[0])
    def _():
        x = x_ref[...]
        a = jnp.dot(x, w1_ref[...], preferred_element_type=_F32)
        b = jnp.dot(x, w3_ref[...], preferred_element_type=_F32)
        mid = (a * _sigmoid(a) * b).astype(_BF16)
        contrib = jnp.dot(mid, w2_ref[...], preferred_element_type=_F32)

        @pl.when(f == 0)
        def _():
            o_ref[...] = contrib

        @pl.when(f > 0)
        def _():
            o_ref[...] += contrib


def moe_grouped_ffn(xs, tile_expert, n_valid, w1, w3, w2):
    R, D = xs.shape
    F = w1.shape[2]
    nf = F // MOE_TF

    def f_idx(n, f, nv):
        return jnp.where(n < nv[0], f, nf - 1)

    return pl.pallas_call(
        _moe_ffn_kernel,
        grid_spec=pltpu.PrefetchScalarGridSpec(
            num_scalar_prefetch=2,
            grid=(R // MOE_TM, nf),
            in_specs=[pl.BlockSpec((MOE_TM, D), lambda n, f, te, nv: (n, 0)),
                      pl.BlockSpec((None, D, MOE_TF), lambda n, f, te, nv: (te[n], 0, f_idx(n, f, nv))),
                      pl.BlockSpec((None, D, MOE_TF), lambda n, f, te, nv: (te[n], 0, f_idx(n, f, nv))),
                      pl.BlockSpec((None, MOE_TF, D), lambda n, f, te, nv: (te[n], f_idx(n, f, nv), 0))],
            out_specs=pl.BlockSpec((MOE_TM, D), lambda n, f, te, nv: (n, 0))),
        out_shape=jax.ShapeDtypeStruct((R, D), _F32),
        compiler_params=_params("parallel", "arbitrary"),
        name="moe_grouped_ffn",
    )(tile_expert, n_valid, xs, w1, w3, w2)


def moe_residual(h, g, router, w1, w3, w2):
    T, D = h.shape
    E = N_EXPERTS
    xn, route = moe_route(h, g, router, tm=512)
    top = route[:, 0:TOP_K_EXPERTS].astype(jnp.int32)
    gates = route[:, TOP_K_EXPERTS:2 * TOP_K_EXPERTS]
    ef = top.T.reshape(TOP_K_EXPERTS * T)
    onehot = (ef[:, None] == jnp.arange(E, dtype=jnp.int32)[None, :]).astype(jnp.int32)
    csum = jnp.cumsum(onehot, axis=0)
    rank = jnp.take_along_axis(csum, ef[:, None], axis=1)[:, 0] - 1
    counts = csum[-1]
    padded = ((counts + MOE_TM - 1) // MOE_TM) * MOE_TM
    pends = jnp.cumsum(padded)
    pstarts = pends - padded
    starts = jnp.cumsum(counts) - counts
    pos = pstarts[ef] + rank
    R = TOP_K_EXPERTS * T + E * MOE_TM
    n_tiles = R // MOE_TM
    tile_expert = jnp.searchsorted(pends, jnp.arange(n_tiles, dtype=jnp.int32) * MOE_TM, side='right')
    tile_expert = jnp.minimum(tile_expert, E - 1).astype(jnp.int32)
    n_valid = (pends[-1:] // MOE_TM).astype(jnp.int32)
    order = jnp.argsort(ef, stable=True).astype(jnp.int32)
    rho = jnp.arange(R, dtype=jnp.int32)
    e_row = tile_expert[rho // MOE_TM]
    off = rho - pstarts[e_row]
    src = order[jnp.clip(starts[e_row] + off, 0, TOP_K_EXPERTS * T - 1)] % T
    xs = jnp.take(xn, src, axis=0)
    ys = moe_grouped_ffn(xs, tile_expert, n_valid, w1, w3, w2)
    mix = sum(gates[:, slot:slot + 1] * jnp.take(ys, pos[slot * T:(slot + 1) * T], axis=0)
              for slot in range(TOP_K_EXPERTS))
    return h + mix


def _pack_w_in(w):
    wq, wk, wv, wqi, wki, wwi, wslab, wga, wgb = _split(w, IN_SIZES)
    D = w.shape[0]
    wb = jnp.concatenate([wq * (ATT_HEAD_DIM ** -0.5 * LOG2_E), wk, wv, wqi, wki, wki], axis=1)
    wf = jnp.concatenate([wslab, jnp.zeros((D, ZF_SLAB_PAD - RWKV_SLAB), w.dtype),
                          wki, wwi, jnp.zeros((D, LANES - IDX_DIM - IDX_HEADS), w.dtype), wga, wgb], axis=1)
    return wb.astype(_BF16), wf.astype(_BF16)


def kernel(x, p, w_in, att_up, rwkv_up, w_out, rel_bias, rwkv_mu, rwkv_w0, rwkv_w_up,
           rwkv_a0, rwkv_a_up, rwkv_g_up, rwkv_k_k, rwkv_k_a, rwkv_r_k, rwkv_gn_w, rwkv_gn_b,
           norm_mix, norm_ffn, norm_ple, ple_proj, ple_gate, ffn_w1, ffn_w3, ffn_w2,
           moe_router, moe_w1, moe_w3, moe_w2, final_norm):
    B, S, D = x.shape
    T = B * S
    depth = w_in.shape[0]
    h = x.reshape(T, D)
    for i in range(depth):
        wb, wf = _pack_w_in(w_in[i])
        zb = norm_matmul(h, norm_mix[i], wb, _BF16, tm=512, tn=ZB_WIDTH)
        zf = norm_matmul(h, norm_mix[i], wf, _F32, tm=512, tn=1536)
        att = dsa_attention(zb, zf, rel_bias, B, S)
        rw = rwkv7_mix(zf, rwkv_mu[i], rwkv_w0[i], rwkv_w_up[i], rwkv_a0[i], rwkv_a_up[i], rwkv_g_up[i],
                       rwkv_k_k[i], rwkv_k_a[i], rwkv_r_k[i], rwkv_gn_w[i], rwkv_gn_b[i], B, S)
        merged = gated_merge(att, rw, zf, att_up[i].astype(_BF16), rwkv_up[i].astype(_BF16), tm=512, tn=512)
        h = matmul_residual(merged, w_out[i].astype(_BF16), h, tm=512, tn=1024)
        if i % 2 == 0:
            j = i // 2
            h = ffn_residual(h, norm_ffn[i], ffn_w1[j].astype(_BF16), ffn_w3[j].astype(_BF16),
                             ffn_w2[j].astype(_BF16), tm=512, tf=512)
        else:
            j = i // 2
            h = moe_residual(h, norm_ffn[i], moe_router[j], moe_w1[j].astype(_BF16), moe_w3[j].astype(_BF16),
                             moe_w2[j].astype(_BF16))
        h = ple_residual(h, norm_ple[i], p[i].reshape(T, PLE_DIM), ple_gate[i].astype(_BF16),
                         ple_proj[i].astype(_BF16), final_norm, final=(i == depth - 1), tm=256)
    return h.reshape(B, S, D)
```

```python
import functools
import math

import jax
import jax.numpy as jnp
import numpy as np
from jax import lax
from jax.experimental import pallas as pl
from jax.experimental.pallas import tpu as pltpu

D_MODEL = 2048
DEPTH = 2
ATT_HEADS = 8
ATT_KV_HEADS = 2
ATT_HEAD_DIM = 128
IDX_HEADS = 8
IDX_DIM = 64
TOPK_MAX = 256
REL_BUCKETS = 32
REL_MAX_DIST = 128
RWKV_HEAD = 64
RWKV_HEADS = 16
DECAY_RANK = 64
ICLR_RANK = 64
GATE_RANK = 160
D_FF = 5632
N_EXPERTS = 8
TOP_K_EXPERTS = 2
D_FF_EXPERT = 7168
PLE_DIM = 256
RMS_EPS = 1e-6
GN_EPS = 64e-5

ATT_WIDTH = ATT_HEADS * ATT_HEAD_DIM
KV_WIDTH = ATT_KV_HEADS * ATT_HEAD_DIM
RWKV_WIDTH = RWKV_HEADS * RWKV_HEAD
RWKV_SIZES = (RWKV_WIDTH, RWKV_WIDTH, RWKV_WIDTH, DECAY_RANK, ICLR_RANK, GATE_RANK)
RWKV_SLAB = 3 * RWKV_WIDTH + DECAY_RANK + ICLR_RANK + GATE_RANK
IN_SIZES = (ATT_WIDTH, KV_WIDTH, KV_WIDTH, IDX_HEADS * IDX_DIM, IDX_DIM, IDX_HEADS, RWKV_SLAB, D_MODEL, D_MODEL)

LANES = 128
VMEM_LIMIT = 56 * 1024 * 1024

Q_BLOCK = 128
KEY_CHUNK = 512
BLOCKS_PER_CHUNK = KEY_CHUNK // Q_BLOCK
BLOCKS_PER_GROUP = 32
MASKED_LOGIT = -1e30
LOG2_E = math.log2(math.e)
GQA_GROUP = ATT_HEADS // ATT_KV_HEADS

ZB_WIDTH = ATT_WIDTH + 2 * KV_WIDTH + IDX_HEADS * IDX_DIM + 2 * IDX_DIM
ZF_SLAB_PAD = 3456
ZF_WIDX = ZF_SLAB_PAD
ZF_GATE_A = ZF_WIDX + LANES
ZF_GATE_B = ZF_GATE_A + D_MODEL
ZF_WIDTH = ZF_GATE_B + D_MODEL

RWKV_CHUNK = 64
HEAD_PAIRS = RWKV_HEADS // 2
MOE_TM = 512
MOE_TF = 1024

_F32 = jnp.float32
_BF16 = jnp.bfloat16
_NT = (((1,), (1,)), ((), ()))
_TN = (((0,), (0,)), ((), ()))


def _split(z, sizes):
    return jnp.split(z, np.cumsum(sizes)[:-1].tolist(), axis=-1)


def _sigmoid(x):
    return 1.0 / (1.0 + jnp.exp(-x))


def _rms_to_bf16(x, g):
    y = x * lax.rsqrt(jnp.mean(x * x, axis=-1, keepdims=True) + RMS_EPS)
    return (y * g).astype(_BF16)


def _params(*sem):
    return pltpu.CompilerParams(dimension_semantics=sem, vmem_limit_bytes=VMEM_LIMIT)


def _norm_mm_kernel(x_ref, g_ref, w_ref, o_ref, xn_ref):
    @pl.when(pl.program_id(1) == 0)
    def _():
        xn_ref[...] = _rms_to_bf16(x_ref[...], g_ref[...])

    o_ref[...] = jnp.dot(xn_ref[...], w_ref[...], preferred_element_type=_F32).astype(o_ref.dtype)


def norm_matmul(x, g, w, out_dtype, tm, tn):
    T, D = x.shape
    N = w.shape[1]
    return pl.pallas_call(
        _norm_mm_kernel,
        grid=(T // tm, N // tn),
        in_specs=[pl.BlockSpec((tm, D), lambda i, j: (i, 0)),
                  pl.BlockSpec((1, D), lambda i, j: (0, 0)),
                  pl.BlockSpec((D, tn), lambda i, j: (0, j))],
        out_specs=pl.BlockSpec((tm, tn), lambda i, j: (i, j)),
        out_shape=jax.ShapeDtypeStruct((T, N), out_dtype),
        scratch_shapes=[pltpu.VMEM((tm, D), _BF16)],
        compiler_params=_params("parallel", "arbitrary"),
        name="norm_matmul",
    )(x, g.reshape(1, D), w)


def _merge_kernel(att_ref, rw_ref, ga_ref, gb_ref, wa_ref, wr_ref, o_ref):
    a = jnp.dot(att_ref[...], wa_ref[...], preferred_element_type=_F32)
    r = jnp.dot(rw_ref[...], wr_ref[...], preferred_element_type=_F32)
    o_ref[...] = (_sigmoid(ga_ref[...]) * a + _sigmoid(gb_ref[...]) * r).astype(o_ref.dtype)


def gated_merge(att, rw, zf, wa, wr, tm, tn):
    T, K = att.shape
    N = wa.shape[1]
    ja, jb = ZF_GATE_A // tn, ZF_GATE_B // tn
    return pl.pallas_call(
        _merge_kernel,
        grid=(T // tm, N // tn),
        in_specs=[pl.BlockSpec((tm, K), lambda i, j: (i, 0)),
                  pl.BlockSpec((tm, K), lambda i, j: (i, 0)),
                  pl.BlockSpec((tm, tn), lambda i, j: (i, ja + j)),
                  pl.BlockSpec((tm, tn), lambda i, j: (i, jb + j)),
                  pl.BlockSpec((K, tn), lambda i, j: (0, j)),
                  pl.BlockSpec((K, tn), lambda i, j: (0, j))],
        out_specs=pl.BlockSpec((tm, tn), lambda i, j: (i, j)),
        out_shape=jax.ShapeDtypeStruct((T, N), _BF16),
        compiler_params=_params("parallel", "arbitrary"),
        name="gated_merge",
    )(att, rw, zf, zf, wa, wr)


def _mm_res_kernel(x_ref, w_ref, h_ref, o_ref):
    o_ref[...] = h_ref[...] + jnp.dot(x_ref[...], w_ref[...], preferred_element_type=_F32)


def matmul_residual(x, w, h, tm, tn):
    T, K = x.shape
    N = w.shape[1]
    return pl.pallas_call(
        _mm_res_kernel,
        grid=(T // tm, N // tn),
        in_specs=[pl.BlockSpec((tm, K), lambda i, j: (i, 0)),
                  pl.BlockSpec((K, tn), lambda i, j: (0, j)),
                  pl.BlockSpec((tm, tn), lambda i, j: (i, j))],
        out_specs=pl.BlockSpec((tm, tn), lambda i, j: (i, j)),
        out_shape=jax.ShapeDtypeStruct((T, N), _F32),
        compiler_params=_params("parallel", "arbitrary"),
        name="matmul_residual",
    )(x, w, h)


def _ffn_kernel(h_ref, g_ref, w1_ref, w3_ref, w2_ref, o_ref, xn_ref):
    f = pl.program_id(1)

    @pl.when(f == 0)
    def _():
        xn_ref[...] = _rms_to_bf16(h_ref[...], g_ref[...])
        o_ref[...] = h_ref[...]

    xn = xn_ref[...]
    a = jnp.dot(xn, w1_ref[...], preferred_element_type=_F32)
    b = jnp.dot(xn, w3_ref[...], preferred_element_type=_F32)
    mid = (a * _sigmoid(a) * b).astype(_BF16)
    o_ref[...] += jnp.dot(mid, w2_ref[...], preferred_element_type=_F32)


def ffn_residual(h, g, w1, w3, w2, tm, tf):
    T, D = h.shape
    F = w1.shape[1]
    return pl.pallas_call(
        _ffn_kernel,
        grid=(T // tm, F // tf),
        in_specs=[pl.BlockSpec((tm, D), lambda i, f: (i, 0)),
                  pl.BlockSpec((1, D), lambda i, f: (0, 0)),
                  pl.BlockSpec((D, tf), lambda i, f: (0, f)),
                  pl.BlockSpec((D, tf), lambda i, f: (0, f)),
                  pl.BlockSpec((tf, D), lambda i, f: (f, 0))],
        out_specs=pl.BlockSpec((tm, D), lambda i, f: (i, 0)),
        out_shape=jax.ShapeDtypeStruct((T, D), _F32),
        scratch_shapes=[pltpu.VMEM((tm, D), _BF16)],
        compiler_params=_params("parallel", "arbitrary"),
        name="ffn_residual",
    )(h, g.reshape(1, D), w1, w3, w2)


def _ple_kernel(h_ref, g_ref, p_ref, wg_ref, wp_ref, fg_ref, o_ref, *, final):
    h = h_ref[...]
    xn = _rms_to_bf16(h, g_ref[...])
    gate = _sigmoid(jnp.dot(xn, wg_ref[...], preferred_element_type=_F32))
    pp = jnp.dot(p_ref[...].astype(_BF16), wp_ref[...], preferred_element_type=_F32)
    out = h + gate * pp
    if final:
        out = out * lax.rsqrt(jnp.mean(out * out, axis=-1, keepdims=True) + RMS_EPS) * fg_ref[...]
    o_ref[...] = out


def ple_residual(h, g, p, wg, wp, final_g, final, tm):
    T, D = h.shape
    P = p.shape[1]
    return pl.pallas_call(
        functools.partial(_ple_kernel, final=final),
        grid=(T // tm,),
        in_specs=[pl.BlockSpec((tm, D), lambda i: (i, 0)),
                  pl.BlockSpec((1, D), lambda i: (0, 0)),
                  pl.BlockSpec((tm, P), lambda i: (i, 0)),
                  pl.BlockSpec((D, D), lambda i: (0, 0)),
                  pl.BlockSpec((P, D), lambda i: (0, 0)),
                  pl.BlockSpec((1, D), lambda i: (0, 0))],
        out_specs=pl.BlockSpec((tm, D), lambda i: (i, 0)),
        out_shape=jax.ShapeDtypeStruct((T, D), _F32),
        compiler_params=_params("parallel"),
        name="ple_residual",
    )(h, g.reshape(1, D), p, wg, wp, final_g.reshape(1, D))


def rel_bucket(dist):
    max_exact = REL_BUCKETS // 2
    dist = jnp.maximum(dist, 0)
    d_f = jnp.maximum(dist, 1).astype(_F32)
    large = max_exact + (jnp.log(d_f / max_exact) / math.log(REL_MAX_DIST / max_exact)
                         * (REL_BUCKETS - max_exact)).astype(jnp.int32)
    large = jnp.minimum(large, REL_BUCKETS - 1)
    return jnp.where(dist < max_exact, dist, large)


def _sortable(x):
    b = lax.bitcast_convert_type(x, jnp.int32)
    return b ^ ((b >> 31) & jnp.int32(0x7FFFFFFF))


_INT_MIN = -2 ** 31
_KEY_NEG_INF = 0x807FFFFF - 2 ** 32


def _dsa_kernel(q_ref, qi_ref, wi_ref, k_ref, v_ref, ki_ref, bprev_ref, bdiag_ref, bfar_ref, o_ref,
                keys_ref, planes_ref, active_ref, qs_ref, qim_ref, m_ref, acc_ref, *, n_sel, idx_bits):
    n_grp = planes_ref.shape[0]
    qb = pl.program_id(1)
    t0 = qb * Q_BLOCK
    n_chunks = qb // BLOCKS_PER_CHUNK + 1
    t_idx = t0 + lax.broadcasted_iota(jnp.int32, (Q_BLOCK, 1), 0)
    lane_blk = lax.broadcasted_iota(jnp.int32, (1, Q_BLOCK), 1)
    lane_chunk = lax.broadcasted_iota(jnp.int32, (1, KEY_CHUNK), 1)

    for hh in range(ATT_HEADS):
        g, h = divmod(hh, GQA_GROUP)
        qs_ref[g, h * Q_BLOCK:(h + 1) * Q_BLOCK, :] = q_ref[:, hh * ATT_HEAD_DIM:(hh + 1) * ATT_HEAD_DIM]
    low = lax.broadcasted_iota(jnp.int32, (Q_BLOCK, LANES), 1) < IDX_DIM
    for hh in range(IDX_HEADS):
        tile = qi_ref[:, (hh // 2) * LANES:(hh // 2 + 1) * LANES]
        keep = low if hh % 2 == 0 else jnp.logical_not(low)
        qim_ref[hh] = jnp.where(keep, tile, jnp.zeros_like(tile))
    m_ref[...] = jnp.full(m_ref.shape, MASKED_LOGIT, _F32)

    @pl.when(qb == 0)
    def _():
        keys_ref[BLOCKS_PER_CHUNK:] = jnp.zeros((keys_ref.shape[0] - BLOCKS_PER_CHUNK, Q_BLOCK, Q_BLOCK), jnp.int32)
        planes_ref[...] = jnp.zeros(planes_ref.shape, jnp.int32)
    acc_ref[...] = jnp.zeros(acc_ref.shape, _F32)

    w_all = wi_ref[...]
    idx_scale = (IDX_HEADS ** -0.5) * (IDX_DIM ** -0.5)

    def score_body(c, carry):
        ks = pl.multiple_of(c * KEY_CHUNK, KEY_CHUNK)
        kic = ki_ref[pl.ds(ks, KEY_CHUNK), :]
        d = lax.dot_general(qim_ref[...].reshape(IDX_HEADS * Q_BLOCK, LANES), kic, _NT,
                            preferred_element_type=_F32)
        sc = jnp.zeros((Q_BLOCK, KEY_CHUNK), _F32)
        for hh in range(IDX_HEADS):
            sc = sc + w_all[:, IDX_DIM + hh:IDX_DIM + hh + 1] * jnp.maximum(d[hh * Q_BLOCK:(hh + 1) * Q_BLOCK], 0.0)
        sc = sc * idx_scale
        sc = jnp.where(sc == 0.0, 0.0, sc)
        key = jnp.where(ks + lane_chunk <= t_idx, _sortable(sc), _KEY_NEG_INF)
        for j in range(BLOCKS_PER_CHUNK):
            keys_ref[c * BLOCKS_PER_CHUNK + j] = key[:, j * Q_BLOCK:(j + 1) * Q_BLOCK]
        return carry

    lax.fori_loop(0, n_chunks, score_body, 0)

    def count(pred):
        def body(c, acc):
            for j in range(BLOCKS_PER_CHUNK):
                blk = c * BLOCKS_PER_CHUNK + j
                acc = acc + jnp.where(pred(keys_ref[blk], blk * Q_BLOCK + lane_blk), 1.0, 0.0)
            return acc

        acc = lax.fori_loop(0, n_chunks, body, jnp.zeros((Q_BLOCK, Q_BLOCK), _F32))
        return jnp.sum(acc, axis=1, keepdims=True)

    n_blocks = n_chunks * BLOCKS_PER_CHUNK
    n_groups = (n_blocks + BLOCKS_PER_GROUP - 1) // BLOCKS_PER_GROUP

    def transpose_group(g, carry):
        def rows(rg, carry_rows):
            r0 = pl.multiple_of(rg * 8, 8)
            x = [keys_ref[g * BLOCKS_PER_GROUP + j, pl.ds(r0, 8), :] for j in range(BLOCKS_PER_GROUP)]
            j, m = 16, 0x0000FFFF
            while j:
                k = 0
                while k < BLOCKS_PER_GROUP:
                    t = (x[k] ^ lax.shift_right_logical(x[k + j], jnp.int32(j))) & jnp.int32(m)
                    x[k] = x[k] ^ t
                    x[k + j] = x[k + j] ^ lax.shift_left(t, jnp.int32(j))
                    k = (k + j + 1) & ~j
                j >>= 1
                m ^= m << j
            planes_ref[g, 0, pl.ds(r0, 8), :] = ~x[0]
            for i in range(1, 32):
                planes_ref[g, i, pl.ds(r0, 8), :] = x[i]
            return carry_rows

        lax.fori_loop(0, Q_BLOCK // 8, rows, 0)
        n_live = jnp.minimum(n_blocks - g * BLOCKS_PER_GROUP, BLOCKS_PER_GROUP)
        active_ref[g] = lax.shift_left(jnp.full((Q_BLOCK, Q_BLOCK), -1, jnp.int32), BLOCKS_PER_GROUP - n_live)
        return carry

    active_ref[...] = jnp.zeros(active_ref.shape, jnp.int32)
    lax.fori_loop(0, n_groups, transpose_group, 0)

    def lane_total(words):
        return jnp.sum(sum(lax.population_count(w) for w in words).astype(_F32), axis=1, keepdims=True)

    def split_active(g, i):
        act = active_ref[g]
        hi = act & planes_ref[g, 2 * i]
        lo = act ^ hi
        hi1 = hi & planes_ref[g, 2 * i + 1]
        lo1 = lo & planes_ref[g, 2 * i + 1]
        return hi1, hi ^ hi1, lo1, lo ^ lo1

    def radix_pass(i, carry):
        prefix, above = carry
        parts = [split_active(g, i) for g in range(n_grp)]
        c11, c10, c01 = (lane_total([parts[g][q] for g in range(n_grp)]) for q in range(3))
        upto10 = above + c11 + c10
        t11 = above + c11 >= n_sel
        t10 = upto10 >= n_sel
        t01 = upto10 + c01 >= n_sel
        for g in range(n_grp):
            s11, s10, s01, s00 = split_active(g, i)
            active_ref[g] = jnp.where(t11, s11, jnp.where(t10, s10, jnp.where(t01, s01, s00)))
        bit1 = jnp.where(t10, lax.shift_left(jnp.int32(1), 31 - 2 * i), 0)
        bit0 = jnp.where(jnp.logical_or(t11, jnp.logical_and(jnp.logical_not(t10), t01)),
                         lax.shift_left(jnp.int32(1), 30 - 2 * i), 0)
        above = jnp.where(t11, above, jnp.where(t10, above + c11, jnp.where(t01, upto10, upto10 + c01)))
        return prefix | bit1 | bit0, above

    prefix, n_gt = lax.fori_loop(0, 16, radix_pass, (jnp.zeros((Q_BLOCK, 1), jnp.int32),
                                                     jnp.zeros((Q_BLOCK, 1), _F32)))
    tau = prefix ^ jnp.int32(_INT_MIN)
    n_eq = lane_total([active_ref[g] for g in range(n_grp)])
    tied = jnp.logical_and(n_gt + n_eq > n_sel, tau > _KEY_NEG_INF)

    def tie_search():
        need = n_sel - count(lambda key, sidx: key > tau)

        def index_bit(i, x):
            cand = x | lax.shift_left(jnp.int32(1), idx_bits - 1 - i)
            cnt = count(lambda key, sidx: jnp.logical_and(key == tau, sidx < cand))
            return jnp.where(cnt < need, cand, x)

        return lax.fori_loop(0, idx_bits, index_bit, jnp.zeros((Q_BLOCK, 1), jnp.int32))

    last_tie = lax.cond(jnp.max(jnp.where(tied, 1.0, 0.0)) > 0.0, tie_search,
                        lambda: jnp.full((Q_BLOCK, 1), 2 ** idx_bits, jnp.int32))

    def attend(key, k_tile, v_tile, s0, lane, bias_ref, enable=None, const_bias=False):
        width = key.shape[1]
        sidx = s0 + lane
        sel = jnp.logical_or(key > tau, jnp.logical_and(key == tau, sidx <= last_tie))
        sel = jnp.logical_and(sel, key > _KEY_NEG_INF)
        if enable is not None:
            sel = jnp.logical_and(sel, jnp.zeros_like(sidx) + enable > 0)
        ones = jnp.ones((width, ATT_HEAD_DIM), _BF16)
        groups = range(ATT_KV_HEADS)
        cols = [slice(g * ATT_HEAD_DIM, (g + 1) * ATT_HEAD_DIM) for g in groups]

        def logits(g):
            return lax.dot_general(qs_ref[g], k_tile[:, cols[g]], _NT,
                                   preferred_element_type=_F32).reshape(GQA_GROUP, Q_BLOCK, width)

        def softmax(g, s):
            bias = bias_ref[g * GQA_GROUP:(g + 1) * GQA_GROUP]
            m_old = m_ref[g]
            if const_bias:
                sh = jnp.where(sel[None], s, MASKED_LOGIT)
                m_new = jnp.maximum(m_old, jnp.max(sh, axis=2, keepdims=True) + bias)
                shift = m_new - bias
            else:
                sh = jnp.where(sel[None], s + bias, MASKED_LOGIT)
                m_new = jnp.maximum(m_old, jnp.max(sh, axis=2, keepdims=True))
                shift = m_new
            m_ref[g] = m_new
            return (jnp.exp2(sh - shift).reshape(GQA_GROUP * Q_BLOCK, width).astype(_BF16),
                    jnp.exp2(m_old - m_new).reshape(GQA_GROUP * Q_BLOCK, 1))

        def accumulate(g, p, alpha):
            pv = jnp.dot(p, jnp.concatenate([v_tile[:, cols[g]], ones], axis=1), preferred_element_type=_F32)
            acc_ref[g] = acc_ref[g] * alpha + pv

        if width <= Q_BLOCK:
            s = [logits(g) for g in groups]
            pa = [softmax(g, s[g]) for g in groups]
            for g in groups:
                accumulate(g, *pa[g])
        else:
            for g in groups:
                accumulate(g, *softmax(g, logits(g)))

    n_far = jnp.maximum(qb - 1, 0) // BLOCKS_PER_CHUNK

    def far_body(c, carry):
        ks = pl.multiple_of(c * KEY_CHUNK, KEY_CHUNK)
        key = jnp.concatenate([keys_ref[c * BLOCKS_PER_CHUNK + j] for j in range(BLOCKS_PER_CHUNK)], axis=1)
        attend(key, k_ref[pl.ds(ks, KEY_CHUNK), :], v_ref[pl.ds(ks, KEY_CHUNK), :], ks, lane_chunk, bfar_ref,
               const_bias=True)
        return carry

    lax.fori_loop(0, n_far, far_body, 0)

    def block_step(j, bias_ref, enable=None, const_bias=False):
        ks = pl.multiple_of(j * Q_BLOCK, Q_BLOCK)
        attend(keys_ref[j], k_ref[pl.ds(ks, Q_BLOCK), :], v_ref[pl.ds(ks, Q_BLOCK), :], ks, lane_blk, bias_ref,
               enable, const_bias)

    def tail_body(j, carry):
        block_step(j, bfar_ref, const_bias=True)
        return carry

    lax.fori_loop(n_far * BLOCKS_PER_CHUNK, qb - 1, tail_body, 0)
    block_step(jnp.maximum(qb - 1, 0), bprev_ref, enable=jnp.minimum(qb, 1))
    block_step(qb, bdiag_ref)

    for hh in range(ATT_HEADS):
        g, h = divmod(hh, GQA_GROUP)
        rows = slice(h * Q_BLOCK, (h + 1) * Q_BLOCK)
        o_ref[:, hh * ATT_HEAD_DIM:(hh + 1) * ATT_HEAD_DIM] = (
            acc_ref[g, rows, 0:ATT_HEAD_DIM] / acc_ref[g, rows, ATT_HEAD_DIM:2 * ATT_HEAD_DIM]).astype(o_ref.dtype)


def dsa_attention(zb, zf, rel_bias, B, S):
    nq = S // Q_BLOCK
    n_grp = -(-nq // BLOCKS_PER_GROUP)
    n_sel = min(TOPK_MAX, S // 4)
    tl = jnp.arange(Q_BLOCK, dtype=jnp.int32)[:, None]
    sr = jnp.arange(Q_BLOCK, dtype=jnp.int32)[None, :]
    rel_bias = rel_bias.astype(_F32) * LOG2_E
    bprev = rel_bias[rel_bucket(tl + Q_BLOCK - sr)].transpose(2, 0, 1)
    bdiag = rel_bias[rel_bucket(tl - sr)].transpose(2, 0, 1)
    bfar = rel_bias[rel_bucket(jnp.int32(REL_MAX_DIST + 1))].reshape(ATT_HEADS, 1, 1)
    kernel = functools.partial(_dsa_kernel, n_sel=float(n_sel), idx_bits=int(S - 1).bit_length())
    c_k = ATT_WIDTH // KV_WIDTH
    c_qi = (ATT_WIDTH + 2 * KV_WIDTH) // (IDX_HEADS * IDX_DIM)
    c_ki = (ATT_WIDTH + 2 * KV_WIDTH + IDX_HEADS * IDX_DIM) // LANES
    return pl.pallas_call(
        kernel,
        grid=(B, nq),
        in_specs=[pl.BlockSpec((Q_BLOCK, ATT_WIDTH), lambda b, i: (b * nq + i, 0)),
                  pl.BlockSpec((Q_BLOCK, IDX_HEADS * IDX_DIM), lambda b, i: (b * nq + i, c_qi)),
                  pl.BlockSpec((Q_BLOCK, LANES), lambda b, i: (b * nq + i, ZF_WIDX // LANES)),
                  pl.BlockSpec((S, KV_WIDTH), lambda b, i: (b, c_k)),
                  pl.BlockSpec((S, KV_WIDTH), lambda b, i: (b, c_k + 1)),
                  pl.BlockSpec((S, LANES), lambda b, i: (b, c_ki)),
                  pl.BlockSpec((ATT_HEADS, Q_BLOCK, Q_BLOCK), lambda b, i: (0, 0, 0)),
                  pl.BlockSpec((ATT_HEADS, Q_BLOCK, Q_BLOCK), lambda b, i: (0, 0, 0)),
                  pl.BlockSpec((ATT_HEADS, 1, 1), lambda b, i: (0, 0, 0))],
        out_specs=pl.BlockSpec((Q_BLOCK, ATT_WIDTH), lambda b, i: (b * nq + i, 0)),
        out_shape=jax.ShapeDtypeStruct((B * S, ATT_WIDTH), _BF16),
        scratch_shapes=[pltpu.VMEM((n_grp * BLOCKS_PER_GROUP, Q_BLOCK, Q_BLOCK), jnp.int32),
                        pltpu.VMEM((n_grp, 32, Q_BLOCK, Q_BLOCK), jnp.int32),
                        pltpu.VMEM((n_grp, Q_BLOCK, Q_BLOCK), jnp.int32),
                        pltpu.VMEM((ATT_KV_HEADS, GQA_GROUP * Q_BLOCK, ATT_HEAD_DIM), _BF16),
                        pltpu.VMEM((IDX_HEADS, Q_BLOCK, LANES), _BF16),
                        pltpu.VMEM((ATT_KV_HEADS, GQA_GROUP, Q_BLOCK, 1), _F32),
                        pltpu.VMEM((ATT_KV_HEADS, GQA_GROUP * Q_BLOCK, 2 * ATT_HEAD_DIM), _F32)],
        compiler_params=_params("parallel", "arbitrary"),
        name="dsa_attention",
    )(zb, zb, zf, zb, zb, zb, bprev, bdiag, bfar)


def _split_dot(x, w):
    hi = x.astype(_BF16)
    lo = (x - hi.astype(_F32)).astype(_BF16)
    return jnp.dot(hi, w, preferred_element_type=_F32) + jnp.dot(lo, w, preferred_element_type=_F32)


def _split_dot_left(w, x):
    hi = x.astype(_BF16)
    lo = (x - hi.astype(_F32)).astype(_BF16)
    return jnp.dot(w, hi, preferred_element_type=_F32) + jnp.dot(w, lo, preferred_element_type=_F32)


def _rwkv_kernel(slab_ref, mu_ref, vec_ref, wup_ref, aup_ref, gup_ref, o_ref, xs_ref, state_ref):
    C = RWKV_CHUNK
    W = RWKV_WIDTH
    c = pl.program_id(1)

    @pl.when(c == 0)
    def _():
        xs_ref[0:8, :] = jnp.zeros((8, ZF_SLAB_PAD), _F32)
        state_ref[...] = jnp.zeros(state_ref.shape, _F32)

    slab = slab_ref[...]
    xs_ref[8:8 + C, :] = slab
    shifted = xs_ref[7:7 + C, :]
    xs = slab + (shifted - slab) * mu_ref[...]
    xs_ref[0:8, :] = slab[C - 8:C, :]

    r = xs[:, 0:W]
    k = xs[:, W:2 * W]
    v = xs[:, 2 * W:3 * W]
    lora = xs[:, 3 * W:3 * W + LANES]
    xg = xs[:, 3 * W + LANES:3 * W + 3 * LANES]
    w0, a0, k_k, k_a, r_k, gn_w, gn_b = (vec_ref[i:i + 1, :] for i in range(7))
    d = w0 + jnp.dot(jnp.tanh(lora).astype(_BF16), wup_ref[...], preferred_element_type=_F32)
    logw = -math.exp(-0.5) * _sigmoid(d)
    a = _sigmoid(a0 + jnp.dot(lora.astype(_BF16), aup_ref[...], preferred_element_type=_F32))
    g = jnp.dot(_sigmoid(xg).astype(_BF16), gup_ref[...], preferred_element_type=_F32)
    kk_raw = k * k_k
    kt = k * (1.0 + (a - 1.0) * k_a)
    bonus_raw = r * kt * r_k

    ti = lax.broadcasted_iota(jnp.int32, (C, C), 0)
    tj = lax.broadcasted_iota(jnp.int32, (C, C), 1)
    cum = _split_dot_left(jnp.where(tj <= ti, 1.0, 0.0).astype(_BF16), logw)
    cum_end = cum[C - 1:C, :]
    g_incl = jnp.exp(cum)
    g_excl = jnp.exp(cum - logw)
    g_inv = jnp.exp(-cum)
    g_end = jnp.exp(cum_end - cum)
    g_total = jnp.exp(cum_end)

    lo = lax.broadcasted_iota(jnp.int32, (C, LANES), 1) < RWKV_HEAD
    ri = lax.broadcasted_iota(jnp.int32, (LANES, LANES), 0)
    ci = lax.broadcasted_iota(jnp.int32, (LANES, LANES), 1)
    same_head = (ri // RWKV_HEAD) == (ci // RWKV_HEAD)
    ones_bd = jnp.where(same_head, 1.0, 0.0).astype(_BF16)
    eye = ri == ci
    same_blk = (ri // C) == (ci // C)
    strict = jnp.logical_and(same_blk, ci < ri)
    incl = jnp.logical_and(same_blk, ci <= ri)

    def stack(x):
        return jnp.concatenate([jnp.where(lo, x, 0.0), jnp.where(lo, 0.0, x)], axis=0)

    def dot(x, y):
        return jnp.dot(x.astype(_BF16), y.astype(_BF16), preferred_element_type=_F32)

    pairs = range(HEAD_PAIRS)
    sls = [slice(p * LANES, (p + 1) * LANES) for p in pairs]
    h_bd = [state_ref[p] for p in pairs]
    kk = [kk_raw[:, sl] * lax.rsqrt(jnp.maximum(_split_dot(kk_raw[:, sl] * kk_raw[:, sl], ones_bd), 1e-24))
          for sl in sls]
    b = [kk[p] * a[:, sls[p]] for p in pairs]
    lhs4 = [jnp.concatenate([stack(kk[p] * g_excl[:, sls[p]]), stack(r[:, sls[p]] * g_incl[:, sls[p]])],
                            axis=0).astype(_BF16) for p in pairs]
    rhs4 = []
    for p in pairs:
        bi = b[p] * g_inv[:, sls[p]]
        ki = kt[:, sls[p]] * g_inv[:, sls[p]]
        rhs4.append(jnp.concatenate([bi, bi, ki, ki], axis=0).astype(_BF16))
    gm = [lax.dot_general(lhs4[p], rhs4[p], _NT, preferred_element_type=_F32) for p in pairs]
    a_bd = [jnp.where(strict, gm[p][:2 * C, :2 * C], 0.0) for p in pairs]
    bp_bd = [jnp.concatenate([jnp.where(strict, gm[p][:2 * C, 2 * C:], 0.0),
                              jnp.where(incl, gm[p][2 * C:, 2 * C:], 0.0)], axis=0) for p in pairs]
    pb_bd = [jnp.where(incl, gm[p][2 * C:, :2 * C], 0.0) for p in pairs]

    inv = [jnp.where(eye, 1.0, 0.0) - jnp.where((ri // 2) == (ci // 2), a_bd[p], 0.0) for p in pairs]
    s = 2
    while s < C:
        off = jnp.logical_and((ri // (2 * s)) == (ci // (2 * s)), (ri // s) != (ci // s))
        tm = [dot(inv[p], jnp.where(off, a_bd[p], 0.0)) for p in pairs]
        inv = [inv[p] - dot(tm[p], inv[p]) for p in pairs]
        s *= 2

    v_st = [stack(v[:, sl]) for sl in sls]
    hw = [dot(lhs4[p], h_bd[p]) for p in pairs]
    bv = [dot(bp_bd[p], v_st[p]) for p in pairs]
    u_st = [dot(inv[p], hw[p][:2 * C] + bv[p][:2 * C]) for p in pairs]
    y_st = [hw[p][2 * C:] + bv[p][2 * C:] - dot(pb_bd[p], u_st[p]) for p in pairs]
    y = [y_st[p][:C] + y_st[p][C:] for p in pairs]
    u = [u_st[p][:C] + u_st[p][C:] for p in pairs]

    for p in pairs:
        sl = sls[p]
        m_t = jnp.concatenate([kt[:, sl] * g_end[:, sl], -(b[p] * g_end[:, sl])], axis=0).astype(_BF16)
        n_t = jnp.concatenate([v[:, sl], u[p]], axis=0).astype(_BF16)
        dh = lax.dot_general(m_t, n_t, _TN, preferred_element_type=_F32)
        gt_col = jnp.sum(jnp.where(eye, jnp.broadcast_to(g_total[:, sl], (LANES, LANES)), 0.0),
                         axis=1, keepdims=True)
        state_ref[p] = jnp.where(same_head, gt_col * h_bd[p] + dh, 0.0)

    mean = [_split_dot(y[p], ones_bd) * (1.0 / RWKV_HEAD) for p in pairs]
    yc = [y[p] - mean[p] for p in pairs]
    var = [_split_dot(yc[p] * yc[p], ones_bd) * (1.0 / RWKV_HEAD) for p in pairs]
    for p in pairs:
        sl = sls[p]
        yn = yc[p] * lax.rsqrt(var[p] + GN_EPS) * gn_w[:, sl] + gn_b[:, sl]
        bonus = _split_dot(bonus_raw[:, sl], ones_bd) * v[:, sl]
        o_ref[:, sl] = ((yn + bonus) * g[:, sl]).astype(o_ref.dtype)


def rwkv7_mix(zf, mu, w0, w_up, a0, a_up, g_up, k_k, k_a, r_k, gn_w, gn_b, B, S):
    C = RWKV_CHUNK
    nc = S // C
    W = RWKV_WIDTH
    mu_p = jnp.pad(mu, (0, ZF_SLAB_PAD - RWKV_SLAB)).reshape(1, ZF_SLAB_PAD)
    vecs = jnp.stack([w0, a0, k_k, k_a, r_k, gn_w, gn_b, jnp.zeros_like(w0)]).astype(_F32)
    wup = jnp.concatenate([w_up, jnp.zeros((LANES - DECAY_RANK, W), w_up.dtype)], axis=0).astype(_BF16)
    aup = jnp.concatenate([jnp.zeros((DECAY_RANK, W), a_up.dtype), a_up], axis=0).astype(_BF16)
    gup = jnp.concatenate([g_up, jnp.zeros((2 * LANES - GATE_RANK, W), g_up.dtype)], axis=0).astype(_BF16)
    return pl.pallas_call(
        _rwkv_kernel,
        grid=(B, nc),
        in_specs=[pl.BlockSpec((C, ZF_SLAB_PAD), lambda b, c: (b * nc + c, 0)),
                  pl.BlockSpec((1, ZF_SLAB_PAD), lambda b, c: (0, 0)),
                  pl.BlockSpec((8, W), lambda b, c: (0, 0)),
                  pl.BlockSpec((LANES, W), lambda b, c: (0, 0)),
                  pl.BlockSpec((LANES, W), lambda b, c: (0, 0)),
                  pl.BlockSpec((2 * LANES, W), lambda b, c: (0, 0))],
        out_specs=pl.BlockSpec((C, W), lambda b, c: (b * nc + c, 0)),
        out_shape=jax.ShapeDtypeStruct((B * S, W), _BF16),
        scratch_shapes=[pltpu.VMEM((C + 8, ZF_SLAB_PAD), _F32),
                        pltpu.VMEM((HEAD_PAIRS, LANES, LANES), _F32)],
        compiler_params=_params("parallel", "arbitrary"),
        name="rwkv7_mix",
    )(zf, mu_p, vecs, wup, aup, gup)


def _router_kernel(h_ref, g_ref, wr_ref, xn_ref, route_ref):
    xn = _rms_to_bf16(h_ref[...], g_ref[...])
    xn_ref[...] = xn
    logits = jnp.dot(xn, wr_ref[...], preferred_element_type=_F32)
    lane = lax.broadcasted_iota(jnp.int32, logits.shape, 1).astype(_F32)
    l1 = jnp.where(lane < N_EXPERTS, logits, -jnp.inf)
    m1 = jnp.max(l1, axis=1, keepdims=True)
    i1 = jnp.min(jnp.where(l1 == m1, lane, float(LANES)), axis=1, keepdims=True)
    l2 = jnp.where(lane == i1, -jnp.inf, l1)
    m2 = jnp.max(l2, axis=1, keepdims=True)
    i2 = jnp.min(jnp.where(l2 == m2, lane, float(LANES)), axis=1, keepdims=True)
    e2 = jnp.exp(m2 - m1)
    g1 = 1.0 / (1.0 + e2)
    g2 = e2 / (1.0 + e2)
    route_ref[...] = jnp.where(lane == 0.0, i1, jnp.where(lane == 1.0, i2,
                               jnp.where(lane == 2.0, g1, jnp.where(lane == 3.0, g2, 0.0))))


def moe_route(h, g, router, tm):
    T, D = h.shape
    wr = jnp.pad(router, ((0, 0), (0, LANES - N_EXPERTS))).astype(_BF16)
    return pl.pallas_call(
        _router_kernel,
        grid=(T // tm,),
        in_specs=[pl.BlockSpec((tm, D), lambda i: (i, 0)),
                  pl.BlockSpec((1, D), lambda i: (0, 0)),
                  pl.BlockSpec((D, LANES), lambda i: (0, 0))],
        out_specs=[pl.BlockSpec((tm, D), lambda i: (i, 0)),
                   pl.BlockSpec((tm, LANES), lambda i: (i, 0))],
        out_shape=[jax.ShapeDtypeStruct((T, D), _BF16), jax.ShapeDtypeStruct((T, LANES), _F32)],
        compiler_params=_params("parallel"),
        name="moe_route",
    )(h, g.reshape(1, D), wr)


def _moe_ffn_kernel(te_ref, nv_ref, x_ref, w1_ref, w3_ref, w2_ref, o_ref):
    n = pl.program_id(0)
    f = pl.program_id(1)

    @pl.when(jnp.logical_and(n >= nv_ref[0], f == 0))
    def _():
        o_ref[...] = jnp.zeros(o_ref.shape, o_ref.dtype)

    @pl.when(n < nv_ref[0])
    def _():
        x = x_ref[...]
        a = jnp.dot(x, w1_ref[...], preferred_element_type=_F32)
        b = jnp.dot(x, w3_ref[...], preferred_element_type=_F32)
        mid = (a * _sigmoid(a) * b).astype(_BF16)
        contrib = jnp.dot(mid, w2_ref[...], preferred_element_type=_F32)

        @pl.when(f == 0)
        def _():
            o_ref[...] = contrib

        @pl.when(f > 0)
        def _():
            o_ref[...] += contrib


def moe_grouped_ffn(xs, tile_expert, n_valid, w1, w3, w2):
    R, D = xs.shape
    F = w1.shape[2]
    nf = F // MOE_TF

    def f_idx(n, f, nv):
        return jnp.where(n < nv[0], f, nf - 1)

    return pl.pallas_call(
        _moe_ffn_kernel,
        grid_spec=pltpu.PrefetchScalarGridSpec(
            num_scalar_prefetch=2,
            grid=(R // MOE_TM, nf),
            in_specs=[pl.BlockSpec((MOE_TM, D), lambda n, f, te, nv: (n, 0)),
                      pl.BlockSpec((None, D, MOE_TF), lambda n, f, te, nv: (te[n], 0, f_idx(n, f, nv))),
                      pl.BlockSpec((None, D, MOE_TF), lambda n, f, te, nv: (te[n], 0, f_idx(n, f, nv))),
                      pl.BlockSpec((None, MOE_TF, D), lambda n, f, te, nv: (te[n], f_idx(n, f, nv), 0))],
            out_specs=pl.BlockSpec((MOE_TM, D), lambda n, f, te, nv: (n, 0))),
        out_shape=jax.ShapeDtypeStruct((R, D), _F32),
        compiler_params=_params("parallel", "arbitrary"),
        name="moe_grouped_ffn",
    )(tile_expert, n_valid, xs, w1, w3, w2)


def moe_residual(h, g, router, w1, w3, w2):
    T, D = h.shape
    E = N_EXPERTS
    xn, route = moe_route(h, g, router, tm=512)
    top = route[:, 0:TOP_K_EXPERTS].astype(jnp.int32)
    gates = route[:, TOP_K_EXPERTS:2 * TOP_K_EXPERTS]
    ef = top.T.reshape(TOP_K_EXPERTS * T)
    onehot = (ef[:, None] == jnp.arange(E, dtype=jnp.int32)[None, :]).astype(jnp.int32)
    csum = jnp.cumsum(onehot, axis=0)
    rank = jnp.take_along_axis(csum, ef[:, None], axis=1)[:, 0] - 1
    counts = csum[-1]
    padded = ((counts + MOE_TM - 1) // MOE_TM) * MOE_TM
    pends = jnp.cumsum(padded)
    pstarts = pends - padded
    starts = jnp.cumsum(counts) - counts
    pos = pstarts[ef] + rank
    R = TOP_K_EXPERTS * T + E * MOE_TM
    n_tiles = R // MOE_TM
    tile_expert = jnp.searchsorted(pends, jnp.arange(n_tiles, dtype=jnp.int32) * MOE_TM, side='right')
    tile_expert = jnp.minimum(tile_expert, E - 1).astype(jnp.int32)
    n_valid = (pends[-1:] // MOE_TM).astype(jnp.int32)
    order = jnp.argsort(ef, stable=True).astype(jnp.int32)
    rho = jnp.arange(R, dtype=jnp.int32)
    e_row = tile_expert[rho // MOE_TM]
    off = rho - pstarts[e_row]
    src = order[jnp.clip(starts[e_row] + off, 0, TOP_K_EXPERTS * T - 1)] % T
    xs = jnp.take(xn, src, axis=0)
    ys = moe_grouped_ffn(xs, tile_expert, n_valid, w1, w3, w2)
    mix = sum(gates[:, slot:slot + 1] * jnp.take(ys, pos[slot * T:(slot + 1) * T], axis=0)
              for slot in range(TOP_K_EXPERTS))
    return h + mix


def _pack_w_in(w):
    wq, wk, wv, wqi, wki, wwi, wslab, wga, wgb = _split(w, IN_SIZES)
    D = w.shape[0]
    wb = jnp.concatenate([wq * (ATT_HEAD_DIM ** -0.5 * LOG2_E), wk, wv, wqi, wki, wki], axis=1)
    wf = jnp.concatenate([wslab, jnp.zeros((D, ZF_SLAB_PAD - RWKV_SLAB), w.dtype),
                          wki, wwi, jnp.zeros((D, LANES - IDX_DIM - IDX_HEADS), w.dtype), wga, wgb], axis=1)
    return wb.astype(_BF16), wf.astype(_BF16)


def kernel(x, p, w_in, att_up, rwkv_up, w_out, rel_bias, rwkv_mu, rwkv_w0, rwkv_w_up,
           rwkv_a0, rwkv_a_up, rwkv_g_up, rwkv_k_k, rwkv_k_a, rwkv_r_k, rwkv_gn_w, rwkv_gn_b,
           norm_mix, norm_ffn, norm_ple, ple_proj, ple_gate, ffn_w1, ffn_w3, ffn_w2,
           moe_router, moe_w1, moe_w3, moe_w2, final_norm):
    B, S, D = x.shape
    T = B * S
    depth = w_in.shape[0]
    h = x.reshape(T, D)
    for i in range(depth):
        wb, wf = _pack_w_in(w_in[i])
        zb = norm_matmul(h, norm_mix[i], wb, _BF16, tm=512, tn=ZB_WIDTH)
        zf = norm_matmul(h, norm_mix[i], wf, _F32, tm=512, tn=1536)
        att = dsa_attention(zb, zf, rel_bias, B, S)
        rw = rwkv7_mix(zf, rwkv_mu[i], rwkv_w0[i], rwkv_w_up[i], rwkv_a0[i], rwkv_a_up[i], rwkv_g_up[i],
                       rwkv_k_k[i], rwkv_k_a[i], rwkv_r_k[i], rwkv_gn_w[i], rwkv_gn_b[i], B, S)
        merged = gated_merge(att, rw, zf, att_up[i].astype(_BF16), rwkv_up[i].astype(_BF16), tm=512, tn=512)
        h = matmul_residual(merged, w_out[i].astype(_BF16), h, tm=512, tn=1024)
        if i % 2 == 0:
            j = i // 2
            h = ffn_residual(h, norm_ffn[i], ffn_w1[j].astype(_BF16), ffn_w3[j].astype(_BF16),
                             ffn_w2[j].astype(_BF16), tm=512, tf=512)
        else:
            j = i // 2
            h = moe_residual(h, norm_ffn[i], moe_router[j], moe_w1[j].astype(_BF16), moe_w3[j].astype(_BF16),
                             moe_w2[j].astype(_BF16))
        h = ple_residual(h, norm_ple[i], p[i].reshape(T, PLE_DIM), ple_gate[i].astype(_BF16),
                         ple_proj[i].astype(_BF16), final_norm, final=(i == depth - 1), tm=256)
    return h.reshape(B, S, D)
```

```python
import functools
import math

import jax
import jax.numpy as jnp
import numpy as np
from jax import lax
from jax.experimental import pallas as pl
from jax.experimental.pallas import tpu as pltpu

D_MODEL = 2048
DEPTH = 2
ATT_HEADS = 8
ATT_KV_HEADS = 2
ATT_HEAD_DIM = 128
IDX_HEADS = 8
IDX_DIM = 64
TOPK_MAX = 256
REL_BUCKETS = 32
REL_MAX_DIST = 128
RWKV_HEAD = 64
RWKV_HEADS = 16
DECAY_RANK = 64
ICLR_RANK = 64
GATE_RANK = 160
D_FF = 5632
N_EXPERTS = 8
TOP_K_EXPERTS = 2
D_FF_EXPERT = 7168
PLE_DIM = 256
RMS_EPS = 1e-6
GN_EPS = 64e-5

ATT_WIDTH = ATT_HEADS * ATT_HEAD_DIM
KV_WIDTH = ATT_KV_HEADS * ATT_HEAD_DIM
RWKV_WIDTH = RWKV_HEADS * RWKV_HEAD
RWKV_SIZES = (RWKV_WIDTH, RWKV_WIDTH, RWKV_WIDTH, DECAY_RANK, ICLR_RANK, GATE_RANK)
RWKV_SLAB = 3 * RWKV_WIDTH + DECAY_RANK + ICLR_RANK + GATE_RANK
IN_SIZES = (ATT_WIDTH, KV_WIDTH, KV_WIDTH, IDX_HEADS * IDX_DIM, IDX_DIM, IDX_HEADS, RWKV_SLAB, D_MODEL, D_MODEL)

LANES = 128
VMEM_LIMIT = 56 * 1024 * 1024
CAST_BLOCK_BYTES = 8 * 1024 * 1024

Q_BLOCK = 128
KEY_CHUNK = 512
BLOCKS_PER_CHUNK = KEY_CHUNK // Q_BLOCK
BLOCKS_PER_GROUP = 32
MASKED_LOGIT = -1e30
LOG2_E = math.log2(math.e)
GQA_GROUP = ATT_HEADS // ATT_KV_HEADS

ZB_WIDTH = ATT_WIDTH + 2 * KV_WIDTH + IDX_HEADS * IDX_DIM + 2 * IDX_DIM
ZF_SLAB_PAD = 3456
ZF_WIDX = ZF_SLAB_PAD
ZF_GATE_A = ZF_WIDX + LANES
ZF_GATE_B = ZF_GATE_A + D_MODEL
ZF_WIDTH = ZF_GATE_B + D_MODEL

RWKV_CHUNK = 64
HEAD_PAIRS = RWKV_HEADS // 2
MOE_TM = 512
MOE_TF = 1024

_F32 = jnp.float32
_BF16 = jnp.bfloat16
_NT = (((1,), (1,)), ((), ()))
_TN = (((0,), (0,)), ((), ()))


def _split(z, sizes):
    return jnp.split(z, np.cumsum(sizes)[:-1].tolist(), axis=-1)


def _sigmoid(x):
    return 1.0 / (1.0 + jnp.exp(-x))


def _rms_to_bf16(x, g):
    y = x * lax.rsqrt(jnp.mean(x * x, axis=-1, keepdims=True) + RMS_EPS)
    return (y * g).astype(_BF16)


def _params(*sem):
    return pltpu.CompilerParams(dimension_semantics=sem, vmem_limit_bytes=VMEM_LIMIT)


def _norm_mm_kernel(x_ref, g_ref, w_ref, o_ref, xn_ref):
    @pl.when(pl.program_id(1) == 0)
    def _():
        xn_ref[...] = _rms_to_bf16(x_ref[...], g_ref[...])

    o_ref[...] = jnp.dot(xn_ref[...], w_ref[...], preferred_element_type=_F32).astype(o_ref.dtype)


def norm_matmul(x, g, w, out_dtype, tm, tn):
    T, D = x.shape
    N = w.shape[1]
    return pl.pallas_call(
        _norm_mm_kernel,
        grid=(T // tm, N // tn),
        in_specs=[pl.BlockSpec((tm, D), lambda i, j: (i, 0)),
                  pl.BlockSpec((1, D), lambda i, j: (0, 0)),
                  pl.BlockSpec((D, tn), lambda i, j: (0, j))],
        out_specs=pl.BlockSpec((tm, tn), lambda i, j: (i, j)),
        out_shape=jax.ShapeDtypeStruct((T, N), out_dtype),
        scratch_shapes=[pltpu.VMEM((tm, D), _BF16)],
        compiler_params=_params("parallel", "arbitrary"),
        name="norm_matmul",
    )(x, g.reshape(1, D), w)


def _merge_kernel(att_ref, rw_ref, ga_ref, gb_ref, wa_ref, wr_ref, o_ref):
    a = jnp.dot(att_ref[...], wa_ref[...], preferred_element_type=_F32)
    r = jnp.dot(rw_ref[...], wr_ref[...], preferred_element_type=_F32)
    o_ref[...] = (_sigmoid(ga_ref[...]) * a + _sigmoid(gb_ref[...]) * r).astype(o_ref.dtype)


def gated_merge(att, rw, zf, wa, wr, tm, tn):
    T, K = att.shape
    N = wa.shape[1]
    ja, jb = ZF_GATE_A // tn, ZF_GATE_B // tn
    return pl.pallas_call(
        _merge_kernel,
        grid=(T // tm, N // tn),
        in_specs=[pl.BlockSpec((tm, K), lambda i, j: (i, 0)),
                  pl.BlockSpec((tm, K), lambda i, j: (i, 0)),
                  pl.BlockSpec((tm, tn), lambda i, j: (i, ja + j)),
                  pl.BlockSpec((tm, tn), lambda i, j: (i, jb + j)),
                  pl.BlockSpec((K, tn), lambda i, j: (0, j)),
                  pl.BlockSpec((K, tn), lambda i, j: (0, j))],
        out_specs=pl.BlockSpec((tm, tn), lambda i, j: (i, j)),
        out_shape=jax.ShapeDtypeStruct((T, N), _BF16),
        compiler_params=_params("parallel", "arbitrary"),
        name="gated_merge",
    )(att, rw, zf, zf, wa, wr)


def _mm_res_kernel(x_ref, w_ref, h_ref, o_ref):
    o_ref[...] = h_ref[...] + jnp.dot(x_ref[...], w_ref[...], preferred_element_type=_F32)


def matmul_residual(x, w, h, tm, tn):
    T, K = x.shape
    N = w.shape[1]
    return pl.pallas_call(
        _mm_res_kernel,
        grid=(T // tm, N // tn),
        in_specs=[pl.BlockSpec((tm, K), lambda i, j: (i, 0)),
                  pl.BlockSpec((K, tn), lambda i, j: (0, j)),
                  pl.BlockSpec((tm, tn), lambda i, j: (i, j))],
        out_specs=pl.BlockSpec((tm, tn), lambda i, j: (i, j)),
        out_shape=jax.ShapeDtypeStruct((T, N), _F32),
        compiler_params=_params("parallel", "arbitrary"),
        name="matmul_residual",
    )(x, w, h)


def _ffn_kernel(h_ref, g_ref, w1_ref, w3_ref, w2_ref, o_ref, xn_ref):
    f = pl.program_id(1)

    @pl.when(f == 0)
    def _():
        xn_ref[...] = _rms_to_bf16(h_ref[...], g_ref[...])
        o_ref[...] = h_ref[...]

    xn = xn_ref[...]
    a = jnp.dot(xn, w1_ref[...], preferred_element_type=_F32)
    b = jnp.dot(xn, w3_ref[...], preferred_element_type=_F32)
    mid = (a * _sigmoid(a) * b).astype(_BF16)
    o_ref[...] += jnp.dot(mid, w2_ref[...], preferred_element_type=_F32)


def ffn_residual(h, g, w1, w3, w2, tm, tf):
    T, D = h.shape
    F = w1.shape[1]
    return pl.pallas_call(
        _ffn_kernel,
        grid=(T // tm, F // tf),
        in_specs=[pl.BlockSpec((tm, D), lambda i, f: (i, 0)),
                  pl.BlockSpec((1, D), lambda i, f: (0, 0)),
                  pl.BlockSpec((D, tf), lambda i, f: (0, f)),
                  pl.BlockSpec((D, tf), lambda i, f: (0, f)),
                  pl.BlockSpec((tf, D), lambda i, f: (f, 0))],
        out_specs=pl.BlockSpec((tm, D), lambda i, f: (i, 0)),
        out_shape=jax.ShapeDtypeStruct((T, D), _F32),
        scratch_shapes=[pltpu.VMEM((tm, D), _BF16)],
        compiler_params=_params("parallel", "arbitrary"),
        name="ffn_residual",
    )(h, g.reshape(1, D), w1, w3, w2)


def _ple_kernel(h_ref, g_ref, p_ref, wg_ref, wp_ref, fg_ref, o_ref, *, final):
    h = h_ref[...]
    xn = _rms_to_bf16(h, g_ref[...])
    gate = _sigmoid(jnp.dot(xn, wg_ref[...], preferred_element_type=_F32))
    pp = jnp.dot(p_ref[...].astype(_BF16), wp_ref[...], preferred_element_type=_F32)
    out = h + gate * pp
    if final:
        out = out * lax.rsqrt(jnp.mean(out * out, axis=-1, keepdims=True) + RMS_EPS) * fg_ref[...]
    o_ref[...] = out


def ple_residual(h, g, p, wg, wp, final_g, final, tm):
    T, D = h.shape
    P = p.shape[1]
    return pl.pallas_call(
        functools.partial(_ple_kernel, final=final),
        grid=(T // tm,),
        in_specs=[pl.BlockSpec((tm, D), lambda i: (i, 0)),
                  pl.BlockSpec((1, D), lambda i: (0, 0)),
                  pl.BlockSpec((tm, P), lambda i: (i, 0)),
                  pl.BlockSpec((D, D), lambda i: (0, 0)),
                  pl.BlockSpec((P, D), lambda i: (0, 0)),
                  pl.BlockSpec((1, D), lambda i: (0, 0))],
        out_specs=pl.BlockSpec((tm, D), lambda i: (i, 0)),
        out_shape=jax.ShapeDtypeStruct((T, D), _F32),
        compiler_params=_params("parallel"),
        name="ple_residual",
    )(h, g.reshape(1, D), p, wg, wp, final_g.reshape(1, D))


def _cast_kernel(x_ref, o_ref):
    o_ref[...] = x_ref[...].astype(o_ref.dtype)


def cast_bf16(w):
    E, R, N = w.shape
    rows = R
    while rows * N * 4 > CAST_BLOCK_BYTES and rows % 2 == 0:
        rows //= 2
    return pl.pallas_call(
        _cast_kernel,
        grid=(E, R // rows),
        in_specs=[pl.BlockSpec((None, rows, N), lambda e, i: (e, i, 0))],
        out_specs=pl.BlockSpec((None, rows, N), lambda e, i: (e, i, 0)),
        out_shape=jax.ShapeDtypeStruct(w.shape, _BF16),
        compiler_params=_params("parallel", "parallel"),
        name="cast_bf16",
    )(w)


def rel_bucket(dist):
    max_exact = REL_BUCKETS // 2
    dist = jnp.maximum(dist, 0)
    d_f = jnp.maximum(dist, 1).astype(_F32)
    large = max_exact + (jnp.log(d_f / max_exact) / math.log(REL_MAX_DIST / max_exact)
                         * (REL_BUCKETS - max_exact)).astype(jnp.int32)
    large = jnp.minimum(large, REL_BUCKETS - 1)
    return jnp.where(dist < max_exact, dist, large)


def _sortable(x):
    b = lax.bitcast_convert_type(x, jnp.int32)
    return b ^ ((b >> 31) & jnp.int32(0x7FFFFFFF))


_INT_MIN = -2 ** 31
_KEY_NEG_INF = 0x807FFFFF - 2 ** 32


def _dsa_kernel(q_ref, qi_ref, wi_ref, k_ref, v_ref, ki_ref, bprev_ref, bdiag_ref, bfar_ref, o_ref,
                keys_ref, planes_ref, active_ref, qs_ref, qim_ref, m_ref, acc_ref, *, n_sel, idx_bits):
    n_grp = planes_ref.shape[0]
    qb = pl.program_id(1)
    t0 = qb * Q_BLOCK
    n_chunks = qb // BLOCKS_PER_CHUNK + 1
    t_idx = t0 + lax.broadcasted_iota(jnp.int32, (Q_BLOCK, 1), 0)
    lane_blk = lax.broadcasted_iota(jnp.int32, (1, Q_BLOCK), 1)
    lane_chunk = lax.broadcasted_iota(jnp.int32, (1, KEY_CHUNK), 1)

    for hh in range(ATT_HEADS):
        g, h = divmod(hh, GQA_GROUP)
        qs_ref[g, h * Q_BLOCK:(h + 1) * Q_BLOCK, :] = q_ref[:, hh * ATT_HEAD_DIM:(hh + 1) * ATT_HEAD_DIM]
    low = lax.broadcasted_iota(jnp.int32, (Q_BLOCK, LANES), 1) < IDX_DIM
    for hh in range(IDX_HEADS):
        tile = qi_ref[:, (hh // 2) * LANES:(hh // 2 + 1) * LANES]
        keep = low if hh % 2 == 0 else jnp.logical_not(low)
        qim_ref[hh] = jnp.where(keep, tile, jnp.zeros_like(tile))
    m_ref[...] = jnp.full(m_ref.shape, MASKED_LOGIT, _F32)

    @pl.when(qb == 0)
    def _():
        keys_ref[BLOCKS_PER_CHUNK:] = jnp.zeros((keys_ref.shape[0] - BLOCKS_PER_CHUNK, Q_BLOCK, Q_BLOCK), jnp.int32)
        planes_ref[...] = jnp.zeros(planes_ref.shape, jnp.int32)
    acc_ref[...] = jnp.zeros(acc_ref.shape, _F32)

    w_all = wi_ref[...]
    idx_scale = (IDX_HEADS ** -0.5) * (IDX_DIM ** -0.5)

    def score_body(c, carry):
        ks = pl.multiple_of(c * KEY_CHUNK, KEY_CHUNK)
        kic = ki_ref[pl.ds(ks, KEY_CHUNK), :]
        d = lax.dot_general(qim_ref[...].reshape(IDX_HEADS * Q_BLOCK, LANES), kic, _NT,
                            preferred_element_type=_F32)
        sc = jnp.zeros((Q_BLOCK, KEY_CHUNK), _F32)
        for hh in range(IDX_HEADS):
            sc = sc + w_all[:, IDX_DIM + hh:IDX_DIM + hh + 1] * jnp.maximum(d[hh * Q_BLOCK:(hh + 1) * Q_BLOCK], 0.0)
        sc = sc * idx_scale
        sc = jnp.where(sc == 0.0, 0.0, sc)
        key = jnp.where(ks + lane_chunk <= t_idx, _sortable(sc), _KEY_NEG_INF)
        for j in range(BLOCKS_PER_CHUNK):
            keys_ref[c * BLOCKS_PER_CHUNK + j] = key[:, j * Q_BLOCK:(j + 1) * Q_BLOCK]
        return carry

    lax.fori_loop(0, n_chunks, score_body, 0)

    def count(pred):
        def body(c, acc):
            for j in range(BLOCKS_PER_CHUNK):
                blk = c * BLOCKS_PER_CHUNK + j
                acc = acc + jnp.where(pred(keys_ref[blk], blk * Q_BLOCK + lane_blk), 1.0, 0.0)
            return acc

        acc = lax.fori_loop(0, n_chunks, body, jnp.zeros((Q_BLOCK, Q_BLOCK), _F32))
        return jnp.sum(acc, axis=1, keepdims=True)

    n_blocks = n_chunks * BLOCKS_PER_CHUNK
    n_groups = (n_blocks + BLOCKS_PER_GROUP - 1) // BLOCKS_PER_GROUP

    def transpose_group(g, carry):
        def rows(rg, carry_rows):
            r0 = pl.multiple_of(rg * 8, 8)
            x = [keys_ref[g * BLOCKS_PER_GROUP + j, pl.ds(r0, 8), :] for j in range(BLOCKS_PER_GROUP)]
            j, m = 16, 0x0000FFFF
            while j:
                k = 0
                while k < BLOCKS_PER_GROUP:
                    t = (x[k] ^ lax.shift_right_logical(x[k + j], jnp.int32(j))) & jnp.int32(m)
                    x[k] = x[k] ^ t
                    x[k + j] = x[k + j] ^ lax.shift_left(t, jnp.int32(j))
                    k = (k + j + 1) & ~j
                j >>= 1
                m ^= m << j
            planes_ref[g, 0, pl.ds(r0, 8), :] = ~x[0]
            for i in range(1, 32):
                planes_ref[g, i, pl.ds(r0, 8), :] = x[i]
            return carry_rows

        lax.fori_loop(0, Q_BLOCK // 8, rows, 0)
        n_live = jnp.minimum(n_blocks - g * BLOCKS_PER_GROUP, BLOCKS_PER_GROUP)
        active_ref[g] = lax.shift_left(jnp.full((Q_BLOCK, Q_BLOCK), -1, jnp.int32), BLOCKS_PER_GROUP - n_live)
        return carry

    active_ref[...] = jnp.zeros(active_ref.shape, jnp.int32)
    lax.fori_loop(0, n_groups, transpose_group, 0)

    def lane_total(words):
        return jnp.sum(sum(lax.population_count(w) for w in words).astype(_F32), axis=1, keepdims=True)

    def split_active(g, i):
        act = active_ref[g]
        hi = act & planes_ref[g, 2 * i]
        lo = act ^ hi
        hi1 = hi & planes_ref[g, 2 * i + 1]
        lo1 = lo & planes_ref[g, 2 * i + 1]
        return hi1, hi ^ hi1, lo1, lo ^ lo1

    def radix_pass(i, carry):
        prefix, above = carry
        parts = [split_active(g, i) for g in range(n_grp)]
        c11, c10, c01 = (lane_total([parts[g][q] for g in range(n_grp)]) for q in range(3))
        upto10 = above + c11 + c10
        t11 = above + c11 >= n_sel
        t10 = upto10 >= n_sel
        t01 = upto10 + c01 >= n_sel
        for g in range(n_grp):
            s11, s10, s01, s00 = split_active(g, i)
            active_ref[g] = jnp.where(t11, s11, jnp.where(t10, s10, jnp.where(t01, s01, s00)))
        bit1 = jnp.where(t10, lax.shift_left(jnp.int32(1), 31 - 2 * i), 0)
        bit0 = jnp.where(jnp.logical_or(t11, jnp.logical_and(jnp.logical_not(t10), t01)),
                         lax.shift_left(jnp.int32(1), 30 - 2 * i), 0)
        above = jnp.where(t11, above, jnp.where(t10, above + c11, jnp.where(t01, upto10, upto10 + c01)))
        return prefix | bit1 | bit0, above

    prefix, n_gt = lax.fori_loop(0, 16, radix_pass, (jnp.zeros((Q_BLOCK, 1), jnp.int32),
                                                     jnp.zeros((Q_BLOCK, 1), _F32)))
    tau = prefix ^ jnp.int32(_INT_MIN)
    n_eq = lane_total([active_ref[g] for g in range(n_grp)])
    tied = jnp.logical_and(n_gt + n_eq > n_sel, tau > _KEY_NEG_INF)

    def tie_search():
        need = n_sel - count(lambda key, sidx: key > tau)

        def index_bit(i, x):
            cand = x | lax.shift_left(jnp.int32(1), idx_bits - 1 - i)
            cnt = count(lambda key, sidx: jnp.logical_and(key == tau, sidx < cand))
            return jnp.where(cnt < need, cand, x)

        return lax.fori_loop(0, idx_bits, index_bit, jnp.zeros((Q_BLOCK, 1), jnp.int32))

    last_tie = lax.cond(jnp.max(jnp.where(tied, 1.0, 0.0)) > 0.0, tie_search,
                        lambda: jnp.full((Q_BLOCK, 1), 2 ** idx_bits, jnp.int32))

    def attend(key, k_tile, v_tile, s0, lane, bias_ref, enable=None, const_bias=False):
        width = key.shape[1]
        sidx = s0 + lane
        sel = jnp.logical_or(key > tau, jnp.logical_and(key == tau, sidx <= last_tie))
        sel = jnp.logical_and(sel, key > _KEY_NEG_INF)
        if enable is not None:
            sel = jnp.logical_and(sel, jnp.zeros_like(sidx) + enable > 0)
        ones = jnp.ones((width, ATT_HEAD_DIM), _BF16)
        groups = range(ATT_KV_HEADS)
        cols = [slice(g * ATT_HEAD_DIM, (g + 1) * ATT_HEAD_DIM) for g in groups]

        def logits(g):
            return lax.dot_general(qs_ref[g], k_tile[:, cols[g]], _NT,
                                   preferred_element_type=_F32).reshape(GQA_GROUP, Q_BLOCK, width)

        def softmax(g, s):
            bias = bias_ref[g * GQA_GROUP:(g + 1) * GQA_GROUP]
            m_old = m_ref[g]
            if const_bias:
                sh = jnp.where(sel[None], s, MASKED_LOGIT)
                m_new = jnp.maximum(m_old, jnp.max(sh, axis=2, keepdims=True) + bias)
                shift = m_new - bias
            else:
                sh = jnp.where(sel[None], s + bias, MASKED_LOGIT)
                m_new = jnp.maximum(m_old, jnp.max(sh, axis=2, keepdims=True))
                shift = m_new
            m_ref[g] = m_new
            return (jnp.exp2(sh - shift).reshape(GQA_GROUP * Q_BLOCK, width).astype(_BF16),
                    jnp.exp2(m_old - m_new).reshape(GQA_GROUP * Q_BLOCK, 1))

        def accumulate(g, p, alpha):
            pv = jnp.dot(p, jnp.concatenate([v_tile[:, cols[g]], ones], axis=1), preferred_element_type=_F32)
            acc_ref[g] = acc_ref[g] * alpha + pv

        if width <= Q_BLOCK:
            s = [logits(g) for g in groups]
            pa = [softmax(g, s[g]) for g in groups]
            for g in groups:
                accumulate(g, *pa[g])
        else:
            for g in groups:
                accumulate(g, *softmax(g, logits(g)))

    n_far = jnp.maximum(qb - 1, 0) // BLOCKS_PER_CHUNK

    def far_body(c, carry):
        ks = pl.multiple_of(c * KEY_CHUNK, KEY_CHUNK)
        key = jnp.concatenate([keys_ref[c * BLOCKS_PER_CHUNK + j] for j in range(BLOCKS_PER_CHUNK)], axis=1)
        attend(key, k_ref[pl.ds(ks, KEY_CHUNK), :], v_ref[pl.ds(ks, KEY_CHUNK), :], ks, lane_chunk, bfar_ref,
               const_bias=True)
        return carry

    lax.fori_loop(0, n_far, far_body, 0)

    def block_step(j, bias_ref, enable=None, const_bias=False):
        ks = pl.multiple_of(j * Q_BLOCK, Q_BLOCK)
        attend(keys_ref[j], k_ref[pl.ds(ks, Q_BLOCK), :], v_ref[pl.ds(ks, Q_BLOCK), :], ks, lane_blk, bias_ref,
               enable, const_bias)

    def tail_body(j, carry):
        block_step(j, bfar_ref, const_bias=True)
        return carry

    lax.fori_loop(n_far * BLOCKS_PER_CHUNK, qb - 1, tail_body, 0)
    block_step(jnp.maximum(qb - 1, 0), bprev_ref, enable=jnp.minimum(qb, 1))
    block_step(qb, bdiag_ref)

    for hh in range(ATT_HEADS):
        g, h = divmod(hh, GQA_GROUP)
        rows = slice(h * Q_BLOCK, (h + 1) * Q_BLOCK)
        o_ref[:, hh * ATT_HEAD_DIM:(hh + 1) * ATT_HEAD_DIM] = (
            acc_ref[g, rows, 0:ATT_HEAD_DIM] / acc_ref[g, rows, ATT_HEAD_DIM:2 * ATT_HEAD_DIM]).astype(o_ref.dtype)


def dsa_attention(zb, zf, rel_bias, B, S):
    nq = S // Q_BLOCK
    n_grp = -(-nq // BLOCKS_PER_GROUP)
    n_sel = min(TOPK_MAX, S // 4)
    tl = jnp.arange(Q_BLOCK, dtype=jnp.int32)[:, None]
    sr = jnp.arange(Q_BLOCK, dtype=jnp.int32)[None, :]
    rel_bias = rel_bias.astype(_F32) * LOG2_E
    bprev = rel_bias[rel_bucket(tl + Q_BLOCK - sr)].transpose(2, 0, 1)
    bdiag = rel_bias[rel_bucket(tl - sr)].transpose(2, 0, 1)
    bfar = rel_bias[rel_bucket(jnp.int32(REL_MAX_DIST + 1))].reshape(ATT_HEADS, 1, 1)
    kernel = functools.partial(_dsa_kernel, n_sel=float(n_sel), idx_bits=int(S - 1).bit_length())
    c_k = ATT_WIDTH // KV_WIDTH
    c_qi = (ATT_WIDTH + 2 * KV_WIDTH) // (IDX_HEADS * IDX_DIM)
    c_ki = (ATT_WIDTH + 2 * KV_WIDTH + IDX_HEADS * IDX_DIM) // LANES
    return pl.pallas_call(
        kernel,
        grid=(B, nq),
        in_specs=[pl.BlockSpec((Q_BLOCK, ATT_WIDTH), lambda b, i: (b * nq + i, 0)),
                  pl.BlockSpec((Q_BLOCK, IDX_HEADS * IDX_DIM), lambda b, i: (b * nq + i, c_qi)),
                  pl.BlockSpec((Q_BLOCK, LANES), lambda b, i: (b * nq + i, ZF_WIDX // LANES)),
                  pl.BlockSpec((S, KV_WIDTH), lambda b, i: (b, c_k)),
                  pl.BlockSpec((S, KV_WIDTH), lambda b, i: (b, c_k + 1)),
                  pl.BlockSpec((S, LANES), lambda b, i: (b, c_ki)),
                  pl.BlockSpec((ATT_HEADS, Q_BLOCK, Q_BLOCK), lambda b, i: (0, 0, 0)),
                  pl.BlockSpec((ATT_HEADS, Q_BLOCK, Q_BLOCK), lambda b, i: (0, 0, 0)),
                  pl.BlockSpec((ATT_HEADS, 1, 1), lambda b, i: (0, 0, 0))],
        out_specs=pl.BlockSpec((Q_BLOCK, ATT_WIDTH), lambda b, i: (b * nq + i, 0)),
        out_shape=jax.ShapeDtypeStruct((B * S, ATT_WIDTH), _BF16),
        scratch_shapes=[pltpu.VMEM((n_grp * BLOCKS_PER_GROUP, Q_BLOCK, Q_BLOCK), jnp.int32),
                        pltpu.VMEM((n_grp, 32, Q_BLOCK, Q_BLOCK), jnp.int32),
                        pltpu.VMEM((n_grp, Q_BLOCK, Q_BLOCK), jnp.int32),
                        pltpu.VMEM((ATT_KV_HEADS, GQA_GROUP * Q_BLOCK, ATT_HEAD_DIM), _BF16),
                        pltpu.VMEM((IDX_HEADS, Q_BLOCK, LANES), _BF16),
                        pltpu.VMEM((ATT_KV_HEADS, GQA_GROUP, Q_BLOCK, 1), _F32),
                        pltpu.VMEM((ATT_KV_HEADS, GQA_GROUP * Q_BLOCK, 2 * ATT_HEAD_DIM), _F32)],
        compiler_params=_params("parallel", "arbitrary"),
        name="dsa_attention",
    )(zb, zb, zf, zb, zb, zb, bprev, bdiag, bfar)


def _split_dot(x, w):
    hi = x.astype(_BF16)
    lo = (x - hi.astype(_F32)).astype(_BF16)
    return jnp.dot(hi, w, preferred_element_type=_F32) + jnp.dot(lo, w, preferred_element_type=_F32)


def _split_dot_left(w, x):
    hi = x.astype(_BF16)
    lo = (x - hi.astype(_F32)).astype(_BF16)
    return jnp.dot(w, hi, preferred_element_type=_F32) + jnp.dot(w, lo, preferred_element_type=_F32)


def _rwkv_kernel(slab_ref, mu_ref, vec_ref, wup_ref, aup_ref, gup_ref, o_ref, xs_ref, state_ref):
    C = RWKV_CHUNK
    W = RWKV_WIDTH
    n_batch = slab_ref.shape[0]
    c = pl.program_id(0)

    @pl.when(c == 0)
    def _():
        xs_ref[:, 0:8, :] = jnp.zeros((n_batch, 8, ZF_SLAB_PAD), _F32)
        state_ref[...] = jnp.zeros(state_ref.shape, _F32)

    w0, a0, k_k, k_a, r_k, gn_w, gn_b = (vec_ref[i:i + 1, :] for i in range(7))
    ti = lax.broadcasted_iota(jnp.int32, (C, C), 0)
    tj = lax.broadcasted_iota(jnp.int32, (C, C), 1)
    tri = jnp.where(tj <= ti, 1.0, 0.0).astype(_BF16)

    def prepare(bi):
        slab = slab_ref[bi]
        xs_ref[bi, 8:8 + C, :] = slab
        shifted = xs_ref[bi, 7:7 + C, :]
        xs = slab + (shifted - slab) * mu_ref[...]
        xs_ref[bi, 0:8, :] = slab[C - 8:C, :]

        r = xs[:, 0:W]
        k = xs[:, W:2 * W]
        lora = xs[:, 3 * W:3 * W + LANES]
        xg = xs[:, 3 * W + LANES:3 * W + 3 * LANES]
        d = w0 + jnp.dot(jnp.tanh(lora).astype(_BF16), wup_ref[...], preferred_element_type=_F32)
        logw = -math.exp(-0.5) * _sigmoid(d)
        a = _sigmoid(a0 + jnp.dot(lora.astype(_BF16), aup_ref[...], preferred_element_type=_F32))
        kt = k * (1.0 + (a - 1.0) * k_a)
        cum = _split_dot_left(tri, logw)
        cum_end = cum[C - 1:C, :]
        return dict(r=r, v=xs[:, 2 * W:3 * W], a=a, kt=kt, kk_raw=k * k_k, bonus_raw=r * kt * r_k,
                    g=jnp.dot(_sigmoid(xg).astype(_BF16), gup_ref[...], preferred_element_type=_F32),
                    g_incl=jnp.exp(cum), g_excl=jnp.exp(cum - logw), g_inv=jnp.exp(-cum),
                    g_end=jnp.exp(cum_end - cum), g_total=jnp.exp(cum_end))

    prepared = [prepare(bi) for bi in range(n_batch)]

    lo = lax.broadcasted_iota(jnp.int32, (C, LANES), 1) < RWKV_HEAD
    ri = lax.broadcasted_iota(jnp.int32, (LANES, LANES), 0)
    ci = lax.broadcasted_iota(jnp.int32, (LANES, LANES), 1)
    same_head = (ri // RWKV_HEAD) == (ci // RWKV_HEAD)
    ones_bd = jnp.where(same_head, 1.0, 0.0).astype(_BF16)
    eye = ri == ci
    same_blk = (ri // C) == (ci // C)
    strict = jnp.logical_and(same_blk, ci < ri)
    incl = jnp.logical_and(same_blk, ci <= ri)

    def stack(x):
        return jnp.concatenate([jnp.where(lo, x, 0.0), jnp.where(lo, 0.0, x)], axis=0)

    def dot(x, y):
        return jnp.dot(x.astype(_BF16), y.astype(_BF16), preferred_element_type=_F32)

    pairs = range(n_batch * HEAD_PAIRS)
    sls = [slice((n % HEAD_PAIRS) * LANES, (n % HEAD_PAIRS + 1) * LANES) for n in pairs]

    def col(name, n):
        return prepared[n // HEAD_PAIRS][name][:, sls[n]]

    h_bd = [state_ref[p] for p in pairs]
    kk = [col('kk_raw', p) * lax.rsqrt(jnp.maximum(_split_dot(col('kk_raw', p) * col('kk_raw', p), ones_bd), 1e-24))
          for p in pairs]
    b = [kk[p] * col('a', p) for p in pairs]
    lhs4 = [jnp.concatenate([stack(kk[p] * col('g_excl', p)), stack(col('r', p) * col('g_incl', p))],
                            axis=0).astype(_BF16) for p in pairs]
    rhs4 = []
    for p in pairs:
        bi = b[p] * col('g_inv', p)
        ki = col('kt', p) * col('g_inv', p)
        rhs4.append(jnp.concatenate([bi, bi, ki, ki], axis=0).astype(_BF16))
    gm = [lax.dot_general(lhs4[p], rhs4[p], _NT, preferred_element_type=_F32) for p in pairs]
    a_bd = [jnp.where(strict, gm[p][:2 * C, :2 * C], 0.0) for p in pairs]
    bp_bd = [jnp.concatenate([jnp.where(strict, gm[p][:2 * C, 2 * C:], 0.0),
                              jnp.where(incl, gm[p][2 * C:, 2 * C:], 0.0)], axis=0) for p in pairs]
    pb_bd = [jnp.where(incl, gm[p][2 * C:, :2 * C], 0.0) for p in pairs]

    inv = [jnp.where(eye, 1.0, 0.0) - jnp.where((ri // 2) == (ci // 2), a_bd[p], 0.0) for p in pairs]
    s = 2
    while s < C:
        off = jnp.logical_and((ri // (2 * s)) == (ci // (2 * s)), (ri // s) != (ci // s))
        tm = [dot(inv[p], jnp.where(off, a_bd[p], 0.0)) for p in pairs]
        inv = [inv[p] - dot(tm[p], inv[p]) for p in pairs]
        s *= 2

    v_st = [stack(col('v', p)) for p in pairs]
    hw = [dot(lhs4[p], h_bd[p]) for p in pairs]
    bv = [dot(bp_bd[p], v_st[p]) for p in pairs]
    u_st = [dot(inv[p], hw[p][:2 * C] + bv[p][:2 * C]) for p in pairs]
    y_st = [hw[p][2 * C:] + bv[p][2 * C:] - dot(pb_bd[p], u_st[p]) for p in pairs]
    y = [y_st[p][:C] + y_st[p][C:] for p in pairs]
    u = [u_st[p][:C] + u_st[p][C:] for p in pairs]

    for p in pairs:
        m_t = jnp.concatenate([col('kt', p) * col('g_end', p), -(b[p] * col('g_end', p))], axis=0).astype(_BF16)
        n_t = jnp.concatenate([col('v', p), u[p]], axis=0).astype(_BF16)
        dh = lax.dot_general(m_t, n_t, _TN, preferred_element_type=_F32)
        gt_col = jnp.sum(jnp.where(eye, jnp.broadcast_to(col('g_total', p), (LANES, LANES)), 0.0),
                         axis=1, keepdims=True)
        state_ref[p] = jnp.where(same_head, gt_col * h_bd[p] + dh, 0.0)

    mean = [_split_dot(y[p], ones_bd) * (1.0 / RWKV_HEAD) for p in pairs]
    yc = [y[p] - mean[p] for p in pairs]
    var = [_split_dot(yc[p] * yc[p], ones_bd) * (1.0 / RWKV_HEAD) for p in pairs]
    for p in pairs:
        sl = sls[p]
        yn = yc[p] * lax.rsqrt(var[p] + GN_EPS) * gn_w[:, sl] + gn_b[:, sl]
        bonus = _split_dot(col('bonus_raw', p), ones_bd) * col('v', p)
        o_ref[p // HEAD_PAIRS, :, sl] = ((yn + bonus) * col('g', p)).astype(o_ref.dtype)


def rwkv7_mix(zf, mu, w0, w_up, a0, a_up, g_up, k_k, k_a, r_k, gn_w, gn_b, B, S):
    C = RWKV_CHUNK
    nc = S // C
    W = RWKV_WIDTH
    mu_p = jnp.pad(mu, (0, ZF_SLAB_PAD - RWKV_SLAB)).reshape(1, ZF_SLAB_PAD)
    vecs = jnp.stack([w0, a0, k_k, k_a, r_k, gn_w, gn_b, jnp.zeros_like(w0)]).astype(_F32)
    wup = jnp.concatenate([w_up, jnp.zeros((LANES - DECAY_RANK, W), w_up.dtype)], axis=0).astype(_BF16)
    aup = jnp.concatenate([jnp.zeros((DECAY_RANK, W), a_up.dtype), a_up], axis=0).astype(_BF16)
    gup = jnp.concatenate([g_up, jnp.zeros((2 * LANES - GATE_RANK, W), g_up.dtype)], axis=0).astype(_BF16)
    out = pl.pallas_call(
        _rwkv_kernel,
        grid=(nc,),
        in_specs=[pl.BlockSpec((B, C, ZF_SLAB_PAD), lambda c: (0, c, 0)),
                  pl.BlockSpec((1, ZF_SLAB_PAD), lambda c: (0, 0)),
                  pl.BlockSpec((8, W), lambda c: (0, 0)),
                  pl.BlockSpec((LANES, W), lambda c: (0, 0)),
                  pl.BlockSpec((LANES, W), lambda c: (0, 0)),
                  pl.BlockSpec((2 * LANES, W), lambda c: (0, 0))],
        out_specs=pl.BlockSpec((B, C, W), lambda c: (0, c, 0)),
        out_shape=jax.ShapeDtypeStruct((B, S, W), _BF16),
        scratch_shapes=[pltpu.VMEM((B, C + 8, ZF_SLAB_PAD), _F32),
                        pltpu.VMEM((B * HEAD_PAIRS, LANES, LANES), _F32)],
        compiler_params=_params("arbitrary"),
        name="rwkv7_mix",
    )(zf.reshape(B, S, ZF_WIDTH), mu_p, vecs, wup, aup, gup)
    return out.reshape(B * S, W)


def _router_kernel(h_ref, g_ref, wr_ref, xn_ref, route_ref):
    xn = _rms_to_bf16(h_ref[...], g_ref[...])
    xn_ref[...] = xn
    logits = jnp.dot(xn, wr_ref[...], preferred_element_type=_F32)
    lane = lax.broadcasted_iota(jnp.int32, logits.shape, 1).astype(_F32)
    l1 = jnp.where(lane < N_EXPERTS, logits, -jnp.inf)
    m1 = jnp.max(l1, axis=1, keepdims=True)
    i1 = jnp.min(jnp.where(l1 == m1, lane, float(LANES)), axis=1, keepdims=True)
    l2 = jnp.where(lane == i1, -jnp.inf, l1)
    m2 = jnp.max(l2, axis=1, keepdims=True)
    i2 = jnp.min(jnp.where(l2 == m2, lane, float(LANES)), axis=1, keepdims=True)
    e2 = jnp.exp(m2 - m1)
    g1 = 1.0 / (1.0 + e2)
    g2 = e2 / (1.0 + e2)
    route_ref[...] = jnp.where(lane == 0.0, i1, jnp.where(lane == 1.0, i2,
                               jnp.where(lane == 2.0, g1, jnp.where(lane == 3.0, g2, 0.0))))


def moe_route(h, g, router, tm):
    T, D = h.shape
    wr = jnp.pad(router, ((0, 0), (0, LANES - N_EXPERTS))).astype(_BF16)
    return pl.pallas_call(
        _router_kernel,
        grid=(T // tm,),
        in_specs=[pl.BlockSpec((tm, D), lambda i: (i, 0)),
                  pl.BlockSpec((1, D), lambda i: (0, 0)),
                  pl.BlockSpec((D, LANES), lambda i: (0, 0))],
        out_specs=[pl.BlockSpec((tm, D), lambda i: (i, 0)),
                   pl.BlockSpec((tm, LANES), lambda i: (i, 0))],
        out_shape=[jax.ShapeDtypeStruct((T, D), _BF16), jax.ShapeDtypeStruct((T, LANES), _F32)],
        compiler_params=_params("parallel"),
        name="moe_route",
    )(h, g.reshape(1, D), wr)


def _moe_ffn_kernel(te_ref, nv_ref, x_ref, w1_ref, w3_ref, w2_ref, o_ref):
    n = pl.program_id(0)
    f = pl.program_id(1)

    @pl.when(jnp.logical_and(n >= nv_ref[0], f == 0))
    def _():
        o_ref[...] = jnp.zeros(o_ref.shape, o_ref.dtype)

    @pl.when(n < nv_ref[0])
    def _():
        x = x_ref[...]
        a = jnp.dot(x, w1_ref[...], preferred_element_type=_F32)
        b = jnp.dot(x, w3_ref[...], preferred_element_type=_F32)
        mid = (a * _sigmoid(a) * b).astype(_BF16)
        contrib = jnp.dot(mid, w2_ref[...], preferred_element_type=_F32)

        @pl.when(f == 0)
        def _():
            o_ref[...] = contrib

        @pl.when(f > 0)
        def _():
            o_ref[...] += contrib


def moe_grouped_ffn(xs, tile_expert, n_valid, w1, w3, w2):
    R, D = xs.shape
    F = w1.shape[2]
    nf = F // MOE_TF

    def f_idx(n, f, nv):
        return jnp.where(n < nv[0], f, nf - 1)

    return pl.pallas_call(
        _moe_ffn_kernel,
        grid_spec=pltpu.PrefetchScalarGridSpec(
            num_scalar_prefetch=2,
            grid=(R // MOE_TM, nf),
            in_specs=[pl.BlockSpec((MOE_TM, D), lambda n, f, te, nv: (n, 0)),
                      pl.BlockSpec((None, D, MOE_TF), lambda n, f, te, nv: (te[n], 0, f_idx(n, f, nv))),
                      pl.BlockSpec((None, D, MOE_TF), lambda n, f, te, nv: (te[n], 0, f_idx(n, f, nv))),
                      pl.BlockSpec((None, MOE_TF, D), lambda n, f, te, nv: (te[n], f_idx(n, f, nv), 0))],
            out_specs=pl.BlockSpec((MOE_TM, D), lambda n, f, te, nv: (n, 0))),
        out_shape=jax.ShapeDtypeStruct((R, D), _F32),
        compiler_params=_params("parallel", "arbitrary"),
        name="moe_grouped_ffn",
    )(tile_expert, n_valid, xs, w1, w3, w2)


def moe_residual(h, g, router, w1, w3, w2):
    T, D = h.shape
    E = N_EXPERTS
    xn, route = moe_route(h, g, router, tm=512)
    top = route[:, 0:TOP_K_EXPERTS].astype(jnp.int32)
    gates = route[:, TOP_K_EXPERTS:2 * TOP_K_EXPERTS]
    ef = top.T.reshape(TOP_K_EXPERTS * T)
    onehot = (ef[:, None] == jnp.arange(E, dtype=jnp.int32)[None, :]).astype(jnp.int32)
    csum = jnp.cumsum(onehot, axis=0)
    rank = jnp.take_along_axis(csum, ef[:, None], axis=1)[:, 0] - 1
    counts = csum[-1]
    padded = ((counts + MOE_TM - 1) // MOE_TM) * MOE_TM
    pends = jnp.cumsum(padded)
    pstarts = pends - padded
    starts = jnp.cumsum(counts) - counts
    pos = pstarts[ef] + rank
    R = TOP_K_EXPERTS * T + E * MOE_TM
    n_tiles = R // MOE_TM
    tile_expert = jnp.searchsorted(pends, jnp.arange(n_tiles, dtype=jnp.int32) * MOE_TM, side='right')
    tile_expert = jnp.minimum(tile_expert, E - 1).astype(jnp.int32)
    n_valid = (pends[-1:] // MOE_TM).astype(jnp.int32)
    order = jnp.argsort(ef, stable=True).astype(jnp.int32)
    rho = jnp.arange(R, dtype=jnp.int32)
    e_row = tile_expert[rho // MOE_TM]
    off = rho - pstarts[e_row]
    src = order[jnp.clip(starts[e_row] + off, 0, TOP_K_EXPERTS * T - 1)] % T
    xs = jnp.take(xn, src, axis=0)
    ys = moe_grouped_ffn(xs, tile_expert, n_valid, w1, w3, w2)
    mix = sum(gates[:, slot:slot + 1] * jnp.take(ys, pos[slot * T:(slot + 1) * T], axis=0)
              for slot in range(TOP_K_EXPERTS))
    return h + mix


def _pack_w_in(w):
    wq, wk, wv, wqi, wki, wwi, wslab, wga, wgb = _split(w, IN_SIZES)
    D = w.shape[0]
    wb = jnp.concatenate([wq * (ATT_HEAD_DIM ** -0.5 * LOG2_E), wk, wv, wqi, wki, wki], axis=1)
    wf = jnp.concatenate([wslab, jnp.zeros((D, ZF_SLAB_PAD - RWKV_SLAB), w.dtype),
                          wki, wwi, jnp.zeros((D, LANES - IDX_DIM - IDX_HEADS), w.dtype), wga, wgb], axis=1)
    return wb.astype(_BF16), wf.astype(_BF16)


def kernel(x, p, w_in, att_up, rwkv_up, w_out, rel_bias, rwkv_mu, rwkv_w0, rwkv_w_up,
           rwkv_a0, rwkv_a_up, rwkv_g_up, rwkv_k_k, rwkv_k_a, rwkv_r_k, rwkv_gn_w, rwkv_gn_b,
           norm_mix, norm_ffn, norm_ple, ple_proj, ple_gate, ffn_w1, ffn_w3, ffn_w2,
           moe_router, moe_w1, moe_w3, moe_w2, final_norm):
    B, S, D = x.shape
    T = B * S
    depth = w_in.shape[0]
    h = x.reshape(T, D)
    for i in range(depth):
        wb, wf = _pack_w_in(w_in[i])
        zb = norm_matmul(h, norm_mix[i], wb, _BF16, tm=512, tn=ZB_WIDTH)
        zf = norm_matmul(h, norm_mix[i], wf, _F32, tm=512, tn=1536)
        att = dsa_attention(zb, zf, rel_bias, B, S)
        rw = rwkv7_mix(zf, rwkv_mu[i], rwkv_w0[i], rwkv_w_up[i], rwkv_a0[i], rwkv_a_up[i], rwkv_g_up[i],
                       rwkv_k_k[i], rwkv_k_a[i], rwkv_r_k[i], rwkv_gn_w[i], rwkv_gn_b[i], B, S)
        merged = gated_merge(att, rw, zf, att_up[i].astype(_BF16), rwkv_up[i].astype(_BF16), tm=512, tn=512)
        h = matmul_residual(merged, w_out[i].astype(_BF16), h, tm=512, tn=1024)
        if i % 2 == 0:
            j = i // 2
            h = ffn_residual(h, norm_ffn[i], ffn_w1[j].astype(_BF16), ffn_w3[j].astype(_BF16),
                             ffn_w2[j].astype(_BF16), tm=512, tf=512)
        else:
            j = i // 2
            h = moe_residual(h, norm_ffn[i], moe_router[j], cast_bf16(moe_w1[j]), cast_bf16(moe_w3[j]),
                             cast_bf16(moe_w2[j]))
        h = ple_residual(h, norm_ple[i], p[i].reshape(T, PLE_DIM), ple_gate[i].astype(_BF16),
                         ple_proj[i].astype(_BF16), final_norm, final=(i == depth - 1), tm=256)
    return h.reshape(B, S, D)
```

```python
import functools
import math

import jax
import jax.numpy as jnp
import numpy as np
from jax import lax
from jax.experimental import pallas as pl
from jax.experimental.pallas import tpu as pltpu

D_MODEL = 2048
DEPTH = 2
ATT_HEADS = 8
ATT_KV_HEADS = 2
ATT_HEAD_DIM = 128
IDX_HEADS = 8
IDX_DIM = 64
TOPK_MAX = 256
REL_BUCKETS = 32
REL_MAX_DIST = 128
RWKV_HEAD = 64
RWKV_HEADS = 16
DECAY_RANK = 64
ICLR_RANK = 64
GATE_RANK = 160
D_FF = 5632
N_EXPERTS = 8
TOP_K_EXPERTS = 2
D_FF_EXPERT = 7168
PLE_DIM = 256
RMS_EPS = 1e-6
GN_EPS = 64e-5

ATT_WIDTH = ATT_HEADS * ATT_HEAD_DIM
KV_WIDTH = ATT_KV_HEADS * ATT_HEAD_DIM
RWKV_WIDTH = RWKV_HEADS * RWKV_HEAD
RWKV_SLAB = 3 * RWKV_WIDTH + DECAY_RANK + ICLR_RANK + GATE_RANK
IN_SIZES = (ATT_WIDTH, KV_WIDTH, KV_WIDTH, IDX_HEADS * IDX_DIM, IDX_DIM, IDX_HEADS, RWKV_SLAB, D_MODEL, D_MODEL)

LANES = 128
SUBLANES = 8
VMEM_LIMIT = 56 * 1024 * 1024
CAST_BLOCK_BYTES = 8 * 1024 * 1024

IN_PROJ_TM = 512
IN_PROJ_F32_TM = 1024
IN_PROJ_F32_TN = 1536
MERGE_TM, MERGE_TN = 512, 512
OUT_PROJ_TM = 512
FFN_TM, FFN_TF = 512, 512
ROUTE_TM = 512
PLE_TM = 256

Q_BLOCK = 128
KEY_CHUNK = 512
BLOCKS_PER_CHUNK = KEY_CHUNK // Q_BLOCK
BLOCKS_PER_GROUP = 32
MASKED_LOGIT = -1e30
LOG2_E = math.log2(math.e)
GQA_GROUP = ATT_HEADS // ATT_KV_HEADS

ZB_WIDTH = ATT_WIDTH + 2 * KV_WIDTH + IDX_HEADS * IDX_DIM + 2 * IDX_DIM
ZF_SLAB_PAD = 3456
ZF_WIDX = ZF_SLAB_PAD
ZF_GATE_A = ZF_WIDX + LANES
ZF_GATE_B = ZF_GATE_A + D_MODEL
ZF_WIDTH = ZF_GATE_B + D_MODEL

RWKV_CHUNK = 64
HEAD_PAIRS = RWKV_HEADS // 2
MOE_TM = 512
MOE_TF = 1024

_F32 = jnp.float32
_BF16 = jnp.bfloat16
_NT = (((1,), (1,)), ((), ()))
_TN = (((0,), (0,)), ((), ()))


def _split(z, sizes):
    return jnp.split(z, np.cumsum(sizes)[:-1].tolist(), axis=-1)


def _sigmoid(x):
    return 1.0 / (1.0 + jnp.exp(-x))


def _rms_to_bf16(x, g):
    y = x * lax.rsqrt(jnp.mean(x * x, axis=-1, keepdims=True) + RMS_EPS)
    return (y * g).astype(_BF16)


def _params(*sem):
    return pltpu.CompilerParams(dimension_semantics=sem, vmem_limit_bytes=VMEM_LIMIT)


def _norm_mm_kernel(x_ref, g_ref, w_ref, o_ref, xn_ref):
    @pl.when(pl.program_id(1) == 0)
    def _():
        xn_ref[...] = _rms_to_bf16(x_ref[...], g_ref[...])

    o_ref[...] = jnp.dot(xn_ref[...], w_ref[...], preferred_element_type=_F32).astype(o_ref.dtype)


def norm_matmul(x, g, w, out_dtype, tm, tn):
    T, D = x.shape
    N = w.shape[1]
    tm = min(tm, T)
    return pl.pallas_call(
        _norm_mm_kernel,
        grid=(T // tm, N // tn),
        in_specs=[pl.BlockSpec((tm, D), lambda i, j: (i, 0)),
                  pl.BlockSpec((1, D), lambda i, j: (0, 0)),
                  pl.BlockSpec((D, tn), lambda i, j: (0, j))],
        out_specs=pl.BlockSpec((tm, tn), lambda i, j: (i, j)),
        out_shape=jax.ShapeDtypeStruct((T, N), out_dtype),
        scratch_shapes=[pltpu.VMEM((tm, D), _BF16)],
        compiler_params=_params("parallel", "arbitrary"),
        name="norm_matmul",
    )(x, g.reshape(1, D), w)


def _merge_kernel(att_ref, rw_ref, ga_ref, gb_ref, wa_ref, wr_ref, o_ref):
    a = jnp.dot(att_ref[...], wa_ref[...], preferred_element_type=_F32)
    r = jnp.dot(rw_ref[...], wr_ref[...], preferred_element_type=_F32)
    o_ref[...] = (_sigmoid(ga_ref[...]) * a + _sigmoid(gb_ref[...]) * r).astype(o_ref.dtype)


def gated_merge(att, rw, zf, wa, wr, tm, tn):
    T, K = att.shape
    N = wa.shape[1]
    assert ZF_GATE_A % tn == 0 and ZF_GATE_B % tn == 0
    ja, jb = ZF_GATE_A // tn, ZF_GATE_B // tn
    return pl.pallas_call(
        _merge_kernel,
        grid=(T // tm, N // tn),
        in_specs=[pl.BlockSpec((tm, K), lambda i, j: (i, 0)),
                  pl.BlockSpec((tm, K), lambda i, j: (i, 0)),
                  pl.BlockSpec((tm, tn), lambda i, j: (i, ja + j)),
                  pl.BlockSpec((tm, tn), lambda i, j: (i, jb + j)),
                  pl.BlockSpec((K, tn), lambda i, j: (0, j)),
                  pl.BlockSpec((K, tn), lambda i, j: (0, j))],
        out_specs=pl.BlockSpec((tm, tn), lambda i, j: (i, j)),
        out_shape=jax.ShapeDtypeStruct((T, N), _BF16),
        compiler_params=_params("parallel", "arbitrary"),
        name="gated_merge",
    )(att, rw, zf, zf, wa, wr)


def _mm_res_kernel(x_ref, w_ref, h_ref, o_ref):
    o_ref[...] = h_ref[...] + jnp.dot(x_ref[...], w_ref[...], preferred_element_type=_F32)


def matmul_residual(x, w, h, tm, tn):
    T, K = x.shape
    N = w.shape[1]
    return pl.pallas_call(
        _mm_res_kernel,
        grid=(T // tm, N // tn),
        in_specs=[pl.BlockSpec((tm, K), lambda i, j: (i, 0)),
                  pl.BlockSpec((K, tn), lambda i, j: (0, j)),
                  pl.BlockSpec((tm, tn), lambda i, j: (i, j))],
        out_specs=pl.BlockSpec((tm, tn), lambda i, j: (i, j)),
        out_shape=jax.ShapeDtypeStruct((T, N), _F32),
        compiler_params=_params("parallel", "arbitrary"),
        name="matmul_residual",
    )(x, w, h)


def _ffn_kernel(h_ref, g_ref, w1_ref, w3_ref, w2_ref, o_ref, xn_ref):
    f = pl.program_id(1)

    @pl.when(f == 0)
    def _():
        xn_ref[...] = _rms_to_bf16(h_ref[...], g_ref[...])
        o_ref[...] = h_ref[...]

    xn = xn_ref[...]
    a = jnp.dot(xn, w1_ref[...], preferred_element_type=_F32)
    b = jnp.dot(xn, w3_ref[...], preferred_element_type=_F32)
    mid = (a * _sigmoid(a) * b).astype(_BF16)
    o_ref[...] += jnp.dot(mid, w2_ref[...], preferred_element_type=_F32)


def ffn_residual(h, g, w1, w3, w2, tm, tf):
    T, D = h.shape
    F = w1.shape[1]
    return pl.pallas_call(
        _ffn_kernel,
        grid=(T // tm, F // tf),
        in_specs=[pl.BlockSpec((tm, D), lambda i, f: (i, 0)),
                  pl.BlockSpec((1, D), lambda i, f: (0, 0)),
                  pl.BlockSpec((D, tf), lambda i, f: (0, f)),
                  pl.BlockSpec((D, tf), lambda i, f: (0, f)),
                  pl.BlockSpec((tf, D), lambda i, f: (f, 0))],
        out_specs=pl.BlockSpec((tm, D), lambda i, f: (i, 0)),
        out_shape=jax.ShapeDtypeStruct((T, D), _F32),
        scratch_shapes=[pltpu.VMEM((tm, D), _BF16)],
        compiler_params=_params("parallel", "arbitrary"),
        name="ffn_residual",
    )(h, g.reshape(1, D), w1, w3, w2)


def _ple_kernel(h_ref, g_ref, p_ref, wg_ref, wp_ref, fg_ref, o_ref, *, final):
    h = h_ref[...]
    xn = _rms_to_bf16(h, g_ref[...])
    gate = _sigmoid(jnp.dot(xn, wg_ref[...], preferred_element_type=_F32))
    pp = jnp.dot(p_ref[...].astype(_BF16), wp_ref[...], preferred_element_type=_F32)
    out = h + gate * pp
    if final:
        out = out * lax.rsqrt(jnp.mean(out * out, axis=-1, keepdims=True) + RMS_EPS) * fg_ref[...]
    o_ref[...] = out


def ple_residual(h, g, p, wg, wp, final_g, final, tm):
    T, D = h.shape
    P = p.shape[1]
    return pl.pallas_call(
        functools.partial(_ple_kernel, final=final),
        grid=(T // tm,),
        in_specs=[pl.BlockSpec((tm, D), lambda i: (i, 0)),
                  pl.BlockSpec((1, D), lambda i: (0, 0)),
                  pl.BlockSpec((tm, P), lambda i: (i, 0)),
                  pl.BlockSpec((D, D), lambda i: (0, 0)),
                  pl.BlockSpec((P, D), lambda i: (0, 0)),
                  pl.BlockSpec((1, D), lambda i: (0, 0))],
        out_specs=pl.BlockSpec((tm, D), lambda i: (i, 0)),
        out_shape=jax.ShapeDtypeStruct((T, D), _F32),
        compiler_params=_params("parallel"),
        name="ple_residual",
    )(h, g.reshape(1, D), p, wg, wp, final_g.reshape(1, D))


def _cast_kernel(x_ref, o_ref):
    o_ref[...] = x_ref[...].astype(o_ref.dtype)


def cast_bf16(w):
    E, R, N = w.shape
    rows = R
    while rows * N * 4 > CAST_BLOCK_BYTES and rows % 2 == 0:
        rows //= 2
    return pl.pallas_call(
        _cast_kernel,
        grid=(E, R // rows),
        in_specs=[pl.BlockSpec((None, rows, N), lambda e, i: (e, i, 0))],
        out_specs=pl.BlockSpec((None, rows, N), lambda e, i: (e, i, 0)),
        out_shape=jax.ShapeDtypeStruct(w.shape, _BF16),
        compiler_params=_params("parallel", "parallel"),
        name="cast_bf16",
    )(w)


def rel_bucket(dist):
    max_exact = REL_BUCKETS // 2
    dist = jnp.maximum(dist, 0)
    d_f = jnp.maximum(dist, 1).astype(_F32)
    large = max_exact + (jnp.log(d_f / max_exact) / math.log(REL_MAX_DIST / max_exact)
                         * (REL_BUCKETS - max_exact)).astype(jnp.int32)
    large = jnp.minimum(large, REL_BUCKETS - 1)
    return jnp.where(dist < max_exact, dist, large)


def _sortable(x):
    b = lax.bitcast_convert_type(x, jnp.int32)
    return b ^ ((b >> 31) & jnp.int32(0x7FFFFFFF))


_INT_MIN = -2 ** 31
_KEY_NEG_INF = 0x807FFFFF - 2 ** 32


def _dsa_kernel(q_ref, qi_ref, wi_ref, k_ref, v_ref, ki_ref, bprev_ref, bdiag_ref, bfar_ref, o_ref,
                keys_ref, planes_ref, active_ref, qs_ref, qim_ref, m_ref, acc_ref, *, n_sel, idx_bits):
    n_grp = planes_ref.shape[0]
    qb = pl.program_id(1)
    t0 = qb * Q_BLOCK
    n_chunks = qb // BLOCKS_PER_CHUNK + 1
    t_idx = t0 + lax.broadcasted_iota(jnp.int32, (Q_BLOCK, 1), 0)
    lane_blk = lax.broadcasted_iota(jnp.int32, (1, Q_BLOCK), 1)
    lane_chunk = lax.broadcasted_iota(jnp.int32, (1, KEY_CHUNK), 1)

    for hh in range(ATT_HEADS):
        g, h = divmod(hh, GQA_GROUP)
        qs_ref[g, h * Q_BLOCK:(h + 1) * Q_BLOCK, :] = q_ref[:, hh * ATT_HEAD_DIM:(hh + 1) * ATT_HEAD_DIM]
    low = lax.broadcasted_iota(jnp.int32, (Q_BLOCK, LANES), 1) < IDX_DIM
    for hh in range(IDX_HEADS):
        tile = qi_ref[:, (hh // 2) * LANES:(hh // 2 + 1) * LANES]
        keep = low if hh % 2 == 0 else jnp.logical_not(low)
        qim_ref[hh] = jnp.where(keep, tile, jnp.zeros_like(tile))
    m_ref[...] = jnp.full(m_ref.shape, MASKED_LOGIT, _F32)

    @pl.when(qb == 0)
    def _():
        keys_ref[BLOCKS_PER_CHUNK:] = jnp.zeros((keys_ref.shape[0] - BLOCKS_PER_CHUNK, Q_BLOCK, Q_BLOCK), jnp.int32)
        planes_ref[...] = jnp.zeros(planes_ref.shape, jnp.int32)
    acc_ref[...] = jnp.zeros(acc_ref.shape, _F32)

    w_all = wi_ref[...]
    idx_scale = (IDX_HEADS ** -0.5) * (IDX_DIM ** -0.5)

    def score_body(c, carry):
        ks = pl.multiple_of(c * KEY_CHUNK, KEY_CHUNK)
        kic = ki_ref[pl.ds(ks, KEY_CHUNK), :]
        d = lax.dot_general(qim_ref[...].reshape(IDX_HEADS * Q_BLOCK, LANES), kic, _NT,
                            preferred_element_type=_F32)
        sc = jnp.zeros((Q_BLOCK, KEY_CHUNK), _F32)
        for hh in range(IDX_HEADS):
            sc = sc + w_all[:, IDX_DIM + hh:IDX_DIM + hh + 1] * jnp.maximum(d[hh * Q_BLOCK:(hh + 1) * Q_BLOCK], 0.0)
        sc = sc * idx_scale
        sc = jnp.where(sc == 0.0, 0.0, sc)
        key = jnp.where(ks + lane_chunk <= t_idx, _sortable(sc), _KEY_NEG_INF)
        for j in range(BLOCKS_PER_CHUNK):
            keys_ref[c * BLOCKS_PER_CHUNK + j] = key[:, j * Q_BLOCK:(j + 1) * Q_BLOCK]
        return carry

    lax.fori_loop(0, n_chunks, score_body, 0)

    def count(pred):
        def body(c, acc):
            for j in range(BLOCKS_PER_CHUNK):
                blk = c * BLOCKS_PER_CHUNK + j
                acc = acc + jnp.where(pred(keys_ref[blk], blk * Q_BLOCK + lane_blk), 1.0, 0.0)
            return acc

        acc = lax.fori_loop(0, n_chunks, body, jnp.zeros((Q_BLOCK, Q_BLOCK), _F32))
        return jnp.sum(acc, axis=1, keepdims=True)

    n_blocks = n_chunks * BLOCKS_PER_CHUNK
    n_groups = (n_blocks + BLOCKS_PER_GROUP - 1) // BLOCKS_PER_GROUP

    def transpose_group(g, carry):
        def rows(rg, carry_rows):
            r0 = pl.multiple_of(rg * SUBLANES, SUBLANES)
            x = [keys_ref[g * BLOCKS_PER_GROUP + j, pl.ds(r0, SUBLANES), :] for j in range(BLOCKS_PER_GROUP)]
            j, m = 16, 0x0000FFFF
            while j:
                k = 0
                while k < BLOCKS_PER_GROUP:
                    t = (x[k] ^ lax.shift_right_logical(x[k + j], jnp.int32(j))) & jnp.int32(m)
                    x[k] = x[k] ^ t
                    x[k + j] = x[k + j] ^ lax.shift_left(t, jnp.int32(j))
                    k = (k + j + 1) & ~j
                j >>= 1
                m ^= m << j
            planes_ref[g, 0, pl.ds(r0, SUBLANES), :] = ~x[0]
            for i in range(1, 32):
                planes_ref[g, i, pl.ds(r0, SUBLANES), :] = x[i]
            return carry_rows

        lax.fori_loop(0, Q_BLOCK // SUBLANES, rows, 0)
        n_live = jnp.minimum(n_blocks - g * BLOCKS_PER_GROUP, BLOCKS_PER_GROUP)
        active_ref[g] = lax.shift_left(jnp.full((Q_BLOCK, Q_BLOCK), -1, jnp.int32), BLOCKS_PER_GROUP - n_live)
        return carry

    active_ref[...] = jnp.zeros(active_ref.shape, jnp.int32)
    lax.fori_loop(0, n_groups, transpose_group, 0)

    def lane_total(words):
        return jnp.sum(sum(lax.population_count(w) for w in words).astype(_F32), axis=1, keepdims=True)

    def split_active(g, i):
        act = active_ref[g]
        hi = act & planes_ref[g, 2 * i]
        lo = act ^ hi
        hi1 = hi & planes_ref[g, 2 * i + 1]
        lo1 = lo & planes_ref[g, 2 * i + 1]
        return hi1, hi ^ hi1, lo1, lo ^ lo1

    def radix_pass(i, carry):
        prefix, above = carry
        parts = [split_active(g, i) for g in range(n_grp)]
        c11, c10, c01 = (lane_total([parts[g][q] for g in range(n_grp)]) for q in range(3))
        upto10 = above + c11 + c10
        t11 = above + c11 >= n_sel
        t10 = upto10 >= n_sel
        t01 = upto10 + c01 >= n_sel
        for g in range(n_grp):
            s11, s10, s01, s00 = split_active(g, i)
            active_ref[g] = jnp.where(t11, s11, jnp.where(t10, s10, jnp.where(t01, s01, s00)))
        bit1 = jnp.where(t10, lax.shift_left(jnp.int32(1), 31 - 2 * i), 0)
        bit0 = jnp.where(jnp.logical_or(t11, jnp.logical_and(jnp.logical_not(t10), t01)),
                         lax.shift_left(jnp.int32(1), 30 - 2 * i), 0)
        above = jnp.where(t11, above, jnp.where(t10, above + c11, jnp.where(t01, upto10, upto10 + c01)))
        return prefix | bit1 | bit0, above

    prefix, n_gt = lax.fori_loop(0, 16, radix_pass, (jnp.zeros((Q_BLOCK, 1), jnp.int32),
                                                     jnp.zeros((Q_BLOCK, 1), _F32)))
    tau = prefix ^ jnp.int32(_INT_MIN)
    n_eq = lane_total([active_ref[g] for g in range(n_grp)])
    tied = jnp.logical_and(n_gt + n_eq > n_sel, tau > _KEY_NEG_INF)

    def tie_search():
        need = n_sel - count(lambda key, sidx: key > tau)

        def index_bit(i, x):
            cand = x | lax.shift_left(jnp.int32(1), idx_bits - 1 - i)
            cnt = count(lambda key, sidx: jnp.logical_and(key == tau, sidx < cand))
            return jnp.where(cnt < need, cand, x)

        return lax.fori_loop(0, idx_bits, index_bit, jnp.zeros((Q_BLOCK, 1), jnp.int32))

    last_tie = lax.cond(jnp.max(jnp.where(tied, 1.0, 0.0)) > 0.0, tie_search,
                        lambda: jnp.full((Q_BLOCK, 1), 2 ** idx_bits, jnp.int32))

    def attend(key, k_tile, v_tile, s0, lane, bias_ref, enable=None, const_bias=False):
        width = key.shape[1]
        sidx = s0 + lane
        sel = jnp.logical_or(key > tau, jnp.logical_and(key == tau, sidx <= last_tie))
        sel = jnp.logical_and(sel, key > _KEY_NEG_INF)
        if enable is not None:
            sel = jnp.logical_and(sel, jnp.zeros_like(sidx) + enable > 0)
        ones = jnp.ones((width, ATT_HEAD_DIM), _BF16)
        groups = range(ATT_KV_HEADS)
        cols = [slice(g * ATT_HEAD_DIM, (g + 1) * ATT_HEAD_DIM) for g in groups]

        def logits(g):
            return lax.dot_general(qs_ref[g], k_tile[:, cols[g]], _NT,
                                   preferred_element_type=_F32).reshape(GQA_GROUP, Q_BLOCK, width)

        def softmax(g, s):
            bias = bias_ref[g * GQA_GROUP:(g + 1) * GQA_GROUP]
            m_old = m_ref[g]
            if const_bias:
                sh = jnp.where(sel[None], s, MASKED_LOGIT)
                m_new = jnp.maximum(m_old, jnp.max(sh, axis=2, keepdims=True) + bias)
                shift = m_new - bias
            else:
                sh = jnp.where(sel[None], s + bias, MASKED_LOGIT)
                m_new = jnp.maximum(m_old, jnp.max(sh, axis=2, keepdims=True))
                shift = m_new
            m_ref[g] = m_new
            return (jnp.exp2(sh - shift).reshape(GQA_GROUP * Q_BLOCK, width).astype(_BF16),
                    jnp.exp2(m_old - m_new).reshape(GQA_GROUP * Q_BLOCK, 1))

        def accumulate(g, p, alpha):
            pv = jnp.dot(p, jnp.concatenate([v_tile[:, cols[g]], ones], axis=1), preferred_element_type=_F32)
            acc_ref[g] = acc_ref[g] * alpha + pv

        if width <= Q_BLOCK:
            s = [logits(g) for g in groups]
            pa = [softmax(g, s[g]) for g in groups]
            for g in groups:
                accumulate(g, *pa[g])
        else:
            for g in groups:
                accumulate(g, *softmax(g, logits(g)))

    n_far = jnp.maximum(qb - 1, 0) // BLOCKS_PER_CHUNK

    def far_body(c, carry):
        ks = pl.multiple_of(c * KEY_CHUNK, KEY_CHUNK)
        key = jnp.concatenate([keys_ref[c * BLOCKS_PER_CHUNK + j] for j in range(BLOCKS_PER_CHUNK)], axis=1)
        attend(key, k_ref[pl.ds(ks, KEY_CHUNK), :], v_ref[pl.ds(ks, KEY_CHUNK), :], ks, lane_chunk, bfar_ref,
               const_bias=True)
        return carry

    lax.fori_loop(0, n_far, far_body, 0)

    def block_step(j, bias_ref, enable=None, const_bias=False):
        ks = pl.multiple_of(j * Q_BLOCK, Q_BLOCK)
        attend(keys_ref[j], k_ref[pl.ds(ks, Q_BLOCK), :], v_ref[pl.ds(ks, Q_BLOCK), :], ks, lane_blk, bias_ref,
               enable, const_bias)

    def tail_body(j, carry):
        block_step(j, bfar_ref, const_bias=True)
        return carry

    lax.fori_loop(n_far * BLOCKS_PER_CHUNK, qb - 1, tail_body, 0)
    block_step(jnp.maximum(qb - 1, 0), bprev_ref, enable=jnp.minimum(qb, 1))
    block_step(qb, bdiag_ref)

    for hh in range(ATT_HEADS):
        g, h = divmod(hh, GQA_GROUP)
        rows = slice(h * Q_BLOCK, (h + 1) * Q_BLOCK)
        o_ref[:, hh * ATT_HEAD_DIM:(hh + 1) * ATT_HEAD_DIM] = (
            acc_ref[g, rows, 0:ATT_HEAD_DIM] / acc_ref[g, rows, ATT_HEAD_DIM:2 * ATT_HEAD_DIM]).astype(o_ref.dtype)


def dsa_attention(zb, zf, rel_bias, B, S):
    nq = S // Q_BLOCK
    n_grp = -(-nq // BLOCKS_PER_GROUP)
    n_sel = min(TOPK_MAX, S // 4)
    tl = jnp.arange(Q_BLOCK, dtype=jnp.int32)[:, None]
    sr = jnp.arange(Q_BLOCK, dtype=jnp.int32)[None, :]
    rel_bias = rel_bias.astype(_F32) * LOG2_E
    bprev = rel_bias[rel_bucket(tl + Q_BLOCK - sr)].transpose(2, 0, 1)
    bdiag = rel_bias[rel_bucket(tl - sr)].transpose(2, 0, 1)
    bfar = rel_bias[rel_bucket(jnp.int32(REL_MAX_DIST + 1))].reshape(ATT_HEADS, 1, 1)
    kernel = functools.partial(_dsa_kernel, n_sel=float(n_sel), idx_bits=int(S - 1).bit_length())
    c_k = ATT_WIDTH // KV_WIDTH
    c_qi = (ATT_WIDTH + 2 * KV_WIDTH) // (IDX_HEADS * IDX_DIM)
    c_ki = (ATT_WIDTH + 2 * KV_WIDTH + IDX_HEADS * IDX_DIM) // LANES
    return pl.pallas_call(
        kernel,
        grid=(B, nq),
        in_specs=[pl.BlockSpec((Q_BLOCK, ATT_WIDTH), lambda b, i: (b * nq + i, 0)),
                  pl.BlockSpec((Q_BLOCK, IDX_HEADS * IDX_DIM), lambda b, i: (b * nq + i, c_qi)),
                  pl.BlockSpec((Q_BLOCK, LANES), lambda b, i: (b * nq + i, ZF_WIDX // LANES)),
                  pl.BlockSpec((S, KV_WIDTH), lambda b, i: (b, c_k)),
                  pl.BlockSpec((S, KV_WIDTH), lambda b, i: (b, c_k + 1)),
                  pl.BlockSpec((S, LANES), lambda b, i: (b, c_ki)),
                  pl.BlockSpec((ATT_HEADS, Q_BLOCK, Q_BLOCK), lambda b, i: (0, 0, 0)),
                  pl.BlockSpec((ATT_HEADS, Q_BLOCK, Q_BLOCK), lambda b, i: (0, 0, 0)),
                  pl.BlockSpec((ATT_HEADS, 1, 1), lambda b, i: (0, 0, 0))],
        out_specs=pl.BlockSpec((Q_BLOCK, ATT_WIDTH), lambda b, i: (b * nq + i, 0)),
        out_shape=jax.ShapeDtypeStruct((B * S, ATT_WIDTH), _BF16),
        scratch_shapes=[pltpu.VMEM((n_grp * BLOCKS_PER_GROUP, Q_BLOCK, Q_BLOCK), jnp.int32),
                        pltpu.VMEM((n_grp, 32, Q_BLOCK, Q_BLOCK), jnp.int32),
                        pltpu.VMEM((n_grp, Q_BLOCK, Q_BLOCK), jnp.int32),
                        pltpu.VMEM((ATT_KV_HEADS, GQA_GROUP * Q_BLOCK, ATT_HEAD_DIM), _BF16),
                        pltpu.VMEM((IDX_HEADS, Q_BLOCK, LANES), _BF16),
                        pltpu.VMEM((ATT_KV_HEADS, GQA_GROUP, Q_BLOCK, 1), _F32),
                        pltpu.VMEM((ATT_KV_HEADS, GQA_GROUP * Q_BLOCK, 2 * ATT_HEAD_DIM), _F32)],
        compiler_params=_params("parallel", "arbitrary"),
        name="dsa_attention",
    )(zb, zb, zf, zb, zb, zb, bprev, bdiag, bfar)


def _split_dot(x, w):
    hi = x.astype(_BF16)
    lo = (x - hi.astype(_F32)).astype(_BF16)
    return jnp.dot(hi, w, preferred_element_type=_F32) + jnp.dot(lo, w, preferred_element_type=_F32)


def _split_dot_left(w, x):
    hi = x.astype(_BF16)
    lo = (x - hi.astype(_F32)).astype(_BF16)
    return jnp.dot(w, hi, preferred_element_type=_F32) + jnp.dot(w, lo, preferred_element_type=_F32)


def _rwkv_kernel(slab_ref, mu_ref, vec_ref, wup_ref, aup_ref, gup_ref, o_ref, xs_ref, state_ref):
    C = RWKV_CHUNK
    W = RWKV_WIDTH
    n_batch = slab_ref.shape[0]
    c = pl.program_id(0)

    @pl.when(c == 0)
    def _():
        xs_ref[:, 0:SUBLANES, :] = jnp.zeros((n_batch, SUBLANES, ZF_SLAB_PAD), _F32)
        state_ref[...] = jnp.zeros(state_ref.shape, _F32)

    w0, a0, k_k, k_a, r_k, gn_w, gn_b = (vec_ref[i:i + 1, :] for i in range(7))
    ti = lax.broadcasted_iota(jnp.int32, (C, C), 0)
    tj = lax.broadcasted_iota(jnp.int32, (C, C), 1)
    tri = jnp.where(tj <= ti, 1.0, 0.0).astype(_BF16)

    def prepare(bi):
        slab = slab_ref[bi]
        xs_ref[bi, SUBLANES:SUBLANES + C, :] = slab
        shifted = xs_ref[bi, SUBLANES - 1:SUBLANES - 1 + C, :]
        xs = slab + (shifted - slab) * mu_ref[...]
        xs_ref[bi, 0:SUBLANES, :] = slab[C - SUBLANES:C, :]

        r = xs[:, 0:W]
        k = xs[:, W:2 * W]
        lora = xs[:, 3 * W:3 * W + LANES]
        xg = xs[:, 3 * W + LANES:3 * W + 3 * LANES]
        d = w0 + jnp.dot(jnp.tanh(lora).astype(_BF16), wup_ref[...], preferred_element_type=_F32)
        logw = -math.exp(-0.5) * _sigmoid(d)
        a = _sigmoid(a0 + jnp.dot(lora.astype(_BF16), aup_ref[...], preferred_element_type=_F32))
        kt = k * (1.0 + (a - 1.0) * k_a)
        cum = _split_dot_left(tri, logw)
        cum_end = cum[C - 1:C, :]
        return dict(r=r, v=xs[:, 2 * W:3 * W], a=a, kt=kt, kk_raw=k * k_k, bonus_raw=r * kt * r_k,
                    g=jnp.dot(_sigmoid(xg).astype(_BF16), gup_ref[...], preferred_element_type=_F32),
                    g_incl=jnp.exp(cum), g_excl=jnp.exp(cum - logw), g_inv=jnp.exp(-cum),
                    g_end=jnp.exp(cum_end - cum), g_total=jnp.exp(cum_end))

    prepared = [prepare(bi) for bi in range(n_batch)]

    lo = lax.broadcasted_iota(jnp.int32, (C, LANES), 1) < RWKV_HEAD
    ri = lax.broadcasted_iota(jnp.int32, (LANES, LANES), 0)
    ci = lax.broadcasted_iota(jnp.int32, (LANES, LANES), 1)
    same_head = (ri // RWKV_HEAD) == (ci // RWKV_HEAD)
    ones_bd = jnp.where(same_head, 1.0, 0.0).astype(_BF16)
    eye = ri == ci
    same_blk = (ri // C) == (ci // C)
    strict = jnp.logical_and(same_blk, ci < ri)
    incl = jnp.logical_and(same_blk, ci <= ri)

    def stack(x):
        return jnp.concatenate([jnp.where(lo, x, 0.0), jnp.where(lo, 0.0, x)], axis=0)

    def dot(x, y):
        return jnp.dot(x.astype(_BF16), y.astype(_BF16), preferred_element_type=_F32)

    pairs = range(n_batch * HEAD_PAIRS)
    sls = [slice((n % HEAD_PAIRS) * LANES, (n % HEAD_PAIRS + 1) * LANES) for n in pairs]

    def col(name, n):
        return prepared[n // HEAD_PAIRS][name][:, sls[n]]

    h_bd = [state_ref[p] for p in pairs]
    kk = [col('kk_raw', p) * lax.rsqrt(jnp.maximum(_split_dot(col('kk_raw', p) * col('kk_raw', p), ones_bd), 1e-24))
          for p in pairs]
    b = [kk[p] * col('a', p) for p in pairs]
    lhs4 = [jnp.concatenate([stack(kk[p] * col('g_excl', p)), stack(col('r', p) * col('g_incl', p))],
                            axis=0).astype(_BF16) for p in pairs]
    rhs4 = []
    for p in pairs:
        bi = b[p] * col('g_inv', p)
        ki = col('kt', p) * col('g_inv', p)
        rhs4.append(jnp.concatenate([bi, bi, ki, ki], axis=0).astype(_BF16))
    gm = [lax.dot_general(lhs4[p], rhs4[p], _NT, preferred_element_type=_F32) for p in pairs]
    a_bd = [jnp.where(strict, gm[p][:2 * C, :2 * C], 0.0) for p in pairs]
    bp_bd = [jnp.concatenate([jnp.where(strict, gm[p][:2 * C, 2 * C:], 0.0),
                              jnp.where(incl, gm[p][2 * C:, 2 * C:], 0.0)], axis=0) for p in pairs]
    pb_bd = [jnp.where(incl, gm[p][2 * C:, :2 * C], 0.0) for p in pairs]

    inv = [jnp.where(eye, 1.0, 0.0) - jnp.where((ri // 2) == (ci // 2), a_bd[p], 0.0) for p in pairs]
    s = 2
    while s < C:
        off = jnp.logical_and((ri // (2 * s)) == (ci // (2 * s)), (ri // s) != (ci // s))
        tm = [dot(inv[p], jnp.where(off, a_bd[p], 0.0)) for p in pairs]
        inv = [inv[p] - dot(tm[p], inv[p]) for p in pairs]
        s *= 2

    v_st = [stack(col('v', p)) for p in pairs]
    hw = [dot(lhs4[p], h_bd[p]) for p in pairs]
    bv = [dot(bp_bd[p], v_st[p]) for p in pairs]
    u_st = [dot(inv[p], hw[p][:2 * C] + bv[p][:2 * C]) for p in pairs]
    y_st = [hw[p][2 * C:] + bv[p][2 * C:] - dot(pb_bd[p], u_st[p]) for p in pairs]
    y = [y_st[p][:C] + y_st[p][C:] for p in pairs]
    u = [u_st[p][:C] + u_st[p][C:] for p in pairs]

    for p in pairs:
        m_t = jnp.concatenate([col('kt', p) * col('g_end', p), -(b[p] * col('g_end', p))], axis=0).astype(_BF16)
        n_t = jnp.concatenate([col('v', p), u[p]], axis=0).astype(_BF16)
        dh = lax.dot_general(m_t, n_t, _TN, preferred_element_type=_F32)
        gt_col = jnp.sum(jnp.where(eye, jnp.broadcast_to(col('g_total', p), (LANES, LANES)), 0.0),
                         axis=1, keepdims=True)
        state_ref[p] = jnp.where(same_head, gt_col * h_bd[p] + dh, 0.0)

    mean = [_split_dot(y[p], ones_bd) * (1.0 / RWKV_HEAD) for p in pairs]
    yc = [y[p] - mean[p] for p in pairs]
    var = [_split_dot(yc[p] * yc[p], ones_bd) * (1.0 / RWKV_HEAD) for p in pairs]
    for p in pairs:
        sl = sls[p]
        yn = yc[p] * lax.rsqrt(var[p] + GN_EPS) * gn_w[:, sl] + gn_b[:, sl]
        bonus = _split_dot(col('bonus_raw', p), ones_bd) * col('v', p)
        o_ref[p // HEAD_PAIRS, :, sl] = ((yn + bonus) * col('g', p)).astype(o_ref.dtype)


def rwkv7_mix(zf, mu, w0, w_up, a0, a_up, g_up, k_k, k_a, r_k, gn_w, gn_b, B, S):
    C = RWKV_CHUNK
    nc = S // C
    W = RWKV_WIDTH
    mu_p = jnp.pad(mu, (0, ZF_SLAB_PAD - RWKV_SLAB)).reshape(1, ZF_SLAB_PAD)
    vecs = jnp.stack([w0, a0, k_k, k_a, r_k, gn_w, gn_b, jnp.zeros_like(w0)]).astype(_F32)
    wup = jnp.concatenate([w_up, jnp.zeros((LANES - DECAY_RANK, W), w_up.dtype)], axis=0).astype(_BF16)
    aup = jnp.concatenate([jnp.zeros((DECAY_RANK, W), a_up.dtype), a_up], axis=0).astype(_BF16)
    gup = jnp.concatenate([g_up, jnp.zeros((2 * LANES - GATE_RANK, W), g_up.dtype)], axis=0).astype(_BF16)
    out = pl.pallas_call(
        _rwkv_kernel,
        grid=(nc,),
        in_specs=[pl.BlockSpec((B, C, ZF_SLAB_PAD), lambda c: (0, c, 0)),
                  pl.BlockSpec((1, ZF_SLAB_PAD), lambda c: (0, 0)),
                  pl.BlockSpec((SUBLANES, W), lambda c: (0, 0)),
                  pl.BlockSpec((LANES, W), lambda c: (0, 0)),
                  pl.BlockSpec((LANES, W), lambda c: (0, 0)),
                  pl.BlockSpec((2 * LANES, W), lambda c: (0, 0))],
        out_specs=pl.BlockSpec((B, C, W), lambda c: (0, c, 0)),
        out_shape=jax.ShapeDtypeStruct((B, S, W), _BF16),
        scratch_shapes=[pltpu.VMEM((B, C + SUBLANES, ZF_SLAB_PAD), _F32),
                        pltpu.VMEM((B * HEAD_PAIRS, LANES, LANES), _F32)],
        compiler_params=_params("arbitrary"),
        name="rwkv7_mix",
    )(zf.reshape(B, S, ZF_WIDTH), mu_p, vecs, wup, aup, gup)
    return out.reshape(B * S, W)


def _router_kernel(h_ref, g_ref, wr_ref, xn_ref, route_ref):
    xn = _rms_to_bf16(h_ref[...], g_ref[...])
    xn_ref[...] = xn
    logits = jnp.dot(xn, wr_ref[...], preferred_element_type=_F32)
    lane = lax.broadcasted_iota(jnp.int32, logits.shape, 1).astype(_F32)
    l1 = jnp.where(lane < N_EXPERTS, logits, -jnp.inf)
    m1 = jnp.max(l1, axis=1, keepdims=True)
    i1 = jnp.min(jnp.where(l1 == m1, lane, float(LANES)), axis=1, keepdims=True)
    l2 = jnp.where(lane == i1, -jnp.inf, l1)
    m2 = jnp.max(l2, axis=1, keepdims=True)
    i2 = jnp.min(jnp.where(l2 == m2, lane, float(LANES)), axis=1, keepdims=True)
    e2 = jnp.exp(m2 - m1)
    g1 = 1.0 / (1.0 + e2)
    g2 = e2 / (1.0 + e2)
    route_ref[...] = jnp.where(lane == 0.0, i1, jnp.where(lane == 1.0, i2,
                               jnp.where(lane == 2.0, g1, jnp.where(lane == 3.0, g2, 0.0))))


def moe_route(h, g, router, tm):
    T, D = h.shape
    wr = jnp.pad(router, ((0, 0), (0, LANES - N_EXPERTS))).astype(_BF16)
    return pl.pallas_call(
        _router_kernel,
        grid=(T // tm,),
        in_specs=[pl.BlockSpec((tm, D), lambda i: (i, 0)),
                  pl.BlockSpec((1, D), lambda i: (0, 0)),
                  pl.BlockSpec((D, LANES), lambda i: (0, 0))],
        out_specs=[pl.BlockSpec((tm, D), lambda i: (i, 0)),
                   pl.BlockSpec((tm, LANES), lambda i: (i, 0))],
        out_shape=[jax.ShapeDtypeStruct((T, D), _BF16), jax.ShapeDtypeStruct((T, LANES), _F32)],
        compiler_params=_params("parallel"),
        name="moe_route",
    )(h, g.reshape(1, D), wr)


def _moe_ffn_kernel(te_ref, nv_ref, x_ref, w1_ref, w3_ref, w2_ref, o_ref):
    n = pl.program_id(0)
    f = pl.program_id(1)

    @pl.when(jnp.logical_and(n >= nv_ref[0], f == 0))
    def _():
        o_ref[...] = jnp.zeros(o_ref.shape, o_ref.dtype)

    @pl.when(n < nv_ref[0])
    def _():
        x = x_ref[...]
        a = jnp.dot(x, w1_ref[...], preferred_element_type=_F32)
        b = jnp.dot(x, w3_ref[...], preferred_element_type=_F32)
        mid = (a * _sigmoid(a) * b).astype(_BF16)
        contrib = jnp.dot(mid, w2_ref[...], preferred_element_type=_F32)

        @pl.when(f == 0)
        def _():
            o_ref[...] = contrib

        @pl.when(f > 0)
        def _():
            o_ref[...] += contrib


def moe_grouped_ffn(xs, tile_expert, n_valid, w1, w3, w2):
    R, D = xs.shape
    F = w1.shape[2]
    nf = F // MOE_TF

    def f_idx(n, f, nv):
        return jnp.where(n < nv[0], f, nf - 1)

    return pl.pallas_call(
        _moe_ffn_kernel,
        grid_spec=pltpu.PrefetchScalarGridSpec(
            num_scalar_prefetch=2,
            grid=(R // MOE_TM, nf),
            in_specs=[pl.BlockSpec((MOE_TM, D), lambda n, f, te, nv: (n, 0)),
                      pl.BlockSpec((None, D, MOE_TF), lambda n, f, te, nv: (te[n], 0, f_idx(n, f, nv))),
                      pl.BlockSpec((None, D, MOE_TF), lambda n, f, te, nv: (te[n], 0, f_idx(n, f, nv))),
                      pl.BlockSpec((None, MOE_TF, D), lambda n, f, te, nv: (te[n], f_idx(n, f, nv), 0))],
            out_specs=pl.BlockSpec((MOE_TM, D), lambda n, f, te, nv: (n, 0))),
        out_shape=jax.ShapeDtypeStruct((R, D), _F32),
        compiler_params=_params("parallel", "arbitrary"),
        name="moe_grouped_ffn",
    )(tile_expert, n_valid, xs, w1, w3, w2)


def moe_residual(h, g, router, w1, w3, w2):
    T, D = h.shape
    E = N_EXPERTS
    xn, route = moe_route(h, g, router, tm=ROUTE_TM)
    top = route[:, 0:TOP_K_EXPERTS].astype(jnp.int32)
    gates = route[:, TOP_K_EXPERTS:2 * TOP_K_EXPERTS]
    ef = top.T.reshape(TOP_K_EXPERTS * T)
    onehot = (ef[:, None] == jnp.arange(E, dtype=jnp.int32)[None, :]).astype(jnp.int32)
    csum = jnp.cumsum(onehot, axis=0)
    rank = jnp.take_along_axis(csum, ef[:, None], axis=1)[:, 0] - 1
    counts = csum[-1]
    padded = ((counts + MOE_TM - 1) // MOE_TM) * MOE_TM
    pends = jnp.cumsum(padded)
    pstarts = pends - padded
    starts = jnp.cumsum(counts) - counts
    pos = pstarts[ef] + rank
    R = TOP_K_EXPERTS * T + E * MOE_TM
    n_tiles = R // MOE_TM
    tile_expert = jnp.searchsorted(pends, jnp.arange(n_tiles, dtype=jnp.int32) * MOE_TM, side='right')
    tile_expert = jnp.minimum(tile_expert, E - 1).astype(jnp.int32)
    n_valid = (pends[-1:] // MOE_TM).astype(jnp.int32)
    order = jnp.argsort(ef, stable=True).astype(jnp.int32)
    rho = jnp.arange(R, dtype=jnp.int32)
    e_row = tile_expert[rho // MOE_TM]
    off = rho - pstarts[e_row]
    src = order[jnp.clip(starts[e_row] + off, 0, TOP_K_EXPERTS * T - 1)] % T
    xs = jnp.take(xn, src, axis=0)
    ys = moe_grouped_ffn(xs, tile_expert, n_valid, w1, w3, w2)
    mix = sum(gates[:, slot:slot + 1] * jnp.take(ys, pos[slot * T:(slot + 1) * T], axis=0)
              for slot in range(TOP_K_EXPERTS))
    return h + mix


def _pack_w_in(w):
    wq, wk, wv, wqi, wki, wwi, wslab, wga, wgb = _split(w, IN_SIZES)
    D = w.shape[0]
    wb = jnp.concatenate([wq * (ATT_HEAD_DIM ** -0.5 * LOG2_E), wk, wv, wqi, wki, wki], axis=1)
    wf = jnp.concatenate([wslab, jnp.zeros((D, ZF_SLAB_PAD - RWKV_SLAB), w.dtype),
                          wki, wwi, jnp.zeros((D, LANES - IDX_DIM - IDX_HEADS), w.dtype), wga, wgb], axis=1)
    return wb.astype(_BF16), wf.astype(_BF16)


def kernel(x, p, w_in, att_up, rwkv_up, w_out, rel_bias, rwkv_mu, rwkv_w0, rwkv_w_up,
           rwkv_a0, rwkv_a_up, rwkv_g_up, rwkv_k_k, rwkv_k_a, rwkv_r_k, rwkv_gn_w, rwkv_gn_b,
           norm_mix, norm_ffn, norm_ple, ple_proj, ple_gate, ffn_w1, ffn_w3, ffn_w2,
           moe_router, moe_w1, moe_w3, moe_w2, final_norm):
    B, S, D = x.shape
    T = B * S
    depth = w_in.shape[0]
    h = x.reshape(T, D)
    for i in range(depth):
        wb, wf = _pack_w_in(w_in[i])
        zb = norm_matmul(h, norm_mix[i], wb, _BF16, tm=IN_PROJ_TM, tn=ZB_WIDTH)
        zf = norm_matmul(h, norm_mix[i], wf, _F32, tm=IN_PROJ_F32_TM, tn=IN_PROJ_F32_TN)
        att = dsa_attention(zb, zf, rel_bias, B, S)
        rw = rwkv7_mix(zf, rwkv_mu[i], rwkv_w0[i], rwkv_w_up[i], rwkv_a0[i], rwkv_a_up[i], rwkv_g_up[i],
                       rwkv_k_k[i], rwkv_k_a[i], rwkv_r_k[i], rwkv_gn_w[i], rwkv_gn_b[i], B, S)
        merged = gated_merge(att, rw, zf, att_up[i].astype(_BF16), rwkv_up[i].astype(_BF16),
                             tm=MERGE_TM, tn=MERGE_TN)
        h = matmul_residual(merged, w_out[i].astype(_BF16), h, tm=OUT_PROJ_TM, tn=D)
        if i % 2 == 0:
            j = i // 2
            h = ffn_residual(h, norm_ffn[i], ffn_w1[j].astype(_BF16), ffn_w3[j].astype(_BF16),
                             ffn_w2[j].astype(_BF16), tm=FFN_TM, tf=FFN_TF)
        else:
            j = i // 2
            h = moe_residual(h, norm_ffn[i], moe_router[j], cast_bf16(moe_w1[j]), cast_bf16(moe_w3[j]),
                             cast_bf16(moe_w2[j]))
        h = ple_residual(h, norm_ple[i], p[i].reshape(T, PLE_DIM), ple_gate[i].astype(_BF16),
                         ple_proj[i].astype(_BF16), final_norm, final=(i == depth - 1), tm=PLE_TM)
    return h.reshape(B, S, D)
```

```python
import functools
import math

import jax
import jax.numpy as jnp
import numpy as np
from jax import lax
from jax.experimental import pallas as pl
from jax.experimental.pallas import tpu as pltpu

D_MODEL = 2048
DEPTH = 2
ATT_HEADS = 8
ATT_KV_HEADS = 2
ATT_HEAD_DIM = 128
IDX_HEADS = 8
IDX_DIM = 64
TOPK_MAX = 256
REL_BUCKETS = 32
REL_MAX_DIST = 128
RWKV_HEAD = 64
RWKV_HEADS = 16
DECAY_RANK = 64
ICLR_RANK = 64
GATE_RANK = 160
D_FF = 5632
N_EXPERTS = 8
TOP_K_EXPERTS = 2
D_FF_EXPERT = 7168
PLE_DIM = 256
RMS_EPS = 1e-6
GN_EPS = 64e-5

ATT_WIDTH = ATT_HEADS * ATT_HEAD_DIM
KV_WIDTH = ATT_KV_HEADS * ATT_HEAD_DIM
RWKV_WIDTH = RWKV_HEADS * RWKV_HEAD
RWKV_SLAB = 3 * RWKV_WIDTH + DECAY_RANK + ICLR_RANK + GATE_RANK
IN_SIZES = (ATT_WIDTH, KV_WIDTH, KV_WIDTH, IDX_HEADS * IDX_DIM, IDX_DIM, IDX_HEADS, RWKV_SLAB, D_MODEL, D_MODEL)

LANES = 128
SUBLANES = 8
VMEM_LIMIT = 56 * 1024 * 1024
CAST_BLOCK_BYTES = 8 * 1024 * 1024

IN_PROJ_TM = 1024
IN_PROJ_F32_TM = 1024
IN_PROJ_F32_TN = 1536
MERGE_TM, MERGE_TN = 1024, 512
OUT_PROJ_TM = 512
FFN_TM, FFN_TF = 512, 512
ROUTE_TM = 512
PLE_TM = 512

Q_BLOCK = 128
KEY_CHUNK = 512
BLOCKS_PER_CHUNK = KEY_CHUNK // Q_BLOCK
BLOCKS_PER_GROUP = 32
MASKED_LOGIT = -1e30
LOG2_E = math.log2(math.e)
GQA_GROUP = ATT_HEADS // ATT_KV_HEADS

ZB_WIDTH = ATT_WIDTH + 2 * KV_WIDTH + IDX_HEADS * IDX_DIM + 2 * IDX_DIM
ZF_SLAB_PAD = 3456
ZF_WIDX = ZF_SLAB_PAD
ZF_GATE_A = ZF_WIDX + LANES
ZF_GATE_B = ZF_GATE_A + D_MODEL
ZF_WIDTH = ZF_GATE_B + D_MODEL

RWKV_CHUNK = 64
HEAD_PAIRS = RWKV_HEADS // 2
MOE_TM = 512
MOE_TF = 1024

_F32 = jnp.float32
_BF16 = jnp.bfloat16
_NT = (((1,), (1,)), ((), ()))
_TN = (((0,), (0,)), ((), ()))


def _split(z, sizes):
    return jnp.split(z, np.cumsum(sizes)[:-1].tolist(), axis=-1)


def _sigmoid(x):
    return 1.0 / (1.0 + jnp.exp(-x))


def _rms_to_bf16(x, g):
    y = x * lax.rsqrt(jnp.mean(x * x, axis=-1, keepdims=True) + RMS_EPS)
    return (y * g).astype(_BF16)


def _params(*sem):
    return pltpu.CompilerParams(dimension_semantics=sem, vmem_limit_bytes=VMEM_LIMIT)


def _norm_mm_kernel(x_ref, g_ref, w_ref, o_ref, xn_ref):
    @pl.when(pl.program_id(1) == 0)
    def _():
        xn_ref[...] = _rms_to_bf16(x_ref[...], g_ref[...])

    o_ref[...] = jnp.dot(xn_ref[...], w_ref[...], preferred_element_type=_F32).astype(o_ref.dtype)


def norm_matmul(x, g, w, out_dtype, tm, tn):
    T, D = x.shape
    N = w.shape[1]
    tm = min(tm, T)
    return pl.pallas_call(
        _norm_mm_kernel,
        grid=(T // tm, N // tn),
        in_specs=[pl.BlockSpec((tm, D), lambda i, j: (i, 0)),
                  pl.BlockSpec((1, D), lambda i, j: (0, 0)),
                  pl.BlockSpec((D, tn), lambda i, j: (0, j))],
        out_specs=pl.BlockSpec((tm, tn), lambda i, j: (i, j)),
        out_shape=jax.ShapeDtypeStruct((T, N), out_dtype),
        scratch_shapes=[pltpu.VMEM((tm, D), _BF16)],
        compiler_params=_params("parallel", "arbitrary"),
        name="norm_matmul",
    )(x, g.reshape(1, D), w)


def _merge_kernel(att_ref, rw_ref, ga_ref, gb_ref, wa_ref, wr_ref, o_ref):
    a = jnp.dot(att_ref[...], wa_ref[...], preferred_element_type=_F32)
    r = jnp.dot(rw_ref[...], wr_ref[...], preferred_element_type=_F32)
    o_ref[...] = (_sigmoid(ga_ref[...]) * a + _sigmoid(gb_ref[...]) * r).astype(o_ref.dtype)


def gated_merge(att, rw, zf, wa, wr, tm, tn):
    T, K = att.shape
    N = wa.shape[1]
    assert ZF_GATE_A % tn == 0 and ZF_GATE_B % tn == 0
    ja, jb = ZF_GATE_A // tn, ZF_GATE_B // tn
    return pl.pallas_call(
        _merge_kernel,
        grid=(T // tm, N // tn),
        in_specs=[pl.BlockSpec((tm, K), lambda i, j: (i, 0)),
                  pl.BlockSpec((tm, K), lambda i, j: (i, 0)),
                  pl.BlockSpec((tm, tn), lambda i, j: (i, ja + j)),
                  pl.BlockSpec((tm, tn), lambda i, j: (i, jb + j)),
                  pl.BlockSpec((K, tn), lambda i, j: (0, j)),
                  pl.BlockSpec((K, tn), lambda i, j: (0, j))],
        out_specs=pl.BlockSpec((tm, tn), lambda i, j: (i, j)),
        out_shape=jax.ShapeDtypeStruct((T, N), _BF16),
        compiler_params=_params("parallel", "arbitrary"),
        name="gated_merge",
    )(att, rw, zf, zf, wa, wr)


def _mm_res_kernel(x_ref, w_ref, h_ref, o_ref):
    o_ref[...] = h_ref[...] + jnp.dot(x_ref[...], w_ref[...], preferred_element_type=_F32)


def matmul_residual(x, w, h, tm, tn):
    T, K = x.shape
    N = w.shape[1]
    return pl.pallas_call(
        _mm_res_kernel,
        grid=(T // tm, N // tn),
        in_specs=[pl.BlockSpec((tm, K), lambda i, j: (i, 0)),
                  pl.BlockSpec((K, tn), lambda i, j: (0, j)),
                  pl.BlockSpec((tm, tn), lambda i, j: (i, j))],
        out_specs=pl.BlockSpec((tm, tn), lambda i, j: (i, j)),
        out_shape=jax.ShapeDtypeStruct((T, N), _F32),
        compiler_params=_params("parallel", "arbitrary"),
        name="matmul_residual",
    )(x, w, h)


def _ffn_kernel(h_ref, g_ref, w1_ref, w3_ref, w2_ref, o_ref, xn_ref):
    f = pl.program_id(1)

    @pl.when(f == 0)
    def _():
        xn_ref[...] = _rms_to_bf16(h_ref[...], g_ref[...])
        o_ref[...] = h_ref[...]

    xn = xn_ref[...]
    a = jnp.dot(xn, w1_ref[...], preferred_element_type=_F32)
    b = jnp.dot(xn, w3_ref[...], preferred_element_type=_F32)
    mid = (a * _sigmoid(a) * b).astype(_BF16)
    o_ref[...] += jnp.dot(mid, w2_ref[...], preferred_element_type=_F32)


def ffn_residual(h, g, w1, w3, w2, tm, tf):
    T, D = h.shape
    F = w1.shape[1]
    return pl.pallas_call(
        _ffn_kernel,
        grid=(T // tm, F // tf),
        in_specs=[pl.BlockSpec((tm, D), lambda i, f: (i, 0)),
                  pl.BlockSpec((1, D), lambda i, f: (0, 0)),
                  pl.BlockSpec((D, tf), lambda i, f: (0, f)),
                  pl.BlockSpec((D, tf), lambda i, f: (0, f)),
                  pl.BlockSpec((tf, D), lambda i, f: (f, 0))],
        out_specs=pl.BlockSpec((tm, D), lambda i, f: (i, 0)),
        out_shape=jax.ShapeDtypeStruct((T, D), _F32),
        scratch_shapes=[pltpu.VMEM((tm, D), _BF16)],
        compiler_params=_params("parallel", "arbitrary"),
        name="ffn_residual",
    )(h, g.reshape(1, D), w1, w3, w2)


def _ple_kernel(h_ref, g_ref, p_ref, wg_ref, wp_ref, fg_ref, o_ref, *, final):
    h = h_ref[...]
    xn = _rms_to_bf16(h, g_ref[...])
    gate = _sigmoid(jnp.dot(xn, wg_ref[...], preferred_element_type=_F32))
    pp = jnp.dot(p_ref[...].astype(_BF16), wp_ref[...], preferred_element_type=_F32)
    out = h + gate * pp
    if final:
        out = out * lax.rsqrt(jnp.mean(out * out, axis=-1, keepdims=True) + RMS_EPS) * fg_ref[...]
    o_ref[...] = out


def ple_residual(h, g, p, wg, wp, final_g, final, tm):
    T, D = h.shape
    P = p.shape[1]
    return pl.pallas_call(
        functools.partial(_ple_kernel, final=final),
        grid=(T // tm,),
        in_specs=[pl.BlockSpec((tm, D), lambda i: (i, 0)),
                  pl.BlockSpec((1, D), lambda i: (0, 0)),
                  pl.BlockSpec((tm, P), lambda i: (i, 0)),
                  pl.BlockSpec((D, D), lambda i: (0, 0)),
                  pl.BlockSpec((P, D), lambda i: (0, 0)),
                  pl.BlockSpec((1, D), lambda i: (0, 0))],
        out_specs=pl.BlockSpec((tm, D), lambda i: (i, 0)),
        out_shape=jax.ShapeDtypeStruct((T, D), _F32),
        compiler_params=_params("parallel"),
        name="ple_residual",
    )(h, g.reshape(1, D), p, wg, wp, final_g.reshape(1, D))


def _cast_kernel(x_ref, o_ref):
    o_ref[...] = x_ref[...].astype(o_ref.dtype)


def cast_bf16(w):
    E, R, N = w.shape
    rows = R
    while rows * N * 4 > CAST_BLOCK_BYTES and rows % 2 == 0:
        rows //= 2
    return pl.pallas_call(
        _cast_kernel,
        grid=(E, R // rows),
        in_specs=[pl.BlockSpec((None, rows, N), lambda e, i: (e, i, 0))],
        out_specs=pl.BlockSpec((None, rows, N), lambda e, i: (e, i, 0)),
        out_shape=jax.ShapeDtypeStruct(w.shape, _BF16),
        compiler_params=_params("parallel", "parallel"),
        name="cast_bf16",
    )(w)


def rel_bucket(dist):
    max_exact = REL_BUCKETS // 2
    dist = jnp.maximum(dist, 0)
    d_f = jnp.maximum(dist, 1).astype(_F32)
    large = max_exact + (jnp.log(d_f / max_exact) / math.log(REL_MAX_DIST / max_exact)
                         * (REL_BUCKETS - max_exact)).astype(jnp.int32)
    large = jnp.minimum(large, REL_BUCKETS - 1)
    return jnp.where(dist < max_exact, dist, large)


def _sortable(x):
    b = lax.bitcast_convert_type(x, jnp.int32)
    return b ^ ((b >> 31) & jnp.int32(0x7FFFFFFF))


_INT_MIN = -2 ** 31
_KEY_NEG_INF = 0x807FFFFF - 2 ** 32


def _dsa_kernel(q_ref, qi_ref, wi_ref, k_ref, v_ref, ki_ref, bprev_ref, bdiag_ref, bfar_ref, o_ref,
                keys_ref, planes_ref, active_ref, qs_ref, qim_ref, m_ref, acc_ref, *, n_sel, idx_bits):
    n_grp = planes_ref.shape[0]
    qb = pl.program_id(1)
    t0 = qb * Q_BLOCK
    n_chunks = qb // BLOCKS_PER_CHUNK + 1
    t_idx = t0 + lax.broadcasted_iota(jnp.int32, (Q_BLOCK, 1), 0)
    lane_blk = lax.broadcasted_iota(jnp.int32, (1, Q_BLOCK), 1)
    lane_chunk = lax.broadcasted_iota(jnp.int32, (1, KEY_CHUNK), 1)

    for hh in range(ATT_HEADS):
        g, h = divmod(hh, GQA_GROUP)
        qs_ref[g, h * Q_BLOCK:(h + 1) * Q_BLOCK, :] = q_ref[:, hh * ATT_HEAD_DIM:(hh + 1) * ATT_HEAD_DIM]
    low = lax.broadcasted_iota(jnp.int32, (Q_BLOCK, LANES), 1) < IDX_DIM
    for hh in range(IDX_HEADS):
        tile = qi_ref[:, (hh // 2) * LANES:(hh // 2 + 1) * LANES]
        keep = low if hh % 2 == 0 else jnp.logical_not(low)
        qim_ref[hh] = jnp.where(keep, tile, jnp.zeros_like(tile))
    m_ref[...] = jnp.full(m_ref.shape, MASKED_LOGIT, _F32)

    @pl.when(qb == 0)
    def _():
        keys_ref[BLOCKS_PER_CHUNK:] = jnp.zeros((keys_ref.shape[0] - BLOCKS_PER_CHUNK, Q_BLOCK, Q_BLOCK), jnp.int32)
        planes_ref[...] = jnp.zeros(planes_ref.shape, jnp.int32)
    acc_ref[...] = jnp.zeros(acc_ref.shape, _F32)

    w_all = wi_ref[...]
    idx_scale = (IDX_HEADS ** -0.5) * (IDX_DIM ** -0.5)

    def score_body(c, carry):
        ks = pl.multiple_of(c * KEY_CHUNK, KEY_CHUNK)
        kic = ki_ref[pl.ds(ks, KEY_CHUNK), :]
        d = lax.dot_general(qim_ref[...].reshape(IDX_HEADS * Q_BLOCK, LANES), kic, _NT,
                            preferred_element_type=_F32)
        sc = jnp.zeros((Q_BLOCK, KEY_CHUNK), _F32)
        for hh in range(IDX_HEADS):
            sc = sc + w_all[:, IDX_DIM + hh:IDX_DIM + hh + 1] * jnp.maximum(d[hh * Q_BLOCK:(hh + 1) * Q_BLOCK], 0.0)
        sc = sc * idx_scale
        sc = jnp.where(sc == 0.0, 0.0, sc)
        key = jnp.where(ks + lane_chunk <= t_idx, _sortable(sc), _KEY_NEG_INF)
        for j in range(BLOCKS_PER_CHUNK):
            keys_ref[c * BLOCKS_PER_CHUNK + j] = key[:, j * Q_BLOCK:(j + 1) * Q_BLOCK]
        return carry

    lax.fori_loop(0, n_chunks, score_body, 0)

    def count(pred):
        def body(c, acc):
            for j in range(BLOCKS_PER_CHUNK):
                blk = c * BLOCKS_PER_CHUNK + j
                acc = acc + jnp.where(pred(keys_ref[blk], blk * Q_BLOCK + lane_blk), 1.0, 0.0)
            return acc

        acc = lax.fori_loop(0, n_chunks, body, jnp.zeros((Q_BLOCK, Q_BLOCK), _F32))
        return jnp.sum(acc, axis=1, keepdims=True)

    n_blocks = n_chunks * BLOCKS_PER_CHUNK
    n_groups = (n_blocks + BLOCKS_PER_GROUP - 1) // BLOCKS_PER_GROUP

    def transpose_group(g, carry):
        def rows(rg, carry_rows):
            r0 = pl.multiple_of(rg * SUBLANES, SUBLANES)
            x = [keys_ref[g * BLOCKS_PER_GROUP + j, pl.ds(r0, SUBLANES), :] for j in range(BLOCKS_PER_GROUP)]
            j, m = 16, 0x0000FFFF
            while j:
                k = 0
                while k < BLOCKS_PER_GROUP:
                    t = (x[k] ^ lax.shift_right_logical(x[k + j], jnp.int32(j))) & jnp.int32(m)
                    x[k] = x[k] ^ t
                    x[k + j] = x[k + j] ^ lax.shift_left(t, jnp.int32(j))
                    k = (k + j + 1) & ~j
                j >>= 1
                m ^= m << j
            planes_ref[g, 0, pl.ds(r0, SUBLANES), :] = ~x[0]
            for i in range(1, 32):
                planes_ref[g, i, pl.ds(r0, SUBLANES), :] = x[i]
            return carry_rows

        lax.fori_loop(0, Q_BLOCK // SUBLANES, rows, 0)
        n_live = jnp.minimum(n_blocks - g * BLOCKS_PER_GROUP, BLOCKS_PER_GROUP)
        active_ref[g] = lax.shift_left(jnp.full((Q_BLOCK, Q_BLOCK), -1, jnp.int32), BLOCKS_PER_GROUP - n_live)
        return carry

    active_ref[...] = jnp.zeros(active_ref.shape, jnp.int32)
    lax.fori_loop(0, n_groups, transpose_group, 0)

    def lane_total(words):
        return jnp.sum(sum(lax.population_count(w) for w in words).astype(_F32), axis=1, keepdims=True)

    def split_active(g, i):
        act = active_ref[g]
        hi = act & planes_ref[g, 2 * i]
        lo = act ^ hi
        hi1 = hi & planes_ref[g, 2 * i + 1]
        lo1 = lo & planes_ref[g, 2 * i + 1]
        return hi1, hi ^ hi1, lo1, lo ^ lo1

    def radix_pass(i, carry):
        prefix, above = carry
        parts = [split_active(g, i) for g in range(n_grp)]
        c11, c10, c01 = (lane_total([parts[g][q] for g in range(n_grp)]) for q in range(3))
        upto10 = above + c11 + c10
        t11 = above + c11 >= n_sel
        t10 = upto10 >= n_sel
        t01 = upto10 + c01 >= n_sel
        for g in range(n_grp):
            s11, s10, s01, s00 = split_active(g, i)
            active_ref[g] = jnp.where(t11, s11, jnp.where(t10, s10, jnp.where(t01, s01, s00)))
        bit1 = jnp.where(t10, lax.shift_left(jnp.int32(1), 31 - 2 * i), 0)
        bit0 = jnp.where(jnp.logical_or(t11, jnp.logical_and(jnp.logical_not(t10), t01)),
                         lax.shift_left(jnp.int32(1), 30 - 2 * i), 0)
        above = jnp.where(t11, above, jnp.where(t10, above + c11, jnp.where(t01, upto10, upto10 + c01)))
        return prefix | bit1 | bit0, above

    prefix, n_gt = lax.fori_loop(0, 16, radix_pass, (jnp.zeros((Q_BLOCK, 1), jnp.int32),
                                                     jnp.zeros((Q_BLOCK, 1), _F32)))
    tau = prefix ^ jnp.int32(_INT_MIN)
    n_eq = lane_total([active_ref[g] for g in range(n_grp)])
    tied = jnp.logical_and(n_gt + n_eq > n_sel, tau > _KEY_NEG_INF)

    def tie_search():
        need = n_sel - count(lambda key, sidx: key > tau)

        def index_bit(i, x):
            cand = x | lax.shift_left(jnp.int32(1), idx_bits - 1 - i)
            cnt = count(lambda key, sidx: jnp.logical_and(key == tau, sidx < cand))
            return jnp.where(cnt < need, cand, x)

        return lax.fori_loop(0, idx_bits, index_bit, jnp.zeros((Q_BLOCK, 1), jnp.int32))

    last_tie = lax.cond(jnp.max(jnp.where(tied, 1.0, 0.0)) > 0.0, tie_search,
                        lambda: jnp.full((Q_BLOCK, 1), 2 ** idx_bits, jnp.int32))

    def attend(key, k_tile, v_tile, s0, lane, bias_ref, enable=None, const_bias=False):
        width = key.shape[1]
        sidx = s0 + lane
        sel = jnp.logical_or(key > tau, jnp.logical_and(key == tau, sidx <= last_tie))
        sel = jnp.logical_and(sel, key > _KEY_NEG_INF)
        if enable is not None:
            sel = jnp.logical_and(sel, jnp.zeros_like(sidx) + enable > 0)
        ones = jnp.ones((width, ATT_HEAD_DIM), _BF16)
        groups = range(ATT_KV_HEADS)
        cols = [slice(g * ATT_HEAD_DIM, (g + 1) * ATT_HEAD_DIM) for g in groups]

        def logits(g):
            return lax.dot_general(qs_ref[g], k_tile[:, cols[g]], _NT,
                                   preferred_element_type=_F32).reshape(GQA_GROUP, Q_BLOCK, width)

        def softmax(g, s):
            bias = bias_ref[g * GQA_GROUP:(g + 1) * GQA_GROUP]
            m_old = m_ref[g]
            if const_bias:
                sh = jnp.where(sel[None], s, MASKED_LOGIT)
                m_new = jnp.maximum(m_old, jnp.max(sh, axis=2, keepdims=True) + bias)
                shift = m_new - bias
            else:
                sh = jnp.where(sel[None], s + bias, MASKED_LOGIT)
                m_new = jnp.maximum(m_old, jnp.max(sh, axis=2, keepdims=True))
                shift = m_new
            m_ref[g] = m_new
            return (jnp.exp2(sh - shift).reshape(GQA_GROUP * Q_BLOCK, width).astype(_BF16),
                    jnp.exp2(m_old - m_new).reshape(GQA_GROUP * Q_BLOCK, 1))

        def accumulate(g, p, alpha):
            pv = jnp.dot(p, jnp.concatenate([v_tile[:, cols[g]], ones], axis=1), preferred_element_type=_F32)
            acc_ref[g] = acc_ref[g] * alpha + pv

        if width <= Q_BLOCK:
            s = [logits(g) for g in groups]
            pa = [softmax(g, s[g]) for g in groups]
            for g in groups:
                accumulate(g, *pa[g])
        else:
            for g in groups:
                accumulate(g, *softmax(g, logits(g)))

    n_far = jnp.maximum(qb - 1, 0) // BLOCKS_PER_CHUNK

    def far_body(c, carry):
        ks = pl.multiple_of(c * KEY_CHUNK, KEY_CHUNK)
        key = jnp.concatenate([keys_ref[c * BLOCKS_PER_CHUNK + j] for j in range(BLOCKS_PER_CHUNK)], axis=1)
        attend(key, k_ref[pl.ds(ks, KEY_CHUNK), :], v_ref[pl.ds(ks, KEY_CHUNK), :], ks, lane_chunk, bfar_ref,
               const_bias=True)
        return carry

    lax.fori_loop(0, n_far, far_body, 0)

    def block_step(j, bias_ref, enable=None, const_bias=False):
        ks = pl.multiple_of(j * Q_BLOCK, Q_BLOCK)
        attend(keys_ref[j], k_ref[pl.ds(ks, Q_BLOCK), :], v_ref[pl.ds(ks, Q_BLOCK), :], ks, lane_blk, bias_ref,
               enable, const_bias)

    def tail_body(j, carry):
        block_step(j, bfar_ref, const_bias=True)
        return carry

    lax.fori_loop(n_far * BLOCKS_PER_CHUNK, qb - 1, tail_body, 0)
    block_step(jnp.maximum(qb - 1, 0), bprev_ref, enable=jnp.minimum(qb, 1))
    block_step(qb, bdiag_ref)

    for hh in range(ATT_HEADS):
        g, h = divmod(hh, GQA_GROUP)
        rows = slice(h * Q_BLOCK, (h + 1) * Q_BLOCK)
        o_ref[:, hh * ATT_HEAD_DIM:(hh + 1) * ATT_HEAD_DIM] = (
            acc_ref[g, rows, 0:ATT_HEAD_DIM] / acc_ref[g, rows, ATT_HEAD_DIM:2 * ATT_HEAD_DIM]).astype(o_ref.dtype)


def dsa_attention(zb, zf, rel_bias, B, S):
    nq = S // Q_BLOCK
    n_grp = -(-nq // BLOCKS_PER_GROUP)
    n_sel = min(TOPK_MAX, S // 4)
    tl = jnp.arange(Q_BLOCK, dtype=jnp.int32)[:, None]
    sr = jnp.arange(Q_BLOCK, dtype=jnp.int32)[None, :]
    rel_bias = rel_bias.astype(_F32) * LOG2_E
    bprev = rel_bias[rel_bucket(tl + Q_BLOCK - sr)].transpose(2, 0, 1)
    bdiag = rel_bias[rel_bucket(tl - sr)].transpose(2, 0, 1)
    bfar = rel_bias[rel_bucket(jnp.int32(REL_MAX_DIST + 1))].reshape(ATT_HEADS, 1, 1)
    kernel = functools.partial(_dsa_kernel, n_sel=float(n_sel), idx_bits=int(S - 1).bit_length())
    c_k = ATT_WIDTH // KV_WIDTH
    c_qi = (ATT_WIDTH + 2 * KV_WIDTH) // (IDX_HEADS * IDX_DIM)
    c_ki = (ATT_WIDTH + 2 * KV_WIDTH + IDX_HEADS * IDX_DIM) // LANES
    return pl.pallas_call(
        kernel,
        grid=(B, nq),
        in_specs=[pl.BlockSpec((Q_BLOCK, ATT_WIDTH), lambda b, i: (b * nq + i, 0)),
                  pl.BlockSpec((Q_BLOCK, IDX_HEADS * IDX_DIM), lambda b, i: (b * nq + i, c_qi)),
                  pl.BlockSpec((Q_BLOCK, LANES), lambda b, i: (b * nq + i, ZF_WIDX // LANES)),
                  pl.BlockSpec((S, KV_WIDTH), lambda b, i: (b, c_k)),
                  pl.BlockSpec((S, KV_WIDTH), lambda b, i: (b, c_k + 1)),
                  pl.BlockSpec((S, LANES), lambda b, i: (b, c_ki)),
                  pl.BlockSpec((ATT_HEADS, Q_BLOCK, Q_BLOCK), lambda b, i: (0, 0, 0)),
                  pl.BlockSpec((ATT_HEADS, Q_BLOCK, Q_BLOCK), lambda b, i: (0, 0, 0)),
                  pl.BlockSpec((ATT_HEADS, 1, 1), lambda b, i: (0, 0, 0))],
        out_specs=pl.BlockSpec((Q_BLOCK, ATT_WIDTH), lambda b, i: (b * nq + i, 0)),
        out_shape=jax.ShapeDtypeStruct((B * S, ATT_WIDTH), _BF16),
        scratch_shapes=[pltpu.VMEM((n_grp * BLOCKS_PER_GROUP, Q_BLOCK, Q_BLOCK), jnp.int32),
                        pltpu.VMEM((n_grp, 32, Q_BLOCK, Q_BLOCK), jnp.int32),
                        pltpu.VMEM((n_grp, Q_BLOCK, Q_BLOCK), jnp.int32),
                        pltpu.VMEM((ATT_KV_HEADS, GQA_GROUP * Q_BLOCK, ATT_HEAD_DIM), _BF16),
                        pltpu.VMEM((IDX_HEADS, Q_BLOCK, LANES), _BF16),
                        pltpu.VMEM((ATT_KV_HEADS, GQA_GROUP, Q_BLOCK, 1), _F32),
                        pltpu.VMEM((ATT_KV_HEADS, GQA_GROUP * Q_BLOCK, 2 * ATT_HEAD_DIM), _F32)],
        compiler_params=_params("parallel", "arbitrary"),
        name="dsa_attention",
    )(zb, zb, zf, zb, zb, zb, bprev, bdiag, bfar)


def _split_dot(x, w):
    hi = x.astype(_BF16)
    lo = (x - hi.astype(_F32)).astype(_BF16)
    return jnp.dot(hi, w, preferred_element_type=_F32) + jnp.dot(lo, w, preferred_element_type=_F32)


def _split_dot_left(w, x):
    hi = x.astype(_BF16)
    lo = (x - hi.astype(_F32)).astype(_BF16)
    return jnp.dot(w, hi, preferred_element_type=_F32) + jnp.dot(w, lo, preferred_element_type=_F32)


def _rwkv_kernel(slab_ref, mu_ref, vec_ref, wup_ref, aup_ref, gup_ref, o_ref, xs_ref, state_ref):
    C = RWKV_CHUNK
    W = RWKV_WIDTH
    n_batch = slab_ref.shape[0]
    c = pl.program_id(0)

    @pl.when(c == 0)
    def _():
        xs_ref[:, 0:SUBLANES, :] = jnp.zeros((n_batch, SUBLANES, ZF_SLAB_PAD), _F32)
        state_ref[...] = jnp.zeros(state_ref.shape, _F32)

    w0, a0, k_k, k_a, r_k, gn_w, gn_b = (vec_ref[i:i + 1, :] for i in range(7))
    ti = lax.broadcasted_iota(jnp.int32, (C, C), 0)
    tj = lax.broadcasted_iota(jnp.int32, (C, C), 1)
    tri = jnp.where(tj <= ti, 1.0, 0.0).astype(_BF16)

    def prepare(bi):
        slab = slab_ref[bi]
        xs_ref[bi, SUBLANES:SUBLANES + C, :] = slab
        shifted = xs_ref[bi, SUBLANES - 1:SUBLANES - 1 + C, :]
        xs = slab + (shifted - slab) * mu_ref[...]
        xs_ref[bi, 0:SUBLANES, :] = slab[C - SUBLANES:C, :]

        r = xs[:, 0:W]
        k = xs[:, W:2 * W]
        lora = xs[:, 3 * W:3 * W + LANES]
        xg = xs[:, 3 * W + LANES:3 * W + 3 * LANES]
        d = w0 + jnp.dot(jnp.tanh(lora).astype(_BF16), wup_ref[...], preferred_element_type=_F32)
        logw = -math.exp(-0.5) * _sigmoid(d)
        a = _sigmoid(a0 + jnp.dot(lora.astype(_BF16), aup_ref[...], preferred_element_type=_F32))
        kt = k * (1.0 + (a - 1.0) * k_a)
        cum = _split_dot_left(tri, logw)
        cum_end = cum[C - 1:C, :]
        return dict(r=r, v=xs[:, 2 * W:3 * W], a=a, kt=kt, kk_raw=k * k_k, bonus_raw=r * kt * r_k,
                    g=jnp.dot(_sigmoid(xg).astype(_BF16), gup_ref[...], preferred_element_type=_F32),
                    g_incl=jnp.exp(cum), g_excl=jnp.exp(cum - logw), g_inv=jnp.exp(-cum),
                    g_end=jnp.exp(cum_end - cum), g_total=jnp.exp(cum_end))

    prepared = [prepare(bi) for bi in range(n_batch)]

    lo = lax.broadcasted_iota(jnp.int32, (C, LANES), 1) < RWKV_HEAD
    ri = lax.broadcasted_iota(jnp.int32, (LANES, LANES), 0)
    ci = lax.broadcasted_iota(jnp.int32, (LANES, LANES), 1)
    same_head = (ri // RWKV_HEAD) == (ci // RWKV_HEAD)
    ones_bd = jnp.where(same_head, 1.0, 0.0).astype(_BF16)
    eye = ri == ci
    same_blk = (ri // C) == (ci // C)
    strict = jnp.logical_and(same_blk, ci < ri)
    incl = jnp.logical_and(same_blk, ci <= ri)

    def stack(x):
        return jnp.concatenate([jnp.where(lo, x, 0.0), jnp.where(lo, 0.0, x)], axis=0)

    def dot(x, y):
        return jnp.dot(x.astype(_BF16), y.astype(_BF16), preferred_element_type=_F32)

    pairs = range(n_batch * HEAD_PAIRS)
    sls = [slice((n % HEAD_PAIRS) * LANES, (n % HEAD_PAIRS + 1) * LANES) for n in pairs]

    def col(name, n):
        return prepared[n // HEAD_PAIRS][name][:, sls[n]]

    h_bd = [state_ref[p] for p in pairs]
    kk = [col('kk_raw', p) * lax.rsqrt(jnp.maximum(_split_dot(col('kk_raw', p) * col('kk_raw', p), ones_bd), 1e-24))
          for p in pairs]
    b = [kk[p] * col('a', p) for p in pairs]
    lhs4 = [jnp.concatenate([stack(kk[p] * col('g_excl', p)), stack(col('r', p) * col('g_incl', p))],
                            axis=0).astype(_BF16) for p in pairs]
    rhs4 = []
    for p in pairs:
        bi = b[p] * col('g_inv', p)
        ki = col('kt', p) * col('g_inv', p)
        rhs4.append(jnp.concatenate([bi, bi, ki, ki], axis=0).astype(_BF16))
    gm = [lax.dot_general(lhs4[p], rhs4[p], _NT, preferred_element_type=_F32) for p in pairs]
    a_bd = [jnp.where(strict, gm[p][:2 * C, :2 * C], 0.0) for p in pairs]
    bp_bd = [jnp.concatenate([jnp.where(strict, gm[p][:2 * C, 2 * C:], 0.0),
                              jnp.where(incl, gm[p][2 * C:, 2 * C:], 0.0)], axis=0) for p in pairs]
    pb_bd = [jnp.where(incl, gm[p][2 * C:, :2 * C], 0.0) for p in pairs]

    inv = [jnp.where(eye, 1.0, 0.0) - jnp.where((ri // 2) == (ci // 2), a_bd[p], 0.0) for p in pairs]
    s = 2
    while s < C:
        off = jnp.logical_and((ri // (2 * s)) == (ci // (2 * s)), (ri // s) != (ci // s))
        tm = [dot(inv[p], jnp.where(off, a_bd[p], 0.0)) for p in pairs]
        inv = [inv[p] - dot(tm[p], inv[p]) for p in pairs]
        s *= 2

    v_st = [stack(col('v', p)) for p in pairs]
    hw = [dot(lhs4[p], h_bd[p]) for p in pairs]
    bv = [dot(bp_bd[p], v_st[p]) for p in pairs]
    u_st = [dot(inv[p], hw[p][:2 * C] + bv[p][:2 * C]) for p in pairs]
    y_st = [hw[p][2 * C:] + bv[p][2 * C:] - dot(pb_bd[p], u_st[p]) for p in pairs]
    y = [y_st[p][:C] + y_st[p][C:] for p in pairs]
    u = [u_st[p][:C] + u_st[p][C:] for p in pairs]

    for p in pairs:
        m_t = jnp.concatenate([col('kt', p) * col('g_end', p), -(b[p] * col('g_end', p))], axis=0).astype(_BF16)
        n_t = jnp.concatenate([col('v', p), u[p]], axis=0).astype(_BF16)
        dh = lax.dot_general(m_t, n_t, _TN, preferred_element_type=_F32)
        gt_col = jnp.sum(jnp.where(eye, jnp.broadcast_to(col('g_total', p), (LANES, LANES)), 0.0),
                         axis=1, keepdims=True)
        state_ref[p] = jnp.where(same_head, gt_col * h_bd[p] + dh, 0.0)

    mean = [_split_dot(y[p], ones_bd) * (1.0 / RWKV_HEAD) for p in pairs]
    yc = [y[p] - mean[p] for p in pairs]
    var = [_split_dot(yc[p] * yc[p], ones_bd) * (1.0 / RWKV_HEAD) for p in pairs]
    for p in pairs:
        sl = sls[p]
        yn = yc[p] * lax.rsqrt(var[p] + GN_EPS) * gn_w[:, sl] + gn_b[:, sl]
        bonus = _split_dot(col('bonus_raw', p), ones_bd) * col('v', p)
        o_ref[p // HEAD_PAIRS, :, sl] = ((yn + bonus) * col('g', p)).astype(o_ref.dtype)


def rwkv7_mix(zf, mu, w0, w_up, a0, a_up, g_up, k_k, k_a, r_k, gn_w, gn_b, B, S):
    C = RWKV_CHUNK
    nc = S // C
    W = RWKV_WIDTH
    mu_p = jnp.pad(mu, (0, ZF_SLAB_PAD - RWKV_SLAB)).reshape(1, ZF_SLAB_PAD)
    vecs = jnp.stack([w0, a0, k_k, k_a, r_k, gn_w, gn_b, jnp.zeros_like(w0)]).astype(_F32)
    wup = jnp.concatenate([w_up, jnp.zeros((LANES - DECAY_RANK, W), w_up.dtype)], axis=0).astype(_BF16)
    aup = jnp.concatenate([jnp.zeros((DECAY_RANK, W), a_up.dtype), a_up], axis=0).astype(_BF16)
    gup = jnp.concatenate([g_up, jnp.zeros((2 * LANES - GATE_RANK, W), g_up.dtype)], axis=0).astype(_BF16)
    out = pl.pallas_call(
        _rwkv_kernel,
        grid=(nc,),
        in_specs=[pl.BlockSpec((B, C, ZF_SLAB_PAD), lambda c: (0, c, 0)),
                  pl.BlockSpec((1, ZF_SLAB_PAD), lambda c: (0, 0)),
                  pl.BlockSpec((SUBLANES, W), lambda c: (0, 0)),
                  pl.BlockSpec((LANES, W), lambda c: (0, 0)),
                  pl.BlockSpec((LANES, W), lambda c: (0, 0)),
                  pl.BlockSpec((2 * LANES, W), lambda c: (0, 0))],
        out_specs=pl.BlockSpec((B, C, W), lambda c: (0, c, 0)),
        out_shape=jax.ShapeDtypeStruct((B, S, W), _BF16),
        scratch_shapes=[pltpu.VMEM((B, C + SUBLANES, ZF_SLAB_PAD), _F32),
                        pltpu.VMEM((B * HEAD_PAIRS, LANES, LANES), _F32)],
        compiler_params=_params("arbitrary"),
        name="rwkv7_mix",
    )(zf.reshape(B, S, ZF_WIDTH), mu_p, vecs, wup, aup, gup)
    return out.reshape(B * S, W)


def _router_kernel(h_ref, g_ref, wr_ref, xn_ref, route_ref):
    xn = _rms_to_bf16(h_ref[...], g_ref[...])
    xn_ref[...] = xn
    logits = jnp.dot(xn, wr_ref[...], preferred_element_type=_F32)
    lane = lax.broadcasted_iota(jnp.int32, logits.shape, 1).astype(_F32)
    l1 = jnp.where(lane < N_EXPERTS, logits, -jnp.inf)
    m1 = jnp.max(l1, axis=1, keepdims=True)
    i1 = jnp.min(jnp.where(l1 == m1, lane, float(LANES)), axis=1, keepdims=True)
    l2 = jnp.where(lane == i1, -jnp.inf, l1)
    m2 = jnp.max(l2, axis=1, keepdims=True)
    i2 = jnp.min(jnp.where(l2 == m2, lane, float(LANES)), axis=1, keepdims=True)
    e2 = jnp.exp(m2 - m1)
    g1 = 1.0 / (1.0 + e2)
    g2 = e2 / (1.0 + e2)
    route_ref[...] = jnp.where(lane == 0.0, i1, jnp.where(lane == 1.0, i2,
                               jnp.where(lane == 2.0, g1, jnp.where(lane == 3.0, g2, 0.0))))


def moe_route(h, g, router, tm):
    T, D = h.shape
    wr = jnp.pad(router, ((0, 0), (0, LANES - N_EXPERTS))).astype(_BF16)
    return pl.pallas_call(
        _router_kernel,
        grid=(T // tm,),
        in_specs=[pl.BlockSpec((tm, D), lambda i: (i, 0)),
                  pl.BlockSpec((1, D), lambda i: (0, 0)),
                  pl.BlockSpec((D, LANES), lambda i: (0, 0))],
        out_specs=[pl.BlockSpec((tm, D), lambda i: (i, 0)),
                   pl.BlockSpec((tm, LANES), lambda i: (i, 0))],
        out_shape=[jax.ShapeDtypeStruct((T, D), _BF16), jax.ShapeDtypeStruct((T, LANES), _F32)],
        compiler_params=_params("parallel"),
        name="moe_route",
    )(h, g.reshape(1, D), wr)


def _moe_ffn_kernel(te_ref, nv_ref, x_ref, w1_ref, w3_ref, w2_ref, o_ref):
    n = pl.program_id(0)
    f = pl.program_id(1)

    @pl.when(jnp.logical_and(n >= nv_ref[0], f == 0))
    def _():
        o_ref[...] = jnp.zeros(o_ref.shape, o_ref.dtype)

    @pl.when(n < nv_ref[0])
    def _():
        x = x_ref[...]
        a = jnp.dot(x, w1_ref[...], preferred_element_type=_F32)
        b = jnp.dot(x, w3_ref[...], preferred_element_type=_F32)
        mid = (a * _sigmoid(a) * b).astype(_BF16)
        contrib = jnp.dot(mid, w2_ref[...], preferred_element_type=_F32)

        @pl.when(f == 0)
        def _():
            o_ref[...] = contrib

        @pl.when(f > 0)
        def _():
            o_ref[...] += contrib


def moe_grouped_ffn(xs, tile_expert, n_valid, w1, w3, w2):
    R, D = xs.shape
    F = w1.shape[2]
    nf = F // MOE_TF

    def f_idx(n, f, nv):
        return jnp.where(n < nv[0], f, nf - 1)

    return pl.pallas_call(
        _moe_ffn_kernel,
        grid_spec=pltpu.PrefetchScalarGridSpec(
            num_scalar_prefetch=2,
            grid=(R // MOE_TM, nf),
            in_specs=[pl.BlockSpec((MOE_TM, D), lambda n, f, te, nv: (n, 0)),
                      pl.BlockSpec((None, D, MOE_TF), lambda n, f, te, nv: (te[n], 0, f_idx(n, f, nv))),
                      pl.BlockSpec((None, D, MOE_TF), lambda n, f, te, nv: (te[n], 0, f_idx(n, f, nv))),
                      pl.BlockSpec((None, MOE_TF, D), lambda n, f, te, nv: (te[n], f_idx(n, f, nv), 0))],
            out_specs=pl.BlockSpec((MOE_TM, D), lambda n, f, te, nv: (n, 0))),
        out_shape=jax.ShapeDtypeStruct((R, D), _F32),
        compiler_params=_params("parallel", "arbitrary"),
        name="moe_grouped_ffn",
    )(tile_expert, n_valid, xs, w1, w3, w2)


def moe_residual(h, g, router, w1, w3, w2):
    T, D = h.shape
    E = N_EXPERTS
    xn, route = moe_route(h, g, router, tm=ROUTE_TM)
    top = route[:, 0:TOP_K_EXPERTS].astype(jnp.int32)
    gates = route[:, TOP_K_EXPERTS:2 * TOP_K_EXPERTS]
    ef = top.T.reshape(TOP_K_EXPERTS * T)
    onehot = (ef[:, None] == jnp.arange(E, dtype=jnp.int32)[None, :]).astype(jnp.int32)
    csum = jnp.cumsum(onehot, axis=0)
    rank = jnp.take_along_axis(csum, ef[:, None], axis=1)[:, 0] - 1
    counts = csum[-1]
    padded = ((counts + MOE_TM - 1) // MOE_TM) * MOE_TM
    pends = jnp.cumsum(padded)
    pstarts = pends - padded
    starts = jnp.cumsum(counts) - counts
    pos = pstarts[ef] + rank
    R = TOP_K_EXPERTS * T + E * MOE_TM
    n_tiles = R // MOE_TM
    tile_expert = jnp.searchsorted(pends, jnp.arange(n_tiles, dtype=jnp.int32) * MOE_TM, side='right')
    tile_expert = jnp.minimum(tile_expert, E - 1).astype(jnp.int32)
    n_valid = (pends[-1:] // MOE_TM).astype(jnp.int32)
    order = jnp.argsort(ef, stable=True).astype(jnp.int32)
    rho = jnp.arange(R, dtype=jnp.int32)
    e_row = tile_expert[rho // MOE_TM]
    off = rho - pstarts[e_row]
    src = order[jnp.clip(starts[e_row] + off, 0, TOP_K_EXPERTS * T - 1)] % T
    xs = jnp.take(xn, src, axis=0)
    ys = moe_grouped_ffn(xs, tile_expert, n_valid, w1, w3, w2)
    mix = sum(gates[:, slot:slot + 1] * jnp.take(ys, pos[slot * T:(slot + 1) * T], axis=0)
              for slot in range(TOP_K_EXPERTS))
    return h + mix


def _pack_w_in(w):
    wq, wk, wv, wqi, wki, wwi, wslab, wga, wgb = _split(w, IN_SIZES)
    D = w.shape[0]
    wb = jnp.concatenate([wq * (ATT_HEAD_DIM ** -0.5 * LOG2_E), wk, wv, wqi, wki, wki], axis=1)
    wf = jnp.concatenate([wslab, jnp.zeros((D, ZF_SLAB_PAD - RWKV_SLAB), w.dtype),
                          wki, wwi, jnp.zeros((D, LANES - IDX_DIM - IDX_HEADS), w.dtype), wga, wgb], axis=1)
    return wb.astype(_BF16), wf.astype(_BF16)


def kernel(x, p, w_in, att_up, rwkv_up, w_out, rel_bias, rwkv_mu, rwkv_w0, rwkv_w_up,
           rwkv_a0, rwkv_a_up, rwkv_g_up, rwkv_k_k, rwkv_k_a, rwkv_r_k, rwkv_gn_w, rwkv_gn_b,
           norm_mix, norm_ffn, norm_ple, ple_proj, ple_gate, ffn_w1, ffn_w3, ffn_w2,
           moe_router, moe_w1, moe_w3, moe_w2, final_norm):
    B, S, D = x.shape
    T = B * S
    depth = w_in.shape[0]
    h = x.reshape(T, D)
    for i in range(depth):
        wb, wf = _pack_w_in(w_in[i])
        zb = norm_matmul(h, norm_mix[i], wb, _BF16, tm=IN_PROJ_TM, tn=ZB_WIDTH)
        zf = norm_matmul(h, norm_mix[i], wf, _F32, tm=IN_PROJ_F32_TM, tn=IN_PROJ_F32_TN)
        att = dsa_attention(zb, zf, rel_bias, B, S)
        rw = rwkv7_mix(zf, rwkv_mu[i], rwkv_w0[i], rwkv_w_up[i], rwkv_a0[i], rwkv_a_up[i], rwkv_g_up[i],
                       rwkv_k_k[i], rwkv_k_a[i], rwkv_r_k[i], rwkv_gn_w[i], rwkv_gn_b[i], B, S)
        merged = gated_merge(att, rw, zf, att_up[i].astype(_BF16), rwkv_up[i].astype(_BF16),
                             tm=MERGE_TM, tn=MERGE_TN)
        h = matmul_residual(merged, w_out[i].astype(_BF16), h, tm=OUT_PROJ_TM, tn=D)
        if i % 2 == 0:
            j = i // 2
            h = ffn_residual(h, norm_ffn[i], ffn_w1[j].astype(_BF16), ffn_w3[j].astype(_BF16),
                             ffn_w2[j].astype(_BF16), tm=FFN_TM, tf=FFN_TF)
        else:
            j = i // 2
            h = moe_residual(h, norm_ffn[i], moe_router[j], cast_bf16(moe_w1[j]), cast_bf16(moe_w3[j]),
                             cast_bf16(moe_w2[j]))
        h = ple_residual(h, norm_ple[i], p[i].reshape(T, PLE_DIM), ple_gate[i].astype(_BF16),
                         ple_proj[i].astype(_BF16), final_norm, final=(i == depth - 1), tm=PLE_TM)
    return h.reshape(B, S, D)
```

```python
import functools
import math

import jax
import jax.numpy as jnp
import numpy as np
from jax import lax
from jax.experimental import pallas as pl
from jax.experimental.pallas import tpu as pltpu

D_MODEL = 2048
DEPTH = 2
ATT_HEADS = 8
ATT_KV_HEADS = 2
ATT_HEAD_DIM = 128
IDX_HEADS = 8
IDX_DIM = 64
TOPK_MAX = 256
REL_BUCKETS = 32
REL_MAX_DIST = 128
RWKV_HEAD = 64
RWKV_HEADS = 16
DECAY_RANK = 64
ICLR_RANK = 64
GATE_RANK = 160
D_FF = 5632
N_EXPERTS = 8
TOP_K_EXPERTS = 2
D_FF_EXPERT = 7168
PLE_DIM = 256
RMS_EPS = 1e-6
GN_EPS = 64e-5

ATT_WIDTH = ATT_HEADS * ATT_HEAD_DIM
KV_WIDTH = ATT_KV_HEADS * ATT_HEAD_DIM
RWKV_WIDTH = RWKV_HEADS * RWKV_HEAD
RWKV_SLAB = 3 * RWKV_WIDTH + DECAY_RANK + ICLR_RANK + GATE_RANK
IN_SIZES = (ATT_WIDTH, KV_WIDTH, KV_WIDTH, IDX_HEADS * IDX_DIM, IDX_DIM, IDX_HEADS, RWKV_SLAB, D_MODEL, D_MODEL)

LANES = 128
SUBLANES = 8
VMEM_LIMIT = 56 * 1024 * 1024
CAST_BLOCK_BYTES = 8 * 1024 * 1024

IN_PROJ_TM = 1024
IN_PROJ_F32_TM = 1024
IN_PROJ_F32_TN = 1536
MERGE_TM, MERGE_TN = 1024, 512
OUT_PROJ_TM = 512
FFN_TM, FFN_TF = 512, 512
ROUTE_TM = 512
PLE_TM = 512

Q_BLOCK = 128
KEY_CHUNK = 512
BLOCKS_PER_CHUNK = KEY_CHUNK // Q_BLOCK
BLOCKS_PER_GROUP = 32
MASKED_LOGIT = -1e30
LOG2_E = math.log2(math.e)
GQA_GROUP = ATT_HEADS // ATT_KV_HEADS

ZB_WIDTH = ATT_WIDTH + 2 * KV_WIDTH + IDX_HEADS * IDX_DIM + 2 * IDX_DIM
ZF_SLAB_PAD = 3456
ZF_WIDX = ZF_SLAB_PAD
ZF_GATE_A = ZF_WIDX + LANES
ZF_GATE_B = ZF_GATE_A + D_MODEL
ZF_WIDTH = ZF_GATE_B + D_MODEL

RWKV_CHUNK = 64
HEAD_PAIRS = RWKV_HEADS // 2
MOE_TM = 512
MOE_TF = 1024

_F32 = jnp.float32
_BF16 = jnp.bfloat16
_NT = (((1,), (1,)), ((), ()))
_TN = (((0,), (0,)), ((), ()))


def _split(z, sizes):
    return jnp.split(z, np.cumsum(sizes)[:-1].tolist(), axis=-1)


def _sigmoid(x):
    return 1.0 / (1.0 + jnp.exp(-x))


def _rms_to_bf16(x, g):
    y = x * lax.rsqrt(jnp.mean(x * x, axis=-1, keepdims=True) + RMS_EPS)
    return (y * g).astype(_BF16)


def _params(*sem):
    return pltpu.CompilerParams(dimension_semantics=sem, vmem_limit_bytes=VMEM_LIMIT)


def _norm_mm_kernel(x_ref, g_ref, w_ref, o_ref, xn_ref):
    @pl.when(pl.program_id(1) == 0)
    def _():
        xn_ref[...] = _rms_to_bf16(x_ref[...], g_ref[...])

    o_ref[...] = jnp.dot(xn_ref[...], w_ref[...], preferred_element_type=_F32).astype(o_ref.dtype)


def norm_matmul(x, g, w, out_dtype, tm, tn):
    T, D = x.shape
    N = w.shape[1]
    tm = min(tm, T)
    return pl.pallas_call(
        _norm_mm_kernel,
        grid=(T // tm, N // tn),
        in_specs=[pl.BlockSpec((tm, D), lambda i, j: (i, 0)),
                  pl.BlockSpec((1, D), lambda i, j: (0, 0)),
                  pl.BlockSpec((D, tn), lambda i, j: (0, j))],
        out_specs=pl.BlockSpec((tm, tn), lambda i, j: (i, j)),
        out_shape=jax.ShapeDtypeStruct((T, N), out_dtype),
        scratch_shapes=[pltpu.VMEM((tm, D), _BF16)],
        compiler_params=_params("parallel", "arbitrary"),
        name="norm_matmul",
    )(x, g.reshape(1, D), w)


def _merge_kernel(att_ref, rw_ref, ga_ref, gb_ref, wa_ref, wr_ref, o_ref):
    a = jnp.dot(att_ref[...], wa_ref[...], preferred_element_type=_F32)
    r = jnp.dot(rw_ref[...], wr_ref[...], preferred_element_type=_F32)
    o_ref[...] = (_sigmoid(ga_ref[...]) * a + _sigmoid(gb_ref[...]) * r).astype(o_ref.dtype)


def gated_merge(att, rw, zf, wa, wr, tm, tn):
    T, K = att.shape
    N = wa.shape[1]
    assert ZF_GATE_A % tn == 0 and ZF_GATE_B % tn == 0
    ja, jb = ZF_GATE_A // tn, ZF_GATE_B // tn
    return pl.pallas_call(
        _merge_kernel,
        grid=(T // tm, N // tn),
        in_specs=[pl.BlockSpec((tm, K), lambda i, j: (i, 0)),
                  pl.BlockSpec((tm, K), lambda i, j: (i, 0)),
                  pl.BlockSpec((tm, tn), lambda i, j: (i, ja + j)),
                  pl.BlockSpec((tm, tn), lambda i, j: (i, jb + j)),
                  pl.BlockSpec((K, tn), lambda i, j: (0, j)),
                  pl.BlockSpec((K, tn), lambda i, j: (0, j))],
        out_specs=pl.BlockSpec((tm, tn), lambda i, j: (i, j)),
        out_shape=jax.ShapeDtypeStruct((T, N), _BF16),
        compiler_params=_params("parallel", "arbitrary"),
        name="gated_merge",
    )(att, rw, zf, zf, wa, wr)


def _mm_res_kernel(x_ref, w_ref, h_ref, o_ref):
    o_ref[...] = h_ref[...] + jnp.dot(x_ref[...], w_ref[...], preferred_element_type=_F32)


def matmul_residual(x, w, h, tm, tn):
    T, K = x.shape
    N = w.shape[1]
    return pl.pallas_call(
        _mm_res_kernel,
        grid=(T // tm, N // tn),
        in_specs=[pl.BlockSpec((tm, K), lambda i, j: (i, 0)),
                  pl.BlockSpec((K, tn), lambda i, j: (0, j)),
                  pl.BlockSpec((tm, tn), lambda i, j: (i, j))],
        out_specs=pl.BlockSpec((tm, tn), lambda i, j: (i, j)),
        out_shape=jax.ShapeDtypeStruct((T, N), _F32),
        compiler_params=_params("parallel", "arbitrary"),
        name="matmul_residual",
    )(x, w, h)


def _ffn_kernel(h_ref, g_ref, w1_ref, w3_ref, w2_ref, o_ref, xn_ref):
    f = pl.program_id(1)

    @pl.when(f == 0)
    def _():
        xn_ref[...] = _rms_to_bf16(h_ref[...], g_ref[...])
        o_ref[...] = h_ref[...]

    xn = xn_ref[...]
    a = jnp.dot(xn, w1_ref[...], preferred_element_type=_F32)
    b = jnp.dot(xn, w3_ref[...], preferred_element_type=_F32)
    mid = (a * _sigmoid(a) * b).astype(_BF16)
    o_ref[...] += jnp.dot(mid, w2_ref[...], preferred_element_type=_F32)


def ffn_residual(h, g, w1, w3, w2, tm, tf):
    T, D = h.shape
    F = w1.shape[1]
    return pl.pallas_call(
        _ffn_kernel,
        grid=(T // tm, F // tf),
        in_specs=[pl.BlockSpec((tm, D), lambda i, f: (i, 0)),
                  pl.BlockSpec((1, D), lambda i, f: (0, 0)),
                  pl.BlockSpec((D, tf), lambda i, f: (0, f)),
                  pl.BlockSpec((D, tf), lambda i, f: (0, f)),
                  pl.BlockSpec((tf, D), lambda i, f: (f, 0))],
        out_specs=pl.BlockSpec((tm, D), lambda i, f: (i, 0)),
        out_shape=jax.ShapeDtypeStruct((T, D), _F32),
        scratch_shapes=[pltpu.VMEM((tm, D), _BF16)],
        compiler_params=_params("parallel", "arbitrary"),
        name="ffn_residual",
    )(h, g.reshape(1, D), w1, w3, w2)


def _ple_kernel(h_ref, g_ref, p_ref, wg_ref, wp_ref, fg_ref, o_ref, *, final):
    h = h_ref[...]
    xn = _rms_to_bf16(h, g_ref[...])
    gate = _sigmoid(jnp.dot(xn, wg_ref[...], preferred_element_type=_F32))
    pp = jnp.dot(p_ref[...].astype(_BF16), wp_ref[...], preferred_element_type=_F32)
    out = h + gate * pp
    if final:
        out = out * lax.rsqrt(jnp.mean(out * out, axis=-1, keepdims=True) + RMS_EPS) * fg_ref[...]
    o_ref[...] = out


def ple_residual(h, g, p, wg, wp, final_g, final, tm):
    T, D = h.shape
    P = p.shape[1]
    return pl.pallas_call(
        functools.partial(_ple_kernel, final=final),
        grid=(T // tm,),
        in_specs=[pl.BlockSpec((tm, D), lambda i: (i, 0)),
                  pl.BlockSpec((1, D), lambda i: (0, 0)),
                  pl.BlockSpec((tm, P), lambda i: (i, 0)),
                  pl.BlockSpec((D, D), lambda i: (0, 0)),
                  pl.BlockSpec((P, D), lambda i: (0, 0)),
                  pl.BlockSpec((1, D), lambda i: (0, 0))],
        out_specs=pl.BlockSpec((tm, D), lambda i: (i, 0)),
        out_shape=jax.ShapeDtypeStruct((T, D), _F32),
        compiler_params=_params("parallel"),
        name="ple_residual",
    )(h, g.reshape(1, D), p, wg, wp, final_g.reshape(1, D))


def _cast_kernel(x_ref, o_ref):
    o_ref[...] = x_ref[...].astype(o_ref.dtype)


def cast_bf16(w):
    E, R, N = w.shape
    rows = R
    while rows * N * 4 > CAST_BLOCK_BYTES and rows % 2 == 0:
        rows //= 2
    return pl.pallas_call(
        _cast_kernel,
        grid=(E, R // rows),
        in_specs=[pl.BlockSpec((None, rows, N), lambda e, i: (e, i, 0))],
        out_specs=pl.BlockSpec((None, rows, N), lambda e, i: (e, i, 0)),
        out_shape=jax.ShapeDtypeStruct(w.shape, _BF16),
        compiler_params=_params("parallel", "parallel"),
        name="cast_bf16",
    )(w)


def rel_bucket(dist):
    max_exact = REL_BUCKETS // 2
    dist = jnp.maximum(dist, 0)
    d_f = jnp.maximum(dist, 1).astype(_F32)
    large = max_exact + (jnp.log(d_f / max_exact) / math.log(REL_MAX_DIST / max_exact)
                         * (REL_BUCKETS - max_exact)).astype(jnp.int32)
    large = jnp.minimum(large, REL_BUCKETS - 1)
    return jnp.where(dist < max_exact, dist, large)


def _sortable(x):
    b = lax.bitcast_convert_type(x, jnp.int32)
    return b ^ ((b >> 31) & jnp.int32(0x7FFFFFFF))


_INT_MIN = -2 ** 31
_KEY_NEG_INF = 0x807FFFFF - 2 ** 32


def _dsa_kernel(q_ref, qi_ref, wi_ref, k_ref, v_ref, ki_ref, bprev_ref, bdiag_ref, bfar_ref, o_ref,
                keys_ref, planes_ref, active_ref, qs_ref, qim_ref, m_ref, acc_ref, *, n_sel, idx_bits):
    n_grp = planes_ref.shape[0]
    qb = pl.program_id(1)
    t0 = qb * Q_BLOCK
    n_chunks = qb // BLOCKS_PER_CHUNK + 1
    t_idx = t0 + lax.broadcasted_iota(jnp.int32, (Q_BLOCK, 1), 0)
    lane_blk = lax.broadcasted_iota(jnp.int32, (1, Q_BLOCK), 1)
    lane_chunk = lax.broadcasted_iota(jnp.int32, (1, KEY_CHUNK), 1)

    for hh in range(ATT_HEADS):
        g, h = divmod(hh, GQA_GROUP)
        qs_ref[g, h * Q_BLOCK:(h + 1) * Q_BLOCK, :] = q_ref[:, hh * ATT_HEAD_DIM:(hh + 1) * ATT_HEAD_DIM]
    low = lax.broadcasted_iota(jnp.int32, (Q_BLOCK, LANES), 1) < IDX_DIM
    for hh in range(IDX_HEADS):
        tile = qi_ref[:, (hh // 2) * LANES:(hh // 2 + 1) * LANES]
        keep = low if hh % 2 == 0 else jnp.logical_not(low)
        qim_ref[hh] = jnp.where(keep, tile, jnp.zeros_like(tile))
    m_ref[...] = jnp.full(m_ref.shape, MASKED_LOGIT, _F32)

    @pl.when(qb == 0)
    def _():
        keys_ref[BLOCKS_PER_CHUNK:] = jnp.zeros((keys_ref.shape[0] - BLOCKS_PER_CHUNK, Q_BLOCK, Q_BLOCK), jnp.int32)
        planes_ref[...] = jnp.zeros(planes_ref.shape, jnp.int32)
    acc_ref[...] = jnp.zeros(acc_ref.shape, _F32)

    w_all = wi_ref[...]
    idx_scale = (IDX_HEADS ** -0.5) * (IDX_DIM ** -0.5)

    def score_body(c, carry):
        ks = pl.multiple_of(c * KEY_CHUNK, KEY_CHUNK)
        kic = ki_ref[pl.ds(ks, KEY_CHUNK), :]
        d = lax.dot_general(qim_ref[...].reshape(IDX_HEADS * Q_BLOCK, LANES), kic, _NT,
                            preferred_element_type=_F32)
        sc = jnp.zeros((Q_BLOCK, KEY_CHUNK), _F32)
        for hh in range(IDX_HEADS):
            sc = sc + w_all[:, IDX_DIM + hh:IDX_DIM + hh + 1] * jnp.maximum(d[hh * Q_BLOCK:(hh + 1) * Q_BLOCK], 0.0)
        sc = sc * idx_scale
        sc = jnp.where(sc == 0.0, 0.0, sc)
        key = jnp.where(ks + lane_chunk <= t_idx, _sortable(sc), _KEY_NEG_INF)
        for j in range(BLOCKS_PER_CHUNK):
            keys_ref[c * BLOCKS_PER_CHUNK + j] = key[:, j * Q_BLOCK:(j + 1) * Q_BLOCK]
        return carry

    lax.fori_loop(0, n_chunks, score_body, 0)

    def count(pred):
        def body(c, acc):
            for j in range(BLOCKS_PER_CHUNK):
                blk = c * BLOCKS_PER_CHUNK + j
                acc = acc + jnp.where(pred(keys_ref[blk], blk * Q_BLOCK + lane_blk), 1.0, 0.0)
            return acc

        acc = lax.fori_loop(0, n_chunks, body, jnp.zeros((Q_BLOCK, Q_BLOCK), _F32))
        return jnp.sum(acc, axis=1, keepdims=True)

    n_blocks = n_chunks * BLOCKS_PER_CHUNK
    n_groups = (n_blocks + BLOCKS_PER_GROUP - 1) // BLOCKS_PER_GROUP

    def transpose_group(g, carry):
        def rows(rg, carry_rows):
            r0 = pl.multiple_of(rg * SUBLANES, SUBLANES)
            x = [keys_ref[g * BLOCKS_PER_GROUP + j, pl.ds(r0, SUBLANES), :] for j in range(BLOCKS_PER_GROUP)]
            j, m = 16, 0x0000FFFF
            while j:
                k = 0
                while k < BLOCKS_PER_GROUP:
                    t = (x[k] ^ lax.shift_right_logical(x[k + j], jnp.int32(j))) & jnp.int32(m)
                    x[k] = x[k] ^ t
                    x[k + j] = x[k + j] ^ lax.shift_left(t, jnp.int32(j))
                    k = (k + j + 1) & ~j
                j >>= 1
                m ^= m << j
            planes_ref[g, 0, pl.ds(r0, SUBLANES), :] = ~x[0]
            for i in range(1, 32):
                planes_ref[g, i, pl.ds(r0, SUBLANES), :] = x[i]
            return carry_rows

        lax.fori_loop(0, Q_BLOCK // SUBLANES, rows, 0)
        n_live = jnp.minimum(n_blocks - g * BLOCKS_PER_GROUP, BLOCKS_PER_GROUP)
        active_ref[g] = lax.shift_left(jnp.full((Q_BLOCK, Q_BLOCK), -1, jnp.int32), BLOCKS_PER_GROUP - n_live)
        return carry

    active_ref[...] = jnp.zeros(active_ref.shape, jnp.int32)
    lax.fori_loop(0, n_groups, transpose_group, 0)

    def lane_total(words):
        return jnp.sum(sum(lax.population_count(w) for w in words).astype(_F32), axis=1, keepdims=True)

    def split_active(g, i):
        act = active_ref[g]
        hi = act & planes_ref[g, 2 * i]
        lo = act ^ hi
        hi1 = hi & planes_ref[g, 2 * i + 1]
        lo1 = lo & planes_ref[g, 2 * i + 1]
        return hi1, hi ^ hi1, lo1, lo ^ lo1

    def radix_pass(i, carry):
        prefix, above = carry
        parts = [split_active(g, i) for g in range(n_grp)]
        c11, c10, c01 = (lane_total([parts[g][q] for g in range(n_grp)]) for q in range(3))
        upto10 = above + c11 + c10
        t11 = above + c11 >= n_sel
        t10 = upto10 >= n_sel
        t01 = upto10 + c01 >= n_sel
        for g in range(n_grp):
            s11, s10, s01, s00 = split_active(g, i)
            active_ref[g] = jnp.where(t11, s11, jnp.where(t10, s10, jnp.where(t01, s01, s00)))
        bit1 = jnp.where(t10, lax.shift_left(jnp.int32(1), 31 - 2 * i), 0)
        bit0 = jnp.where(jnp.logical_or(t11, jnp.logical_and(jnp.logical_not(t10), t01)),
                         lax.shift_left(jnp.int32(1), 30 - 2 * i), 0)
        above = jnp.where(t11, above, jnp.where(t10, above + c11, jnp.where(t01, upto10, upto10 + c01)))
        return prefix | bit1 | bit0, above

    prefix, n_gt = lax.fori_loop(0, 16, radix_pass, (jnp.zeros((Q_BLOCK, 1), jnp.int32),
                                                     jnp.zeros((Q_BLOCK, 1), _F32)))
    tau = prefix ^ jnp.int32(_INT_MIN)
    n_eq = lane_total([active_ref[g] for g in range(n_grp)])
    tied = jnp.logical_and(n_gt + n_eq > n_sel, tau > _KEY_NEG_INF)

    def tie_search():
        need = n_sel - count(lambda key, sidx: key > tau)

        def index_bit(i, x):
            cand = x | lax.shift_left(jnp.int32(1), idx_bits - 1 - i)
            cnt = count(lambda key, sidx: jnp.logical_and(key == tau, sidx < cand))
            return jnp.where(cnt < need, cand, x)

        return lax.fori_loop(0, idx_bits, index_bit, jnp.zeros((Q_BLOCK, 1), jnp.int32))

    last_tie = lax.cond(jnp.max(jnp.where(tied, 1.0, 0.0)) > 0.0, tie_search,
                        lambda: jnp.full((Q_BLOCK, 1), 2 ** idx_bits, jnp.int32))

    def attend(key, k_tile, v_tile, s0, lane, bias_ref, enable=None, const_bias=False):
        width = key.shape[1]
        sidx = s0 + lane
        sel = jnp.logical_or(key > tau, jnp.logical_and(key == tau, sidx <= last_tie))
        sel = jnp.logical_and(sel, key > _KEY_NEG_INF)
        if enable is not None:
            sel = jnp.logical_and(sel, jnp.zeros_like(sidx) + enable > 0)
        ones = jnp.ones((width, ATT_HEAD_DIM), _BF16)
        groups = range(ATT_KV_HEADS)
        cols = [slice(g * ATT_HEAD_DIM, (g + 1) * ATT_HEAD_DIM) for g in groups]

        def logits(g):
            return lax.dot_general(qs_ref[g], k_tile[:, cols[g]], _NT,
                                   preferred_element_type=_F32).reshape(GQA_GROUP, Q_BLOCK, width)

        def softmax(g, s):
            bias = bias_ref[g * GQA_GROUP:(g + 1) * GQA_GROUP]
            m_old = m_ref[g]
            if const_bias:
                sh = jnp.where(sel[None], s, MASKED_LOGIT)
                m_new = jnp.maximum(m_old, jnp.max(sh, axis=2, keepdims=True) + bias)
                shift = m_new - bias
            else:
                sh = jnp.where(sel[None], s + bias, MASKED_LOGIT)
                m_new = jnp.maximum(m_old, jnp.max(sh, axis=2, keepdims=True))
                shift = m_new
            m_ref[g] = m_new
            return (jnp.exp2(sh - shift).reshape(GQA_GROUP * Q_BLOCK, width).astype(_BF16),
                    jnp.exp2(m_old - m_new).reshape(GQA_GROUP * Q_BLOCK, 1))

        def accumulate(g, p, alpha):
            pv = jnp.dot(p, jnp.concatenate([v_tile[:, cols[g]], ones], axis=1), preferred_element_type=_F32)
            acc_ref[g] = acc_ref[g] * alpha + pv

        if width <= Q_BLOCK:
            s = [logits(g) for g in groups]
            pa = [softmax(g, s[g]) for g in groups]
            for g in groups:
                accumulate(g, *pa[g])
        else:
            for g in groups:
                accumulate(g, *softmax(g, logits(g)))

    n_far = jnp.maximum(qb - 1, 0) // BLOCKS_PER_CHUNK

    def far_body(c, carry):
        ks = pl.multiple_of(c * KEY_CHUNK, KEY_CHUNK)
        key = jnp.concatenate([keys_ref[c * BLOCKS_PER_CHUNK + j] for j in range(BLOCKS_PER_CHUNK)], axis=1)
        attend(key, k_ref[pl.ds(ks, KEY_CHUNK), :], v_ref[pl.ds(ks, KEY_CHUNK), :], ks, lane_chunk, bfar_ref,
               const_bias=True)
        return carry

    lax.fori_loop(0, n_far, far_body, 0)

    def block_step(j, bias_ref, enable=None, const_bias=False):
        ks = pl.multiple_of(j * Q_BLOCK, Q_BLOCK)
        attend(keys_ref[j], k_ref[pl.ds(ks, Q_BLOCK), :], v_ref[pl.ds(ks, Q_BLOCK), :], ks, lane_blk, bias_ref,
               enable, const_bias)

    def tail_body(j, carry):
        block_step(j, bfar_ref, const_bias=True)
        return carry

    lax.fori_loop(n_far * BLOCKS_PER_CHUNK, qb - 1, tail_body, 0)
    block_step(jnp.maximum(qb - 1, 0), bprev_ref, enable=jnp.minimum(qb, 1))
    block_step(qb, bdiag_ref)

    for hh in range(ATT_HEADS):
        g, h = divmod(hh, GQA_GROUP)
        rows = slice(h * Q_BLOCK, (h + 1) * Q_BLOCK)
        o_ref[:, hh * ATT_HEAD_DIM:(hh + 1) * ATT_HEAD_DIM] = (
            acc_ref[g, rows, 0:ATT_HEAD_DIM] / acc_ref[g, rows, ATT_HEAD_DIM:2 * ATT_HEAD_DIM]).astype(o_ref.dtype)


def dsa_attention(zb, zf, rel_bias, B, S):
    nq = S // Q_BLOCK
    n_grp = -(-nq // BLOCKS_PER_GROUP)
    n_sel = min(TOPK_MAX, S // 4)
    tl = jnp.arange(Q_BLOCK, dtype=jnp.int32)[:, None]
    sr = jnp.arange(Q_BLOCK, dtype=jnp.int32)[None, :]
    rel_bias = rel_bias.astype(_F32) * LOG2_E

    def bias_table(dist):
        hit = rel_bucket(dist)[None, :, :, None] == jnp.arange(REL_BUCKETS, dtype=jnp.int32)
        return jnp.sum(jnp.where(hit, rel_bias.T[:, None, None, :], 0.0), axis=-1)

    bprev = bias_table(tl + Q_BLOCK - sr)
    bdiag = bias_table(tl - sr)
    bfar = rel_bias[rel_bucket(jnp.int32(REL_MAX_DIST + 1))].reshape(ATT_HEADS, 1, 1)
    kernel = functools.partial(_dsa_kernel, n_sel=float(n_sel), idx_bits=int(S - 1).bit_length())
    c_k = ATT_WIDTH // KV_WIDTH
    c_qi = (ATT_WIDTH + 2 * KV_WIDTH) // (IDX_HEADS * IDX_DIM)
    c_ki = (ATT_WIDTH + 2 * KV_WIDTH + IDX_HEADS * IDX_DIM) // LANES
    return pl.pallas_call(
        kernel,
        grid=(B, nq),
        in_specs=[pl.BlockSpec((Q_BLOCK, ATT_WIDTH), lambda b, i: (b * nq + i, 0)),
                  pl.BlockSpec((Q_BLOCK, IDX_HEADS * IDX_DIM), lambda b, i: (b * nq + i, c_qi)),
                  pl.BlockSpec((Q_BLOCK, LANES), lambda b, i: (b * nq + i, ZF_WIDX // LANES)),
                  pl.BlockSpec((S, KV_WIDTH), lambda b, i: (b, c_k)),
                  pl.BlockSpec((S, KV_WIDTH), lambda b, i: (b, c_k + 1)),
                  pl.BlockSpec((S, LANES), lambda b, i: (b, c_ki)),
                  pl.BlockSpec((ATT_HEADS, Q_BLOCK, Q_BLOCK), lambda b, i: (0, 0, 0)),
                  pl.BlockSpec((ATT_HEADS, Q_BLOCK, Q_BLOCK), lambda b, i: (0, 0, 0)),
                  pl.BlockSpec((ATT_HEADS, 1, 1), lambda b, i: (0, 0, 0))],
        out_specs=pl.BlockSpec((Q_BLOCK, ATT_WIDTH), lambda b, i: (b * nq + i, 0)),
        out_shape=jax.ShapeDtypeStruct((B * S, ATT_WIDTH), _BF16),
        scratch_shapes=[pltpu.VMEM((n_grp * BLOCKS_PER_GROUP, Q_BLOCK, Q_BLOCK), jnp.int32),
                        pltpu.VMEM((n_grp, 32, Q_BLOCK, Q_BLOCK), jnp.int32),
                        pltpu.VMEM((n_grp, Q_BLOCK, Q_BLOCK), jnp.int32),
                        pltpu.VMEM((ATT_KV_HEADS, GQA_GROUP * Q_BLOCK, ATT_HEAD_DIM), _BF16),
                        pltpu.VMEM((IDX_HEADS, Q_BLOCK, LANES), _BF16),
                        pltpu.VMEM((ATT_KV_HEADS, GQA_GROUP, Q_BLOCK, 1), _F32),
                        pltpu.VMEM((ATT_KV_HEADS, GQA_GROUP * Q_BLOCK, 2 * ATT_HEAD_DIM), _F32)],
        compiler_params=_params("parallel", "arbitrary"),
        name="dsa_attention",
    )(zb, zb, zf, zb, zb, zb, bprev, bdiag, bfar)


def _split_dot(x, w):
    hi = x.astype(_BF16)
    lo = (x - hi.astype(_F32)).astype(_BF16)
    return jnp.dot(hi, w, preferred_element_type=_F32) + jnp.dot(lo, w, preferred_element_type=_F32)


def _split_dot_left(w, x):
    hi = x.astype(_BF16)
    lo = (x - hi.astype(_F32)).astype(_BF16)
    return jnp.dot(w, hi, preferred_element_type=_F32) + jnp.dot(w, lo, preferred_element_type=_F32)


def _rwkv_kernel(slab_ref, mu_ref, vec_ref, wup_ref, aup_ref, gup_ref, o_ref, xs_ref, state_ref):
    C = RWKV_CHUNK
    W = RWKV_WIDTH
    n_batch = slab_ref.shape[0]
    c = pl.program_id(0)

    @pl.when(c == 0)
    def _():
        xs_ref[:, 0:SUBLANES, :] = jnp.zeros((n_batch, SUBLANES, ZF_SLAB_PAD), _F32)
        state_ref[...] = jnp.zeros(state_ref.shape, _F32)

    w0, a0, k_k, k_a, r_k, gn_w, gn_b = (vec_ref[i:i + 1, :] for i in range(7))
    ti = lax.broadcasted_iota(jnp.int32, (C, C), 0)
    tj = lax.broadcasted_iota(jnp.int32, (C, C), 1)
    tri = jnp.where(tj <= ti, 1.0, 0.0).astype(_BF16)

    def prepare(bi):
        slab = slab_ref[bi]
        xs_ref[bi, SUBLANES:SUBLANES + C, :] = slab
        shifted = xs_ref[bi, SUBLANES - 1:SUBLANES - 1 + C, :]
        xs = slab + (shifted - slab) * mu_ref[...]
        xs_ref[bi, 0:SUBLANES, :] = slab[C - SUBLANES:C, :]

        r = xs[:, 0:W]
        k = xs[:, W:2 * W]
        lora = xs[:, 3 * W:3 * W + LANES]
        xg = xs[:, 3 * W + LANES:3 * W + 3 * LANES]
        d = w0 + jnp.dot(jnp.tanh(lora).astype(_BF16), wup_ref[...], preferred_element_type=_F32)
        logw = -math.exp(-0.5) * _sigmoid(d)
        a = _sigmoid(a0 + jnp.dot(lora.astype(_BF16), aup_ref[...], preferred_element_type=_F32))
        kt = k * (1.0 + (a - 1.0) * k_a)
        cum = _split_dot_left(tri, logw)
        cum_end = cum[C - 1:C, :]
        return dict(r=r, v=xs[:, 2 * W:3 * W], a=a, kt=kt, kk_raw=k * k_k, bonus_raw=r * kt * r_k,
                    g=jnp.dot(_sigmoid(xg).astype(_BF16), gup_ref[...], preferred_element_type=_F32),
                    g_incl=jnp.exp(cum), g_excl=jnp.exp(cum - logw), g_inv=jnp.exp(-cum),
                    g_end=jnp.exp(cum_end - cum), g_total=jnp.exp(cum_end))

    prepared = [prepare(bi) for bi in range(n_batch)]

    lo = lax.broadcasted_iota(jnp.int32, (C, LANES), 1) < RWKV_HEAD
    ri = lax.broadcasted_iota(jnp.int32, (LANES, LANES), 0)
    ci = lax.broadcasted_iota(jnp.int32, (LANES, LANES), 1)
    same_head = (ri // RWKV_HEAD) == (ci // RWKV_HEAD)
    ones_bd = jnp.where(same_head, 1.0, 0.0).astype(_BF16)
    eye = ri == ci
    same_blk = (ri // C) == (ci // C)
    strict = jnp.logical_and(same_blk, ci < ri)
    incl = jnp.logical_and(same_blk, ci <= ri)

    def stack(x):
        return jnp.concatenate([jnp.where(lo, x, 0.0), jnp.where(lo, 0.0, x)], axis=0)

    def dot(x, y):
        return jnp.dot(x.astype(_BF16), y.astype(_BF16), preferred_element_type=_F32)

    pairs = range(n_batch * HEAD_PAIRS)
    sls = [slice((n % HEAD_PAIRS) * LANES, (n % HEAD_PAIRS + 1) * LANES) for n in pairs]

    def col(name, n):
        return prepared[n // HEAD_PAIRS][name][:, sls[n]]

    h_bd = [state_ref[p] for p in pairs]
    kk = [col('kk_raw', p) * lax.rsqrt(jnp.maximum(_split_dot(col('kk_raw', p) * col('kk_raw', p), ones_bd), 1e-24))
          for p in pairs]
    b = [kk[p] * col('a', p) for p in pairs]
    lhs4 = [jnp.concatenate([stack(kk[p] * col('g_excl', p)), stack(col('r', p) * col('g_incl', p))],
                            axis=0).astype(_BF16) for p in pairs]
    rhs4 = []
    for p in pairs:
        bi = b[p] * col('g_inv', p)
        ki = col('kt', p) * col('g_inv', p)
        rhs4.append(jnp.concatenate([bi, bi, ki, ki], axis=0).astype(_BF16))
    gm = [lax.dot_general(lhs4[p], rhs4[p], _NT, preferred_element_type=_F32) for p in pairs]
    a_bd = [jnp.where(strict, gm[p][:2 * C, :2 * C], 0.0) for p in pairs]
    bp_bd = [jnp.concatenate([jnp.where(strict, gm[p][:2 * C, 2 * C:], 0.0),
                              jnp.where(incl, gm[p][2 * C:, 2 * C:], 0.0)], axis=0) for p in pairs]
    pb_bd = [jnp.where(incl, gm[p][2 * C:, :2 * C], 0.0) for p in pairs]

    inv = [jnp.where(eye, 1.0, 0.0) - jnp.where((ri // 2) == (ci // 2), a_bd[p], 0.0) for p in pairs]
    s = 2
    while s < C:
        off = jnp.logical_and((ri // (2 * s)) == (ci // (2 * s)), (ri // s) != (ci // s))
        tm = [dot(inv[p], jnp.where(off, a_bd[p], 0.0)) for p in pairs]
        inv = [inv[p] - dot(tm[p], inv[p]) for p in pairs]
        s *= 2

    v_st = [stack(col('v', p)) for p in pairs]
    hw = [dot(lhs4[p], h_bd[p]) for p in pairs]
    bv = [dot(bp_bd[p], v_st[p]) for p in pairs]
    u_st = [dot(inv[p], hw[p][:2 * C] + bv[p][:2 * C]) for p in pairs]
    y_st = [hw[p][2 * C:] + bv[p][2 * C:] - dot(pb_bd[p], u_st[p]) for p in pairs]
    y = [y_st[p][:C] + y_st[p][C:] for p in pairs]
    u = [u_st[p][:C] + u_st[p][C:] for p in pairs]

    for p in pairs:
        m_t = jnp.concatenate([col('kt', p) * col('g_end', p), -(b[p] * col('g_end', p))], axis=0).astype(_BF16)
        n_t = jnp.concatenate([col('v', p), u[p]], axis=0).astype(_BF16)
        dh = lax.dot_general(m_t, n_t, _TN, preferred_element_type=_F32)
        gt_col = jnp.sum(jnp.where(eye, jnp.broadcast_to(col('g_total', p), (LANES, LANES)), 0.0),
                         axis=1, keepdims=True)
        state_ref[p] = jnp.where(same_head, gt_col * h_bd[p] + dh, 0.0)

    mean = [_split_dot(y[p], ones_bd) * (1.0 / RWKV_HEAD) for p in pairs]
    yc = [y[p] - mean[p] for p in pairs]
    var = [_split_dot(yc[p] * yc[p], ones_bd) * (1.0 / RWKV_HEAD) for p in pairs]
    for p in pairs:
        sl = sls[p]
        yn = yc[p] * lax.rsqrt(var[p] + GN_EPS) * gn_w[:, sl] + gn_b[:, sl]
        bonus = _split_dot(col('bonus_raw', p), ones_bd) * col('v', p)
        o_ref[p // HEAD_PAIRS, :, sl] = ((yn + bonus) * col('g', p)).astype(o_ref.dtype)


def rwkv7_mix(zf, mu, w0, w_up, a0, a_up, g_up, k_k, k_a, r_k, gn_w, gn_b, B, S):
    C = RWKV_CHUNK
    nc = S // C
    W = RWKV_WIDTH
    mu_p = jnp.pad(mu, (0, ZF_SLAB_PAD - RWKV_SLAB)).reshape(1, ZF_SLAB_PAD)
    vecs = jnp.stack([w0, a0, k_k, k_a, r_k, gn_w, gn_b, jnp.zeros_like(w0)]).astype(_F32)
    wup = jnp.concatenate([w_up, jnp.zeros((LANES - DECAY_RANK, W), w_up.dtype)], axis=0).astype(_BF16)
    aup = jnp.concatenate([jnp.zeros((DECAY_RANK, W), a_up.dtype), a_up], axis=0).astype(_BF16)
    gup = jnp.concatenate([g_up, jnp.zeros((2 * LANES - GATE_RANK, W), g_up.dtype)], axis=0).astype(_BF16)
    out = pl.pallas_call(
        _rwkv_kernel,
        grid=(nc,),
        in_specs=[pl.BlockSpec((B, C, ZF_SLAB_PAD), lambda c: (0, c, 0)),
                  pl.BlockSpec((1, ZF_SLAB_PAD), lambda c: (0, 0)),
                  pl.BlockSpec((SUBLANES, W), lambda c: (0, 0)),
                  pl.BlockSpec((LANES, W), lambda c: (0, 0)),
                  pl.BlockSpec((LANES, W), lambda c: (0, 0)),
                  pl.BlockSpec((2 * LANES, W), lambda c: (0, 0))],
        out_specs=pl.BlockSpec((B, C, W), lambda c: (0, c, 0)),
        out_shape=jax.ShapeDtypeStruct((B, S, W), _BF16),
        scratch_shapes=[pltpu.VMEM((B, C + SUBLANES, ZF_SLAB_PAD), _F32),
                        pltpu.VMEM((B * HEAD_PAIRS, LANES, LANES), _F32)],
        compiler_params=_params("arbitrary"),
        name="rwkv7_mix",
    )(zf.reshape(B, S, ZF_WIDTH), mu_p, vecs, wup, aup, gup)
    return out.reshape(B * S, W)


def _router_kernel(h_ref, g_ref, wr_ref, xn_ref, route_ref):
    xn = _rms_to_bf16(h_ref[...], g_ref[...])
    xn_ref[...] = xn
    logits = jnp.dot(xn, wr_ref[...], preferred_element_type=_F32)
    lane = lax.broadcasted_iota(jnp.int32, logits.shape, 1).astype(_F32)
    l1 = jnp.where(lane < N_EXPERTS, logits, -jnp.inf)
    m1 = jnp.max(l1, axis=1, keepdims=True)
    i1 = jnp.min(jnp.where(l1 == m1, lane, float(LANES)), axis=1, keepdims=True)
    l2 = jnp.where(lane == i1, -jnp.inf, l1)
    m2 = jnp.max(l2, axis=1, keepdims=True)
    i2 = jnp.min(jnp.where(l2 == m2, lane, float(LANES)), axis=1, keepdims=True)
    e2 = jnp.exp(m2 - m1)
    g1 = 1.0 / (1.0 + e2)
    g2 = e2 / (1.0 + e2)
    route_ref[...] = jnp.where(lane == 0.0, i1, jnp.where(lane == 1.0, i2,
                               jnp.where(lane == 2.0, g1, jnp.where(lane == 3.0, g2, 0.0))))


def moe_route(h, g, router, tm):
    T, D = h.shape
    wr = jnp.pad(router, ((0, 0), (0, LANES - N_EXPERTS))).astype(_BF16)
    return pl.pallas_call(
        _router_kernel,
        grid=(T // tm,),
        in_specs=[pl.BlockSpec((tm, D), lambda i: (i, 0)),
                  pl.BlockSpec((1, D), lambda i: (0, 0)),
                  pl.BlockSpec((D, LANES), lambda i: (0, 0))],
        out_specs=[pl.BlockSpec((tm, D), lambda i: (i, 0)),
                   pl.BlockSpec((tm, LANES), lambda i: (i, 0))],
        out_shape=[jax.ShapeDtypeStruct((T, D), _BF16), jax.ShapeDtypeStruct((T, LANES), _F32)],
        compiler_params=_params("parallel"),
        name="moe_route",
    )(h, g.reshape(1, D), wr)


def _moe_ffn_kernel(te_ref, nv_ref, x_ref, w1_ref, w3_ref, w2_ref, o_ref):
    n = pl.program_id(0)
    f = pl.program_id(1)

    @pl.when(jnp.logical_and(n >= nv_ref[0], f == 0))
    def _():
        o_ref[...] = jnp.zeros(o_ref.shape, o_ref.dtype)

    @pl.when(n < nv_ref[0])
    def _():
        x = x_ref[...]
        a = jnp.dot(x, w1_ref[...], preferred_element_type=_F32)
        b = jnp.dot(x, w3_ref[...], preferred_element_type=_F32)
        mid = (a * _sigmoid(a) * b).astype(_BF16)
        contrib = jnp.dot(mid, w2_ref[...], preferred_element_type=_F32)

        @pl.when(f == 0)
        def _():
            o_ref[...] = contrib

        @pl.when(f > 0)
        def _():
            o_ref[...] += contrib


def moe_grouped_ffn(xs, tile_expert, n_valid, w1, w3, w2):
    R, D = xs.shape
    F = w1.shape[2]
    nf = F // MOE_TF

    def f_idx(n, f, nv):
        return jnp.where(n < nv[0], f, nf - 1)

    return pl.pallas_call(
        _moe_ffn_kernel,
        grid_spec=pltpu.PrefetchScalarGridSpec(
            num_scalar_prefetch=2,
            grid=(R // MOE_TM, nf),
            in_specs=[pl.BlockSpec((MOE_TM, D), lambda n, f, te, nv: (n, 0)),
                      pl.BlockSpec((None, D, MOE_TF), lambda n, f, te, nv: (te[n], 0, f_idx(n, f, nv))),
                      pl.BlockSpec((None, D, MOE_TF), lambda n, f, te, nv: (te[n], 0, f_idx(n, f, nv))),
                      pl.BlockSpec((None, MOE_TF, D), lambda n, f, te, nv: (te[n], f_idx(n, f, nv), 0))],
            out_specs=pl.BlockSpec((MOE_TM, D), lambda n, f, te, nv: (n, 0))),
        out_shape=jax.ShapeDtypeStruct((R, D), _F32),
        compiler_params=_params("parallel", "arbitrary"),
        name="moe_grouped_ffn",
    )(tile_expert, n_valid, xs, w1, w3, w2)


def moe_residual(h, g, router, w1, w3, w2):
    T, D = h.shape
    E = N_EXPERTS
    xn, route = moe_route(h, g, router, tm=ROUTE_TM)
    top = route[:, 0:TOP_K_EXPERTS].astype(jnp.int32)
    gates = route[:, TOP_K_EXPERTS:2 * TOP_K_EXPERTS]
    ef = top.T.reshape(TOP_K_EXPERTS * T)
    onehot = (ef[:, None] == jnp.arange(E, dtype=jnp.int32)[None, :]).astype(jnp.int32)
    csum = jnp.cumsum(onehot, axis=0)
    counts = csum[-1]
    padded = ((counts + MOE_TM - 1) // MOE_TM) * MOE_TM
    pends = jnp.cumsum(padded)
    pstarts = pends - padded
    starts = jnp.cumsum(counts) - counts
    pos = jnp.sum(onehot * (csum - 1 + pstarts[None, :]), axis=1)
    R = TOP_K_EXPERTS * T + E * MOE_TM
    n_tiles = R // MOE_TM
    tile_start = jnp.arange(n_tiles, dtype=jnp.int32) * MOE_TM
    tile_expert = jnp.minimum(jnp.searchsorted(pends, tile_start, side='right'), E - 1).astype(jnp.int32)
    n_valid = (pends[-1:] // MOE_TM).astype(jnp.int32)
    order = jnp.argsort(ef, stable=True).astype(jnp.int32)
    first = starts[tile_expert] + tile_start - pstarts[tile_expert]
    idx = (first[:, None] + jnp.arange(MOE_TM, dtype=jnp.int32)[None, :]).reshape(R)
    src = order[jnp.clip(idx, 0, TOP_K_EXPERTS * T - 1)] % T
    xs = jnp.take(xn, src, axis=0)
    ys = moe_grouped_ffn(xs, tile_expert, n_valid, w1, w3, w2)
    mix = sum(gates[:, slot:slot + 1] * jnp.take(ys, pos[slot * T:(slot + 1) * T], axis=0)
              for slot in range(TOP_K_EXPERTS))
    return h + mix


def _pack_w_in(w):
    wq, wk, wv, wqi, wki, wwi, wslab, wga, wgb = _split(w, IN_SIZES)
    D = w.shape[0]
    wb = jnp.concatenate([wq * (ATT_HEAD_DIM ** -0.5 * LOG2_E), wk, wv, wqi, wki, wki], axis=1)
    wf = jnp.concatenate([wslab, jnp.zeros((D, ZF_SLAB_PAD - RWKV_SLAB), w.dtype),
                          wki, wwi, jnp.zeros((D, LANES - IDX_DIM - IDX_HEADS), w.dtype), wga, wgb], axis=1)
    return wb.astype(_BF16), wf.astype(_BF16)


def kernel(x, p, w_in, att_up, rwkv_up, w_out, rel_bias, rwkv_mu, rwkv_w0, rwkv_w_up,
           rwkv_a0, rwkv_a_up, rwkv_g_up, rwkv_k_k, rwkv_k_a, rwkv_r_k, rwkv_gn_w, rwkv_gn_b,
           norm_mix, norm_ffn, norm_ple, ple_proj, ple_gate, ffn_w1, ffn_w3, ffn_w2,
           moe_router, moe_w1, moe_w3, moe_w2, final_norm):
    B, S, D = x.shape
    T = B * S
    depth = w_in.shape[0]
    h = x.reshape(T, D)
    for i in range(depth):
        wb, wf = _pack_w_in(w_in[i])
        zb = norm_matmul(h, norm_mix[i], wb, _BF16, tm=IN_PROJ_TM, tn=ZB_WIDTH)
        zf = norm_matmul(h, norm_mix[i], wf, _F32, tm=IN_PROJ_F32_TM, tn=IN_PROJ_F32_TN)
        att = dsa_attention(zb, zf, rel_bias, B, S)
        rw = rwkv7_mix(zf, rwkv_mu[i], rwkv_w0[i], rwkv_w_up[i], rwkv_a0[i], rwkv_a_up[i], rwkv_g_up[i],
                       rwkv_k_k[i], rwkv_k_a[i], rwkv_r_k[i], rwkv_gn_w[i], rwkv_gn_b[i], B, S)
        merged = gated_merge(att, rw, zf, att_up[i].astype(_BF16), rwkv_up[i].astype(_BF16),
                             tm=MERGE_TM, tn=MERGE_TN)
        h = matmul_residual(merged, w_out[i].astype(_BF16), h, tm=OUT_PROJ_TM, tn=D)
        if i % 2 == 0:
            j = i // 2
            h = ffn_residual(h, norm_ffn[i], ffn_w1[j].astype(_BF16), ffn_w3[j].astype(_BF16),
                             ffn_w2[j].astype(_BF16), tm=FFN_TM, tf=FFN_TF)
        else:
            j = i // 2
            h = moe_residual(h, norm_ffn[i], moe_router[j], cast_bf16(moe_w1[j]), cast_bf16(moe_w3[j]),
                             cast_bf16(moe_w2[j]))
        h = ple_residual(h, norm_ple[i], p[i].reshape(T, PLE_DIM), ple_gate[i].astype(_BF16),
                         ple_proj[i].astype(_BF16), final_norm, final=(i == depth - 1), tm=PLE_TM)
    return h.reshape(B, S, D)
```

```python
import functools
import math

import jax
import jax.numpy as jnp
import numpy as np
from jax import lax
from jax.experimental import pallas as pl
from jax.experimental.pallas import tpu as pltpu

D_MODEL = 2048
DEPTH = 2
ATT_HEADS = 8
ATT_KV_HEADS = 2
ATT_HEAD_DIM = 128
IDX_HEADS = 8
IDX_DIM = 64
TOPK_MAX = 256
REL_BUCKETS = 32
REL_MAX_DIST = 128
RWKV_HEAD = 64
RWKV_HEADS = 16
DECAY_RANK = 64
ICLR_RANK = 64
GATE_RANK = 160
D_FF = 5632
N_EXPERTS = 8
TOP_K_EXPERTS = 2
D_FF_EXPERT = 7168
PLE_DIM = 256
RMS_EPS = 1e-6
GN_EPS = 64e-5

ATT_WIDTH = ATT_HEADS * ATT_HEAD_DIM
KV_WIDTH = ATT_KV_HEADS * ATT_HEAD_DIM
RWKV_WIDTH = RWKV_HEADS * RWKV_HEAD
RWKV_SLAB = 3 * RWKV_WIDTH + DECAY_RANK + ICLR_RANK + GATE_RANK
IN_SIZES = (ATT_WIDTH, KV_WIDTH, KV_WIDTH, IDX_HEADS * IDX_DIM, IDX_DIM, IDX_HEADS, RWKV_SLAB, D_MODEL, D_MODEL)

LANES = 128
SUBLANES = 8
VMEM_LIMIT = 56 * 1024 * 1024
CAST_BLOCK_BYTES = 8 * 1024 * 1024

IN_PROJ_TM = 1024
IN_PROJ_F32_TM = 1024
IN_PROJ_F32_TN = 1536
MERGE_TM, MERGE_TN = 1024, 512
OUT_PROJ_TM = 512
FFN_TM, FFN_TF = 512, 512
ROUTE_TM = 512
PLE_TM = 512

Q_BLOCK = 128
KEY_CHUNK = 512
BLOCKS_PER_CHUNK = KEY_CHUNK // Q_BLOCK
BLOCKS_PER_GROUP = 32
MASKED_LOGIT = -1e30
LOG2_E = math.log2(math.e)
GQA_GROUP = ATT_HEADS // ATT_KV_HEADS

ZB_WIDTH = ATT_WIDTH + 2 * KV_WIDTH + IDX_HEADS * IDX_DIM + 2 * IDX_DIM
ZF_SLAB_PAD = 3456
ZF_WIDX = ZF_SLAB_PAD
ZF_GATE_A = ZF_WIDX + LANES
ZF_GATE_B = ZF_GATE_A + D_MODEL
ZF_WIDTH = ZF_GATE_B + D_MODEL

RWKV_CHUNK = 64
HEAD_PAIRS = RWKV_HEADS // 2
MOE_TM = 512
MOE_TF = 1024

_F32 = jnp.float32
_BF16 = jnp.bfloat16
_NT = (((1,), (1,)), ((), ()))
_TN = (((0,), (0,)), ((), ()))


def _split(z, sizes):
    return jnp.split(z, np.cumsum(sizes)[:-1].tolist(), axis=-1)


def _sigmoid(x):
    return 1.0 / (1.0 + jnp.exp(-x))


def _rms_to_bf16(x, g):
    y = x * lax.rsqrt(jnp.mean(x * x, axis=-1, keepdims=True) + RMS_EPS)
    return (y * g).astype(_BF16)


def _params(*sem):
    return pltpu.CompilerParams(dimension_semantics=sem, vmem_limit_bytes=VMEM_LIMIT)


def _norm_mm_kernel(x_ref, g_ref, w_ref, o_ref, xn_ref):
    @pl.when(pl.program_id(1) == 0)
    def _():
        xn_ref[...] = _rms_to_bf16(x_ref[...], g_ref[...])

    o_ref[...] = jnp.dot(xn_ref[...], w_ref[...], preferred_element_type=_F32).astype(o_ref.dtype)


def norm_matmul(x, g, w, out_dtype, tm, tn):
    T, D = x.shape
    N = w.shape[1]
    tm = min(tm, T)
    return pl.pallas_call(
        _norm_mm_kernel,
        grid=(T // tm, N // tn),
        in_specs=[pl.BlockSpec((tm, D), lambda i, j: (i, 0)),
                  pl.BlockSpec((1, D), lambda i, j: (0, 0)),
                  pl.BlockSpec((D, tn), lambda i, j: (0, j))],
        out_specs=pl.BlockSpec((tm, tn), lambda i, j: (i, j)),
        out_shape=jax.ShapeDtypeStruct((T, N), out_dtype),
        scratch_shapes=[pltpu.VMEM((tm, D), _BF16)],
        compiler_params=_params("parallel", "arbitrary"),
        name="norm_matmul",
    )(x, g.reshape(1, D), w)


def _merge_kernel(att_ref, rw_ref, ga_ref, gb_ref, wa_ref, wr_ref, o_ref):
    a = jnp.dot(att_ref[...], wa_ref[...], preferred_element_type=_F32)
    r = jnp.dot(rw_ref[...], wr_ref[...], preferred_element_type=_F32)
    o_ref[...] = (_sigmoid(ga_ref[...]) * a + _sigmoid(gb_ref[...]) * r).astype(o_ref.dtype)


def gated_merge(att, rw, zf, wa, wr, tm, tn):
    T, K = att.shape
    N = wa.shape[1]
    assert ZF_GATE_A % tn == 0 and ZF_GATE_B % tn == 0
    ja, jb = ZF_GATE_A // tn, ZF_GATE_B // tn
    return pl.pallas_call(
        _merge_kernel,
        grid=(T // tm, N // tn),
        in_specs=[pl.BlockSpec((tm, K), lambda i, j: (i, 0)),
                  pl.BlockSpec((tm, K), lambda i, j: (i, 0)),
                  pl.BlockSpec((tm, tn), lambda i, j: (i, ja + j)),
                  pl.BlockSpec((tm, tn), lambda i, j: (i, jb + j)),
                  pl.BlockSpec((K, tn), lambda i, j: (0, j)),
                  pl.BlockSpec((K, tn), lambda i, j: (0, j))],
        out_specs=pl.BlockSpec((tm, tn), lambda i, j: (i, j)),
        out_shape=jax.ShapeDtypeStruct((T, N), _BF16),
        compiler_params=_params("parallel", "arbitrary"),
        name="gated_merge",
    )(att, rw, zf, zf, wa, wr)


def _mm_res_kernel(x_ref, w_ref, h_ref, o_ref):
    o_ref[...] = h_ref[...] + jnp.dot(x_ref[...], w_ref[...], preferred_element_type=_F32)


def matmul_residual(x, w, h, tm, tn):
    T, K = x.shape
    N = w.shape[1]
    return pl.pallas_call(
        _mm_res_kernel,
        grid=(T // tm, N // tn),
        in_specs=[pl.BlockSpec((tm, K), lambda i, j: (i, 0)),
                  pl.BlockSpec((K, tn), lambda i, j: (0, j)),
                  pl.BlockSpec((tm, tn), lambda i, j: (i, j))],
        out_specs=pl.BlockSpec((tm, tn), lambda i, j: (i, j)),
        out_shape=jax.ShapeDtypeStruct((T, N), _F32),
        compiler_params=_params("parallel", "arbitrary"),
        name="matmul_residual",
    )(x, w, h)


def _ffn_kernel(h_ref, g_ref, w1_ref, w3_ref, w2_ref, o_ref, xn_ref):
    f = pl.program_id(1)

    @pl.when(f == 0)
    def _():
        xn_ref[...] = _rms_to_bf16(h_ref[...], g_ref[...])
        o_ref[...] = h_ref[...]

    xn = xn_ref[...]
    a = jnp.dot(xn, w1_ref[...], preferred_element_type=_F32)
    b = jnp.dot(xn, w3_ref[...], preferred_element_type=_F32)
    mid = (a * _sigmoid(a) * b).astype(_BF16)
    o_ref[...] += jnp.dot(mid, w2_ref[...], preferred_element_type=_F32)


def ffn_residual(h, g, w1, w3, w2, tm, tf):
    T, D = h.shape
    F = w1.shape[1]
    return pl.pallas_call(
        _ffn_kernel,
        grid=(T // tm, F // tf),
        in_specs=[pl.BlockSpec((tm, D), lambda i, f: (i, 0)),
                  pl.BlockSpec((1, D), lambda i, f: (0, 0)),
                  pl.BlockSpec((D, tf), lambda i, f: (0, f)),
                  pl.BlockSpec((D, tf), lambda i, f: (0, f)),
                  pl.BlockSpec((tf, D), lambda i, f: (f, 0))],
        out_specs=pl.BlockSpec((tm, D), lambda i, f: (i, 0)),
        out_shape=jax.ShapeDtypeStruct((T, D), _F32),
        scratch_shapes=[pltpu.VMEM((tm, D), _BF16)],
        compiler_params=_params("parallel", "arbitrary"),
        name="ffn_residual",
    )(h, g.reshape(1, D), w1, w3, w2)


def _ple_kernel(h_ref, g_ref, p_ref, wg_ref, wp_ref, fg_ref, o_ref, *, final):
    h = h_ref[...]
    xn = _rms_to_bf16(h, g_ref[...])
    gate = _sigmoid(jnp.dot(xn, wg_ref[...], preferred_element_type=_F32))
    pp = jnp.dot(p_ref[...].astype(_BF16), wp_ref[...], preferred_element_type=_F32)
    out = h + gate * pp
    if final:
        out = out * lax.rsqrt(jnp.mean(out * out, axis=-1, keepdims=True) + RMS_EPS) * fg_ref[...]
    o_ref[...] = out


def ple_residual(h, g, p, wg, wp, final_g, final, tm):
    T, D = h.shape
    P = p.shape[1]
    return pl.pallas_call(
        functools.partial(_ple_kernel, final=final),
        grid=(T // tm,),
        in_specs=[pl.BlockSpec((tm, D), lambda i: (i, 0)),
                  pl.BlockSpec((1, D), lambda i: (0, 0)),
                  pl.BlockSpec((tm, P), lambda i: (i, 0)),
                  pl.BlockSpec((D, D), lambda i: (0, 0)),
                  pl.BlockSpec((P, D), lambda i: (0, 0)),
                  pl.BlockSpec((1, D), lambda i: (0, 0))],
        out_specs=pl.BlockSpec((tm, D), lambda i: (i, 0)),
        out_shape=jax.ShapeDtypeStruct((T, D), _F32),
        compiler_params=_params("parallel"),
        name="ple_residual",
    )(h, g.reshape(1, D), p, wg, wp, final_g.reshape(1, D))


def _cast_kernel(x_ref, o_ref):
    o_ref[...] = x_ref[...].astype(o_ref.dtype)


def cast_bf16(w):
    E, R, N = w.shape
    rows = R
    while rows * N * 4 > CAST_BLOCK_BYTES and rows % 2 == 0:
        rows //= 2
    return pl.pallas_call(
        _cast_kernel,
        grid=(E, R // rows),
        in_specs=[pl.BlockSpec((None, rows, N), lambda e, i: (e, i, 0))],
        out_specs=pl.BlockSpec((None, rows, N), lambda e, i: (e, i, 0)),
        out_shape=jax.ShapeDtypeStruct(w.shape, _BF16),
        compiler_params=_params("parallel", "parallel"),
        name="cast_bf16",
    )(w)


def rel_bucket(dist):
    max_exact = REL_BUCKETS // 2
    dist = jnp.maximum(dist, 0)
    d_f = jnp.maximum(dist, 1).astype(_F32)
    large = max_exact + (jnp.log(d_f / max_exact) / math.log(REL_MAX_DIST / max_exact)
                         * (REL_BUCKETS - max_exact)).astype(jnp.int32)
    large = jnp.minimum(large, REL_BUCKETS - 1)
    return jnp.where(dist < max_exact, dist, large)


def _sortable(x):
    b = lax.bitcast_convert_type(x, jnp.int32)
    return b ^ ((b >> 31) & jnp.int32(0x7FFFFFFF))


_INT_MIN = -2 ** 31
_KEY_NEG_INF = 0x807FFFFF - 2 ** 32


def _dsa_kernel(q_ref, qi_ref, wi_ref, k_ref, v_ref, ki_ref, bprev_ref, bdiag_ref, bfar_ref, o_ref,
                keys_ref, planes_ref, active_ref, qs_ref, qim_ref, m_ref, acc_ref, *, n_sel, idx_bits):
    n_grp = planes_ref.shape[0]
    qb = pl.program_id(1)
    t0 = qb * Q_BLOCK
    n_chunks = qb // BLOCKS_PER_CHUNK + 1
    t_idx = t0 + lax.broadcasted_iota(jnp.int32, (Q_BLOCK, 1), 0)
    lane_blk = lax.broadcasted_iota(jnp.int32, (1, Q_BLOCK), 1)
    lane_chunk = lax.broadcasted_iota(jnp.int32, (1, KEY_CHUNK), 1)

    for hh in range(ATT_HEADS):
        g, h = divmod(hh, GQA_GROUP)
        qs_ref[g, h * Q_BLOCK:(h + 1) * Q_BLOCK, :] = q_ref[:, hh * ATT_HEAD_DIM:(hh + 1) * ATT_HEAD_DIM]
    low = lax.broadcasted_iota(jnp.int32, (Q_BLOCK, LANES), 1) < IDX_DIM
    for hh in range(IDX_HEADS):
        tile = qi_ref[:, (hh // 2) * LANES:(hh // 2 + 1) * LANES]
        keep = low if hh % 2 == 0 else jnp.logical_not(low)
        qim_ref[hh] = jnp.where(keep, tile, jnp.zeros_like(tile))
    m_ref[...] = jnp.full(m_ref.shape, MASKED_LOGIT, _F32)

    @pl.when(qb == 0)
    def _():
        keys_ref[BLOCKS_PER_CHUNK:] = jnp.zeros((keys_ref.shape[0] - BLOCKS_PER_CHUNK, Q_BLOCK, Q_BLOCK), jnp.int32)
        planes_ref[...] = jnp.zeros(planes_ref.shape, jnp.int32)
    acc_ref[...] = jnp.zeros(acc_ref.shape, _F32)

    w_all = wi_ref[...]
    idx_scale = (IDX_HEADS ** -0.5) * (IDX_DIM ** -0.5)

    def score_body(c, carry):
        ks = pl.multiple_of(c * KEY_CHUNK, KEY_CHUNK)
        kic = ki_ref[pl.ds(ks, KEY_CHUNK), :]
        d = lax.dot_general(qim_ref[...].reshape(IDX_HEADS * Q_BLOCK, LANES), kic, _NT,
                            preferred_element_type=_F32)
        sc = jnp.zeros((Q_BLOCK, KEY_CHUNK), _F32)
        for hh in range(IDX_HEADS):
            sc = sc + w_all[:, IDX_DIM + hh:IDX_DIM + hh + 1] * jnp.maximum(d[hh * Q_BLOCK:(hh + 1) * Q_BLOCK], 0.0)
        sc = sc * idx_scale
        sc = jnp.where(sc == 0.0, 0.0, sc)
        key = jnp.where(ks + lane_chunk <= t_idx, _sortable(sc), _KEY_NEG_INF)
        for j in range(BLOCKS_PER_CHUNK):
            keys_ref[c * BLOCKS_PER_CHUNK + j] = key[:, j * Q_BLOCK:(j + 1) * Q_BLOCK]
        return carry

    lax.fori_loop(0, n_chunks, score_body, 0)

    def count(pred):
        def body(c, acc):
            for j in range(BLOCKS_PER_CHUNK):
                blk = c * BLOCKS_PER_CHUNK + j
                acc = acc + jnp.where(pred(keys_ref[blk], blk * Q_BLOCK + lane_blk), 1.0, 0.0)
            return acc

        acc = lax.fori_loop(0, n_chunks, body, jnp.zeros((Q_BLOCK, Q_BLOCK), _F32))
        return jnp.sum(acc, axis=1, keepdims=True)

    n_blocks = n_chunks * BLOCKS_PER_CHUNK
    n_groups = (n_blocks + BLOCKS_PER_GROUP - 1) // BLOCKS_PER_GROUP

    def transpose_group(g, carry):
        def rows(rg, carry_rows):
            r0 = pl.multiple_of(rg * SUBLANES, SUBLANES)
            x = [keys_ref[g * BLOCKS_PER_GROUP + j, pl.ds(r0, SUBLANES), :] for j in range(BLOCKS_PER_GROUP)]
            j, m = 16, 0x0000FFFF
            while j:
                k = 0
                while k < BLOCKS_PER_GROUP:
                    t = (x[k] ^ lax.shift_right_logical(x[k + j], jnp.int32(j))) & jnp.int32(m)
                    x[k] = x[k] ^ t
                    x[k + j] = x[k + j] ^ lax.shift_left(t, jnp.int32(j))
                    k = (k + j + 1) & ~j
                j >>= 1
                m ^= m << j
            planes_ref[g, 0, pl.ds(r0, SUBLANES), :] = ~x[0]
            for i in range(1, 32):
                planes_ref[g, i, pl.ds(r0, SUBLANES), :] = x[i]
            return carry_rows

        lax.fori_loop(0, Q_BLOCK // SUBLANES, rows, 0)
        n_live = jnp.minimum(n_blocks - g * BLOCKS_PER_GROUP, BLOCKS_PER_GROUP)
        active_ref[g] = lax.shift_left(jnp.full((Q_BLOCK, Q_BLOCK), -1, jnp.int32), BLOCKS_PER_GROUP - n_live)
        return carry

    active_ref[...] = jnp.zeros(active_ref.shape, jnp.int32)
    lax.fori_loop(0, n_groups, transpose_group, 0)

    def lane_total(words):
        return jnp.sum(sum(lax.population_count(w) for w in words).astype(_F32), axis=1, keepdims=True)

    def split_active(g, i):
        act = active_ref[g]
        hi = act & planes_ref[g, 2 * i]
        lo = act ^ hi
        hi1 = hi & planes_ref[g, 2 * i + 1]
        lo1 = lo & planes_ref[g, 2 * i + 1]
        return hi1, hi ^ hi1, lo1, lo ^ lo1

    def radix_pass(i, carry):
        prefix, above = carry
        parts = [split_active(g, i) for g in range(n_grp)]
        c11, c10, c01 = (lane_total([parts[g][q] for g in range(n_grp)]) for q in range(3))
        upto10 = above + c11 + c10
        t11 = above + c11 >= n_sel
        t10 = upto10 >= n_sel
        t01 = upto10 + c01 >= n_sel
        for g in range(n_grp):
            s11, s10, s01, s00 = split_active(g, i)
            active_ref[g] = jnp.where(t11, s11, jnp.where(t10, s10, jnp.where(t01, s01, s00)))
        bit1 = jnp.where(t10, lax.shift_left(jnp.int32(1), 31 - 2 * i), 0)
        bit0 = jnp.where(jnp.logical_or(t11, jnp.logical_and(jnp.logical_not(t10), t01)),
                         lax.shift_left(jnp.int32(1), 30 - 2 * i), 0)
        above = jnp.where(t11, above, jnp.where(t10, above + c11, jnp.where(t01, upto10, upto10 + c01)))
        return prefix | bit1 | bit0, above

    prefix, n_gt = lax.fori_loop(0, 16, radix_pass, (jnp.zeros((Q_BLOCK, 1), jnp.int32),
                                                     jnp.zeros((Q_BLOCK, 1), _F32)))
    tau = prefix ^ jnp.int32(_INT_MIN)
    n_eq = lane_total([active_ref[g] for g in range(n_grp)])
    tied = jnp.logical_and(n_gt + n_eq > n_sel, tau > _KEY_NEG_INF)

    def tie_search():
        need = n_sel - count(lambda key, sidx: key > tau)

        def index_bit(i, x):
            cand = x | lax.shift_left(jnp.int32(1), idx_bits - 1 - i)
            cnt = count(lambda key, sidx: jnp.logical_and(key == tau, sidx < cand))
            return jnp.where(cnt < need, cand, x)

        return lax.fori_loop(0, idx_bits, index_bit, jnp.zeros((Q_BLOCK, 1), jnp.int32))

    last_tie = lax.cond(jnp.max(jnp.where(tied, 1.0, 0.0)) > 0.0, tie_search,
                        lambda: jnp.full((Q_BLOCK, 1), 2 ** idx_bits, jnp.int32))

    def attend(key, k_tile, v_tile, s0, lane, bias_ref, enable=None, const_bias=False):
        width = key.shape[1]
        sidx = s0 + lane
        sel = jnp.logical_or(key > tau, jnp.logical_and(key == tau, sidx <= last_tie))
        sel = jnp.logical_and(sel, key > _KEY_NEG_INF)
        if enable is not None:
            sel = jnp.logical_and(sel, jnp.zeros_like(sidx) + enable > 0)
        ones = jnp.ones((width, ATT_HEAD_DIM), _BF16)
        groups = range(ATT_KV_HEADS)
        cols = [slice(g * ATT_HEAD_DIM, (g + 1) * ATT_HEAD_DIM) for g in groups]

        def logits(g):
            return lax.dot_general(qs_ref[g], k_tile[:, cols[g]], _NT,
                                   preferred_element_type=_F32).reshape(GQA_GROUP, Q_BLOCK, width)

        def softmax(g, s):
            bias = bias_ref[g * GQA_GROUP:(g + 1) * GQA_GROUP]
            m_old = m_ref[g]
            if const_bias:
                sh = jnp.where(sel[None], s, MASKED_LOGIT)
                m_new = jnp.maximum(m_old, jnp.max(sh, axis=2, keepdims=True) + bias)
                shift = m_new - bias
            else:
                sh = jnp.where(sel[None], s + bias, MASKED_LOGIT)
                m_new = jnp.maximum(m_old, jnp.max(sh, axis=2, keepdims=True))
                shift = m_new
            m_ref[g] = m_new
            return (jnp.exp2(sh - shift).reshape(GQA_GROUP * Q_BLOCK, width).astype(_BF16),
                    jnp.exp2(m_old - m_new).reshape(GQA_GROUP * Q_BLOCK, 1))

        def accumulate(g, p, alpha):
            pv = jnp.dot(p, jnp.concatenate([v_tile[:, cols[g]], ones], axis=1), preferred_element_type=_F32)
            acc_ref[g] = acc_ref[g] * alpha + pv

        if width <= Q_BLOCK:
            s = [logits(g) for g in groups]
            pa = [softmax(g, s[g]) for g in groups]
            for g in groups:
                accumulate(g, *pa[g])
        else:
            for g in groups:
                accumulate(g, *softmax(g, logits(g)))

    n_far = jnp.maximum(qb - 1, 0) // BLOCKS_PER_CHUNK

    def far_body(c, carry):
        ks = pl.multiple_of(c * KEY_CHUNK, KEY_CHUNK)
        key = jnp.concatenate([keys_ref[c * BLOCKS_PER_CHUNK + j] for j in range(BLOCKS_PER_CHUNK)], axis=1)
        attend(key, k_ref[pl.ds(ks, KEY_CHUNK), :], v_ref[pl.ds(ks, KEY_CHUNK), :], ks, lane_chunk, bfar_ref,
               const_bias=True)
        return carry

    lax.fori_loop(0, n_far, far_body, 0)

    def block_step(j, bias_ref, enable=None, const_bias=False):
        ks = pl.multiple_of(j * Q_BLOCK, Q_BLOCK)
        attend(keys_ref[j], k_ref[pl.ds(ks, Q_BLOCK), :], v_ref[pl.ds(ks, Q_BLOCK), :], ks, lane_blk, bias_ref,
               enable, const_bias)

    def tail_body(j, carry):
        block_step(j, bfar_ref, const_bias=True)
        return carry

    lax.fori_loop(n_far * BLOCKS_PER_CHUNK, qb - 1, tail_body, 0)
    block_step(jnp.maximum(qb - 1, 0), bprev_ref, enable=jnp.minimum(qb, 1))
    block_step(qb, bdiag_ref)

    for hh in range(ATT_HEADS):
        g, h = divmod(hh, GQA_GROUP)
        rows = slice(h * Q_BLOCK, (h + 1) * Q_BLOCK)
        o_ref[:, hh * ATT_HEAD_DIM:(hh + 1) * ATT_HEAD_DIM] = (
            acc_ref[g, rows, 0:ATT_HEAD_DIM] / acc_ref[g, rows, ATT_HEAD_DIM:2 * ATT_HEAD_DIM]).astype(o_ref.dtype)


def dsa_attention(zb, zf, rel_bias, B, S):
    nq = S // Q_BLOCK
    n_grp = -(-nq // BLOCKS_PER_GROUP)
    n_sel = min(TOPK_MAX, S // 4)
    tl = jnp.arange(Q_BLOCK, dtype=jnp.int32)[:, None]
    sr = jnp.arange(Q_BLOCK, dtype=jnp.int32)[None, :]
    rel_bias = rel_bias.astype(_F32) * LOG2_E

    def bias_table(dist):
        hit = rel_bucket(dist)[None, :, :, None] == jnp.arange(REL_BUCKETS, dtype=jnp.int32)
        return jnp.sum(jnp.where(hit, rel_bias.T[:, None, None, :], 0.0), axis=-1)

    bprev = bias_table(tl + Q_BLOCK - sr)
    bdiag = bias_table(tl - sr)
    bfar = rel_bias[rel_bucket(jnp.int32(REL_MAX_DIST + 1))].reshape(ATT_HEADS, 1, 1)
    kernel = functools.partial(_dsa_kernel, n_sel=float(n_sel), idx_bits=int(S - 1).bit_length())
    c_k = ATT_WIDTH // KV_WIDTH
    c_qi = (ATT_WIDTH + 2 * KV_WIDTH) // (IDX_HEADS * IDX_DIM)
    c_ki = (ATT_WIDTH + 2 * KV_WIDTH + IDX_HEADS * IDX_DIM) // LANES
    return pl.pallas_call(
        kernel,
        grid=(B, nq),
        in_specs=[pl.BlockSpec((Q_BLOCK, ATT_WIDTH), lambda b, i: (b * nq + i, 0)),
                  pl.BlockSpec((Q_BLOCK, IDX_HEADS * IDX_DIM), lambda b, i: (b * nq + i, c_qi)),
                  pl.BlockSpec((Q_BLOCK, LANES), lambda b, i: (b * nq + i, ZF_WIDX // LANES)),
                  pl.BlockSpec((S, KV_WIDTH), lambda b, i: (b, c_k)),
                  pl.BlockSpec((S, KV_WIDTH), lambda b, i: (b, c_k + 1)),
                  pl.BlockSpec((S, LANES), lambda b, i: (b, c_ki)),
                  pl.BlockSpec((ATT_HEADS, Q_BLOCK, Q_BLOCK), lambda b, i: (0, 0, 0)),
                  pl.BlockSpec((ATT_HEADS, Q_BLOCK, Q_BLOCK), lambda b, i: (0, 0, 0)),
                  pl.BlockSpec((ATT_HEADS, 1, 1), lambda b, i: (0, 0, 0))],
        out_specs=pl.BlockSpec((Q_BLOCK, ATT_WIDTH), lambda b, i: (b * nq + i, 0)),
        out_shape=jax.ShapeDtypeStruct((B * S, ATT_WIDTH), _BF16),
        scratch_shapes=[pltpu.VMEM((n_grp * BLOCKS_PER_GROUP, Q_BLOCK, Q_BLOCK), jnp.int32),
                        pltpu.VMEM((n_grp, 32, Q_BLOCK, Q_BLOCK), jnp.int32),
                        pltpu.VMEM((n_grp, Q_BLOCK, Q_BLOCK), jnp.int32),
                        pltpu.VMEM((ATT_KV_HEADS, GQA_GROUP * Q_BLOCK, ATT_HEAD_DIM), _BF16),
                        pltpu.VMEM((IDX_HEADS, Q_BLOCK, LANES), _BF16),
                        pltpu.VMEM((ATT_KV_HEADS, GQA_GROUP, Q_BLOCK, 1), _F32),
                        pltpu.VMEM((ATT_KV_HEADS, GQA_GROUP * Q_BLOCK, 2 * ATT_HEAD_DIM), _F32)],
        compiler_params=_params("parallel", "arbitrary"),
        name="dsa_attention",
    )(zb, zb, zf, zb, zb, zb, bprev, bdiag, bfar)


def _split_dot(x, w):
    hi = x.astype(_BF16)
    lo = (x - hi.astype(_F32)).astype(_BF16)
    return jnp.dot(hi, w, preferred_element_type=_F32) + jnp.dot(lo, w, preferred_element_type=_F32)


def _split_dot_left(w, x):
    hi = x.astype(_BF16)
    lo = (x - hi.astype(_F32)).astype(_BF16)
    return jnp.dot(w, hi, preferred_element_type=_F32) + jnp.dot(w, lo, preferred_element_type=_F32)


def _rwkv_kernel(slab_ref, mu_ref, vec_ref, wup_ref, aup_ref, gup_ref, o_ref, xs_ref, state_ref):
    C = RWKV_CHUNK
    W = RWKV_WIDTH
    n_batch = slab_ref.shape[0]
    c = pl.program_id(0)

    @pl.when(c == 0)
    def _():
        xs_ref[:, 0:SUBLANES, :] = jnp.zeros((n_batch, SUBLANES, ZF_SLAB_PAD), _F32)
        state_ref[...] = jnp.zeros(state_ref.shape, _F32)

    w0, a0, k_k, k_a, r_k, gn_w, gn_b = (vec_ref[i:i + 1, :] for i in range(7))
    ti = lax.broadcasted_iota(jnp.int32, (C, C), 0)
    tj = lax.broadcasted_iota(jnp.int32, (C, C), 1)
    tri = jnp.where(tj <= ti, 1.0, 0.0).astype(_BF16)

    def prepare(bi):
        slab = slab_ref[bi]
        xs_ref[bi, SUBLANES:SUBLANES + C, :] = slab
        shifted = xs_ref[bi, SUBLANES - 1:SUBLANES - 1 + C, :]
        xs = slab + (shifted - slab) * mu_ref[...]
        xs_ref[bi, 0:SUBLANES, :] = slab[C - SUBLANES:C, :]

        r = xs[:, 0:W]
        k = xs[:, W:2 * W]
        lora = xs[:, 3 * W:3 * W + LANES]
        xg = xs[:, 3 * W + LANES:3 * W + 3 * LANES]
        d = w0 + jnp.dot(jnp.tanh(lora).astype(_BF16), wup_ref[...], preferred_element_type=_F32)
        logw = -math.exp(-0.5) * _sigmoid(d)
        a = _sigmoid(a0 + jnp.dot(lora.astype(_BF16), aup_ref[...], preferred_element_type=_F32))
        kt = k * (1.0 + (a - 1.0) * k_a)
        cum = _split_dot_left(tri, logw)
        cum_end = cum[C - 1:C, :]
        return dict(r=r, v=xs[:, 2 * W:3 * W], a=a, kt=kt, kk_raw=k * k_k, bonus_raw=r * kt * r_k,
                    g=jnp.dot(_sigmoid(xg).astype(_BF16), gup_ref[...], preferred_element_type=_F32),
                    g_incl=jnp.exp(cum), g_excl=jnp.exp(cum - logw), g_inv=jnp.exp(-cum),
                    g_end=jnp.exp(cum_end - cum), g_total=jnp.exp(cum_end))

    prepared = [prepare(bi) for bi in range(n_batch)]

    lo = lax.broadcasted_iota(jnp.int32, (C, LANES), 1) < RWKV_HEAD
    ri = lax.broadcasted_iota(jnp.int32, (LANES, LANES), 0)
    ci = lax.broadcasted_iota(jnp.int32, (LANES, LANES), 1)
    same_head = (ri // RWKV_HEAD) == (ci // RWKV_HEAD)
    ones_bd = jnp.where(same_head, 1.0, 0.0).astype(_BF16)
    eye = ri == ci
    same_blk = (ri // C) == (ci // C)
    strict = jnp.logical_and(same_blk, ci < ri)
    incl = jnp.logical_and(same_blk, ci <= ri)

    def stack(x):
        return jnp.concatenate([jnp.where(lo, x, 0.0), jnp.where(lo, 0.0, x)], axis=0)

    def dot(x, y):
        return jnp.dot(x.astype(_BF16), y.astype(_BF16), preferred_element_type=_F32)

    pairs = range(n_batch * HEAD_PAIRS)
    sls = [slice((n % HEAD_PAIRS) * LANES, (n % HEAD_PAIRS + 1) * LANES) for n in pairs]

    def col(name, n):
        return prepared[n // HEAD_PAIRS][name][:, sls[n]]

    h_bd = [state_ref[p] for p in pairs]
    kk = [col('kk_raw', p) * lax.rsqrt(jnp.maximum(_split_dot(col('kk_raw', p) * col('kk_raw', p), ones_bd), 1e-24))
          for p in pairs]
    b = [kk[p] * col('a', p) for p in pairs]
    lhs4 = [jnp.concatenate([stack(kk[p] * col('g_excl', p)), stack(col('r', p) * col('g_incl', p))],
                            axis=0).astype(_BF16) for p in pairs]
    rhs4 = []
    for p in pairs:
        bi = b[p] * col('g_inv', p)
        ki = col('kt', p) * col('g_inv', p)
        rhs4.append(jnp.concatenate([bi, bi, ki, ki], axis=0).astype(_BF16))
    gm = [lax.dot_general(lhs4[p], rhs4[p], _NT, preferred_element_type=_F32) for p in pairs]
    a_bd = [jnp.where(strict, gm[p][:2 * C, :2 * C], 0.0) for p in pairs]
    bp_bd = [jnp.concatenate([jnp.where(strict, gm[p][:2 * C, 2 * C:], 0.0),
                              jnp.where(incl, gm[p][2 * C:, 2 * C:], 0.0)], axis=0) for p in pairs]
    pb_bd = [jnp.where(incl, gm[p][2 * C:, :2 * C], 0.0) for p in pairs]

    inv = [jnp.where(eye, 1.0, 0.0) - jnp.where((ri // 2) == (ci // 2), a_bd[p], 0.0) for p in pairs]
    s = 2
    while s < C:
        off = jnp.logical_and((ri // (2 * s)) == (ci // (2 * s)), (ri // s) != (ci // s))
        tm = [dot(inv[p], jnp.where(off, a_bd[p], 0.0)) for p in pairs]
        inv = [inv[p] - dot(tm[p], inv[p]) for p in pairs]
        s *= 2

    v_st = [stack(col('v', p)) for p in pairs]
    hw = [dot(lhs4[p], h_bd[p]) for p in pairs]
    bv = [dot(bp_bd[p], v_st[p]) for p in pairs]
    u_st = [dot(inv[p], hw[p][:2 * C] + bv[p][:2 * C]) for p in pairs]
    y_st = [hw[p][2 * C:] + bv[p][2 * C:] - dot(pb_bd[p], u_st[p]) for p in pairs]
    y = [y_st[p][:C] + y_st[p][C:] for p in pairs]
    u = [u_st[p][:C] + u_st[p][C:] for p in pairs]

    for p in pairs:
        m_t = jnp.concatenate([col('kt', p) * col('g_end', p), -(b[p] * col('g_end', p))], axis=0).astype(_BF16)
        n_t = jnp.concatenate([col('v', p), u[p]], axis=0).astype(_BF16)
        dh = lax.dot_general(m_t, n_t, _TN, preferred_element_type=_F32)
        gt_col = jnp.sum(jnp.where(eye, jnp.broadcast_to(col('g_total', p), (LANES, LANES)), 0.0),
                         axis=1, keepdims=True)
        state_ref[p] = jnp.where(same_head, gt_col * h_bd[p] + dh, 0.0)

    mean = [_split_dot(y[p], ones_bd) * (1.0 / RWKV_HEAD) for p in pairs]
    yc = [y[p] - mean[p] for p in pairs]
    var = [_split_dot(yc[p] * yc[p], ones_bd) * (1.0 / RWKV_HEAD) for p in pairs]
    for p in pairs:
        sl = sls[p]
        yn = yc[p] * lax.rsqrt(var[p] + GN_EPS) * gn_w[:, sl] + gn_b[:, sl]
        bonus = _split_dot(col('bonus_raw', p), ones_bd) * col('v', p)
        o_ref[p // HEAD_PAIRS, :, sl] = ((yn + bonus) * col('g', p)).astype(o_ref.dtype)


def rwkv7_mix(zf, mu, w0, w_up, a0, a_up, g_up, k_k, k_a, r_k, gn_w, gn_b, B, S):
    C = RWKV_CHUNK
    nc = S // C
    W = RWKV_WIDTH
    mu_p = jnp.pad(mu, (0, ZF_SLAB_PAD - RWKV_SLAB)).reshape(1, ZF_SLAB_PAD)
    vecs = jnp.stack([w0, a0, k_k, k_a, r_k, gn_w, gn_b, jnp.zeros_like(w0)]).astype(_F32)
    wup = jnp.concatenate([w_up, jnp.zeros((LANES - DECAY_RANK, W), w_up.dtype)], axis=0).astype(_BF16)
    aup = jnp.concatenate([jnp.zeros((DECAY_RANK, W), a_up.dtype), a_up], axis=0).astype(_BF16)
    gup = jnp.concatenate([g_up, jnp.zeros((2 * LANES - GATE_RANK, W), g_up.dtype)], axis=0).astype(_BF16)
    out = pl.pallas_call(
        _rwkv_kernel,
        grid=(nc,),
        in_specs=[pl.BlockSpec((B, C, ZF_SLAB_PAD), lambda c: (0, c, 0)),
                  pl.BlockSpec((1, ZF_SLAB_PAD), lambda c: (0, 0)),
                  pl.BlockSpec((SUBLANES, W), lambda c: (0, 0)),
                  pl.BlockSpec((LANES, W), lambda c: (0, 0)),
                  pl.BlockSpec((LANES, W), lambda c: (0, 0)),
                  pl.BlockSpec((2 * LANES, W), lambda c: (0, 0))],
        out_specs=pl.BlockSpec((B, C, W), lambda c: (0, c, 0)),
        out_shape=jax.ShapeDtypeStruct((B, S, W), _BF16),
        scratch_shapes=[pltpu.VMEM((B, C + SUBLANES, ZF_SLAB_PAD), _F32),
                        pltpu.VMEM((B * HEAD_PAIRS, LANES, LANES), _F32)],
        compiler_params=_params("arbitrary"),
        name="rwkv7_mix",
    )(zf.reshape(B, S, ZF_WIDTH), mu_p, vecs, wup, aup, gup)
    return out.reshape(B * S, W)


def _router_kernel(h_ref, g_ref, wr_ref, xn_ref, route_ref):
    xn = _rms_to_bf16(h_ref[...], g_ref[...])
    xn_ref[...] = xn
    logits = jnp.dot(xn, wr_ref[...], preferred_element_type=_F32)
    lane = lax.broadcasted_iota(jnp.int32, logits.shape, 1).astype(_F32)
    l1 = jnp.where(lane < N_EXPERTS, logits, -jnp.inf)
    m1 = jnp.max(l1, axis=1, keepdims=True)
    i1 = jnp.min(jnp.where(l1 == m1, lane, float(LANES)), axis=1, keepdims=True)
    l2 = jnp.where(lane == i1, -jnp.inf, l1)
    m2 = jnp.max(l2, axis=1, keepdims=True)
    i2 = jnp.min(jnp.where(l2 == m2, lane, float(LANES)), axis=1, keepdims=True)
    e2 = jnp.exp(m2 - m1)
    g1 = 1.0 / (1.0 + e2)
    g2 = e2 / (1.0 + e2)
    route_ref[...] = jnp.where(lane == 0.0, i1, jnp.where(lane == 1.0, i2,
                               jnp.where(lane == 2.0, g1, jnp.where(lane == 3.0, g2, 0.0))))


def moe_route(h, g, router, tm):
    T, D = h.shape
    wr = jnp.pad(router, ((0, 0), (0, LANES - N_EXPERTS))).astype(_BF16)
    return pl.pallas_call(
        _router_kernel,
        grid=(T // tm,),
        in_specs=[pl.BlockSpec((tm, D), lambda i: (i, 0)),
                  pl.BlockSpec((1, D), lambda i: (0, 0)),
                  pl.BlockSpec((D, LANES), lambda i: (0, 0))],
        out_specs=[pl.BlockSpec((tm, D), lambda i: (i, 0)),
                   pl.BlockSpec((tm, LANES), lambda i: (i, 0))],
        out_shape=[jax.ShapeDtypeStruct((T, D), _BF16), jax.ShapeDtypeStruct((T, LANES), _F32)],
        compiler_params=_params("parallel"),
        name="moe_route",
    )(h, g.reshape(1, D), wr)


def _moe_ffn_kernel(te_ref, nv_ref, x_ref, w1_ref, w3_ref, w2_ref, o_ref):
    n = pl.program_id(0)
    f = pl.program_id(1)

    @pl.when(jnp.logical_and(n >= nv_ref[0], f == 0))
    def _():
        o_ref[...] = jnp.zeros(o_ref.shape, o_ref.dtype)

    @pl.when(n < nv_ref[0])
    def _():
        x = x_ref[...]
        a = jnp.dot(x, w1_ref[...], preferred_element_type=_F32)
        b = jnp.dot(x, w3_ref[...], preferred_element_type=_F32)
        mid = (a * _sigmoid(a) * b).astype(_BF16)
        contrib = jnp.dot(mid, w2_ref[...], preferred_element_type=_F32)

        @pl.when(f == 0)
        def _():
            o_ref[...] = contrib

        @pl.when(f > 0)
        def _():
            o_ref[...] += contrib


def moe_grouped_ffn(xs, tile_expert, n_valid, w1, w3, w2):
    R, D = xs.shape
    F = w1.shape[2]
    nf = F // MOE_TF

    def f_idx(n, f, nv):
        return jnp.where(n < nv[0], f, nf - 1)

    return pl.pallas_call(
        _moe_ffn_kernel,
        grid_spec=pltpu.PrefetchScalarGridSpec(
            num_scalar_prefetch=2,
            grid=(R // MOE_TM, nf),
            in_specs=[pl.BlockSpec((MOE_TM, D), lambda n, f, te, nv: (n, 0)),
                      pl.BlockSpec((None, D, MOE_TF), lambda n, f, te, nv: (te[n], 0, f_idx(n, f, nv))),
                      pl.BlockSpec((None, D, MOE_TF), lambda n, f, te, nv: (te[n], 0, f_idx(n, f, nv))),
                      pl.BlockSpec((None, MOE_TF, D), lambda n, f, te, nv: (te[n], f_idx(n, f, nv), 0))],
            out_specs=pl.BlockSpec((MOE_TM, D), lambda n, f, te, nv: (n, 0))),
        out_shape=jax.ShapeDtypeStruct((R, D), _F32),
        compiler_params=_params("parallel", "arbitrary"),
        name="moe_grouped_ffn",
    )(tile_expert, n_valid, xs, w1, w3, w2)


def moe_residual(h, g, router, w1, w3, w2):
    T, D = h.shape
    E = N_EXPERTS
    xn, route = moe_route(h, g, router, tm=ROUTE_TM)
    top = route[:, 0:TOP_K_EXPERTS].astype(jnp.int32)
    gates = route[:, TOP_K_EXPERTS:2 * TOP_K_EXPERTS]
    ef = top.T.reshape(TOP_K_EXPERTS * T)
    onehot = (ef[:, None] == jnp.arange(E, dtype=jnp.int32)[None, :]).astype(jnp.int32)
    csum = jnp.cumsum(onehot, axis=0)
    counts = csum[-1]
    padded = ((counts + MOE_TM - 1) // MOE_TM) * MOE_TM
    pends = jnp.cumsum(padded)
    pstarts = pends - padded
    starts = jnp.cumsum(counts) - counts
    pos = jnp.sum(onehot * (csum - 1 + pstarts[None, :]), axis=1)
    R = TOP_K_EXPERTS * T + E * MOE_TM
    n_tiles = R // MOE_TM
    tile_start = jnp.arange(n_tiles, dtype=jnp.int32) * MOE_TM
    tile_expert = jnp.minimum(jnp.searchsorted(pends, tile_start, side='right'), E - 1).astype(jnp.int32)
    n_valid = (pends[-1:] // MOE_TM).astype(jnp.int32)
    order = jnp.argsort(ef, stable=True).astype(jnp.int32)
    first = starts[tile_expert] + tile_start - pstarts[tile_expert]
    idx = (first[:, None] + jnp.arange(MOE_TM, dtype=jnp.int32)[None, :]).reshape(R)
    src = order[jnp.clip(idx, 0, TOP_K_EXPERTS * T - 1)] % T
    xs = jnp.take(xn, src, axis=0, mode='clip')
    ys = moe_grouped_ffn(xs, tile_expert, n_valid, w1, w3, w2)
    mix = sum(gates[:, slot:slot + 1] * jnp.take(ys, pos[slot * T:(slot + 1) * T], axis=0, mode='clip')
              for slot in range(TOP_K_EXPERTS))
    return h + mix


def _pack_w_in(w):
    wq, wk, wv, wqi, wki, wwi, wslab, wga, wgb = _split(w, IN_SIZES)
    D = w.shape[0]
    wb = jnp.concatenate([wq * (ATT_HEAD_DIM ** -0.5 * LOG2_E), wk, wv, wqi, wki, wki], axis=1)
    wf = jnp.concatenate([wslab, jnp.zeros((D, ZF_SLAB_PAD - RWKV_SLAB), w.dtype),
                          wki, wwi, jnp.zeros((D, LANES - IDX_DIM - IDX_HEADS), w.dtype), wga, wgb], axis=1)
    return wb.astype(_BF16), wf.astype(_BF16)


def kernel(x, p, w_in, att_up, rwkv_up, w_out, rel_bias, rwkv_mu, rwkv_w0, rwkv_w_up,
           rwkv_a0, rwkv_a_up, rwkv_g_up, rwkv_k_k, rwkv_k_a, rwkv_r_k, rwkv_gn_w, rwkv_gn_b,
           norm_mix, norm_ffn, norm_ple, ple_proj, ple_gate, ffn_w1, ffn_w3, ffn_w2,
           moe_router, moe_w1, moe_w3, moe_w2, final_norm):
    B, S, D = x.shape
    T = B * S
    depth = w_in.shape[0]
    h = x.reshape(T, D)
    for i in range(depth):
        wb, wf = _pack_w_in(w_in[i])
        zb = norm_matmul(h, norm_mix[i], wb, _BF16, tm=IN_PROJ_TM, tn=ZB_WIDTH)
        zf = norm_matmul(h, norm_mix[i], wf, _F32, tm=IN_PROJ_F32_TM, tn=IN_PROJ_F32_TN)
        att = dsa_attention(zb, zf, rel_bias, B, S)
        rw = rwkv7_mix(zf, rwkv_mu[i], rwkv_w0[i], rwkv_w_up[i], rwkv_a0[i], rwkv_a_up[i], rwkv_g_up[i],
                       rwkv_k_k[i], rwkv_k_a[i], rwkv_r_k[i], rwkv_gn_w[i], rwkv_gn_b[i], B, S)
        merged = gated_merge(att, rw, zf, att_up[i].astype(_BF16), rwkv_up[i].astype(_BF16),
                             tm=MERGE_TM, tn=MERGE_TN)
        h = matmul_residual(merged, w_out[i].astype(_BF16), h, tm=OUT_PROJ_TM, tn=D)
        if i % 2 == 0:
            j = i // 2
            h = ffn_residual(h, norm_ffn[i], ffn_w1[j].astype(_BF16), ffn_w3[j].astype(_BF16),
                             ffn_w2[j].astype(_BF16), tm=FFN_TM, tf=FFN_TF)
        else:
            j = i // 2
            h = moe_residual(h, norm_ffn[i], moe_router[j], cast_bf16(moe_w1[j]), cast_bf16(moe_w3[j]),
                             cast_bf16(moe_w2[j]))
        h = ple_residual(h, norm_ple[i], p[i].reshape(T, PLE_DIM), ple_gate[i].astype(_BF16),
                         ple_proj[i].astype(_BF16), final_norm, final=(i == depth - 1), tm=PLE_TM)
    return h.reshape(B, S, D)
```

```python
import functools
import math

import jax
import jax.numpy as jnp
import numpy as np
from jax import lax
from jax.experimental import pallas as pl
from jax.experimental.pallas import tpu as pltpu

D_MODEL = 2048
DEPTH = 2
ATT_HEADS = 8
ATT_KV_HEADS = 2
ATT_HEAD_DIM = 128
IDX_HEADS = 8
IDX_DIM = 64
TOPK_MAX = 256
REL_BUCKETS = 32
REL_MAX_DIST = 128
RWKV_HEAD = 64
RWKV_HEADS = 16
DECAY_RANK = 64
ICLR_RANK = 64
GATE_RANK = 160
D_FF = 5632
N_EXPERTS = 8
TOP_K_EXPERTS = 2
D_FF_EXPERT = 7168
PLE_DIM = 256
RMS_EPS = 1e-6
GN_EPS = 64e-5

ATT_WIDTH = ATT_HEADS * ATT_HEAD_DIM
KV_WIDTH = ATT_KV_HEADS * ATT_HEAD_DIM
RWKV_WIDTH = RWKV_HEADS * RWKV_HEAD
RWKV_SLAB = 3 * RWKV_WIDTH + DECAY_RANK + ICLR_RANK + GATE_RANK
IN_SIZES = (ATT_WIDTH, KV_WIDTH, KV_WIDTH, IDX_HEADS * IDX_DIM, IDX_DIM, IDX_HEADS, RWKV_SLAB, D_MODEL, D_MODEL)

LANES = 128
SUBLANES = 8
VMEM_LIMIT = 56 * 1024 * 1024
CAST_BLOCK_BYTES = 8 * 1024 * 1024

IN_PROJ_TM = 1024
IN_PROJ_F32_TM = 1024
IN_PROJ_F32_TN = 1536
MERGE_TM, MERGE_TN = 1024, 512
OUT_PROJ_TM = 512
FFN_TM, FFN_TF = 512, 512
ROUTE_TM = 512
PLE_TM = 512

Q_BLOCK = 128
KEY_CHUNK = 512
BLOCKS_PER_CHUNK = KEY_CHUNK // Q_BLOCK
BLOCKS_PER_GROUP = 32
MASKED_LOGIT = -1e30
LOG2_E = math.log2(math.e)
GQA_GROUP = ATT_HEADS // ATT_KV_HEADS
HEADS_PER_UNIT = 2

ZB_WIDTH = ATT_WIDTH + 2 * KV_WIDTH + IDX_HEADS * IDX_DIM + 2 * IDX_DIM
ZF_SLAB_PAD = 3456
ZF_WIDX = ZF_SLAB_PAD
ZF_GATE_A = ZF_WIDX + LANES
ZF_GATE_B = ZF_GATE_A + D_MODEL
ZF_WIDTH = ZF_GATE_B + D_MODEL

RWKV_CHUNK = 64
HEAD_PAIRS = RWKV_HEADS // 2
MOE_TM = 512
MOE_TF = 1024

_F32 = jnp.float32
_BF16 = jnp.bfloat16
_NT = (((1,), (1,)), ((), ()))
_TN = (((0,), (0,)), ((), ()))


def _split(z, sizes):
    return jnp.split(z, np.cumsum(sizes)[:-1].tolist(), axis=-1)


def _sigmoid(x):
    return 1.0 / (1.0 + jnp.exp(-x))


def _rms_to_bf16(x, g):
    y = x * lax.rsqrt(jnp.mean(x * x, axis=-1, keepdims=True) + RMS_EPS)
    return (y * g).astype(_BF16)


def _params(*sem):
    return pltpu.CompilerParams(dimension_semantics=sem, vmem_limit_bytes=VMEM_LIMIT)


def _norm_mm_kernel(x_ref, g_ref, w_ref, o_ref, xn_ref):
    @pl.when(pl.program_id(1) == 0)
    def _():
        xn_ref[...] = _rms_to_bf16(x_ref[...], g_ref[...])

    o_ref[...] = jnp.dot(xn_ref[...], w_ref[...], preferred_element_type=_F32).astype(o_ref.dtype)


def norm_matmul(x, g, w, out_dtype, tm, tn):
    T, D = x.shape
    N = w.shape[1]
    tm = min(tm, T)
    return pl.pallas_call(
        _norm_mm_kernel,
        grid=(T // tm, N // tn),
        in_specs=[pl.BlockSpec((tm, D), lambda i, j: (i, 0)),
                  pl.BlockSpec((1, D), lambda i, j: (0, 0)),
                  pl.BlockSpec((D, tn), lambda i, j: (0, j))],
        out_specs=pl.BlockSpec((tm, tn), lambda i, j: (i, j)),
        out_shape=jax.ShapeDtypeStruct((T, N), out_dtype),
        scratch_shapes=[pltpu.VMEM((tm, D), _BF16)],
        compiler_params=_params("parallel", "arbitrary"),
        name="norm_matmul",
    )(x, g.reshape(1, D), w)


def _merge_kernel(att_ref, rw_ref, ga_ref, gb_ref, wa_ref, wr_ref, o_ref):
    a = jnp.dot(att_ref[...], wa_ref[...], preferred_element_type=_F32)
    r = jnp.dot(rw_ref[...], wr_ref[...], preferred_element_type=_F32)
    o_ref[...] = (_sigmoid(ga_ref[...]) * a + _sigmoid(gb_ref[...]) * r).astype(o_ref.dtype)


def gated_merge(att, rw, zf, wa, wr, tm, tn):
    T, K = att.shape
    N = wa.shape[1]
    assert ZF_GATE_A % tn == 0 and ZF_GATE_B % tn == 0
    ja, jb = ZF_GATE_A // tn, ZF_GATE_B // tn
    return pl.pallas_call(
        _merge_kernel,
        grid=(T // tm, N // tn),
        in_specs=[pl.BlockSpec((tm, K), lambda i, j: (i, 0)),
                  pl.BlockSpec((tm, K), lambda i, j: (i, 0)),
                  pl.BlockSpec((tm, tn), lambda i, j: (i, ja + j)),
                  pl.BlockSpec((tm, tn), lambda i, j: (i, jb + j)),
                  pl.BlockSpec((K, tn), lambda i, j: (0, j)),
                  pl.BlockSpec((K, tn), lambda i, j: (0, j))],
        out_specs=pl.BlockSpec((tm, tn), lambda i, j: (i, j)),
        out_shape=jax.ShapeDtypeStruct((T, N), _BF16),
        compiler_params=_params("parallel", "arbitrary"),
        name="gated_merge",
    )(att, rw, zf, zf, wa, wr)


def _mm_res_kernel(x_ref, w_ref, h_ref, o_ref):
    o_ref[...] = h_ref[...] + jnp.dot(x_ref[...], w_ref[...], preferred_element_type=_F32)


def matmul_residual(x, w, h, tm, tn):
    T, K = x.shape
    N = w.shape[1]
    return pl.pallas_call(
        _mm_res_kernel,
        grid=(T // tm, N // tn),
        in_specs=[pl.BlockSpec((tm, K), lambda i, j: (i, 0)),
                  pl.BlockSpec((K, tn), lambda i, j: (0, j)),
                  pl.BlockSpec((tm, tn), lambda i, j: (i, j))],
        out_specs=pl.BlockSpec((tm, tn), lambda i, j: (i, j)),
        out_shape=jax.ShapeDtypeStruct((T, N), _F32),
        compiler_params=_params("parallel", "arbitrary"),
        name="matmul_residual",
    )(x, w, h)


def _ffn_kernel(h_ref, g_ref, w1_ref, w3_ref, w2_ref, o_ref, xn_ref):
    f = pl.program_id(1)

    @pl.when(f == 0)
    def _():
        xn_ref[...] = _rms_to_bf16(h_ref[...], g_ref[...])
        o_ref[...] = h_ref[...]

    xn = xn_ref[...]
    a = jnp.dot(xn, w1_ref[...], preferred_element_type=_F32)
    b = jnp.dot(xn, w3_ref[...], preferred_element_type=_F32)
    mid = (a * _sigmoid(a) * b).astype(_BF16)
    o_ref[...] += jnp.dot(mid, w2_ref[...], preferred_element_type=_F32)


def ffn_residual(h, g, w1, w3, w2, tm, tf):
    T, D = h.shape
    F = w1.shape[1]
    return pl.pallas_call(
        _ffn_kernel,
        grid=(T // tm, F // tf),
        in_specs=[pl.BlockSpec((tm, D), lambda i, f: (i, 0)),
                  pl.BlockSpec((1, D), lambda i, f: (0, 0)),
                  pl.BlockSpec((D, tf), lambda i, f: (0, f)),
                  pl.BlockSpec((D, tf), lambda i, f: (0, f)),
                  pl.BlockSpec((tf, D), lambda i, f: (f, 0))],
        out_specs=pl.BlockSpec((tm, D), lambda i, f: (i, 0)),
        out_shape=jax.ShapeDtypeStruct((T, D), _F32),
        scratch_shapes=[pltpu.VMEM((tm, D), _BF16)],
        compiler_params=_params("parallel", "arbitrary"),
        name="ffn_residual",
    )(h, g.reshape(1, D), w1, w3, w2)


def _ple_kernel(h_ref, g_ref, p_ref, wg_ref, wp_ref, fg_ref, o_ref, *, final):
    h = h_ref[...]
    xn = _rms_to_bf16(h, g_ref[...])
    gate = _sigmoid(jnp.dot(xn, wg_ref[...], preferred_element_type=_F32))
    pp = jnp.dot(p_ref[...].astype(_BF16), wp_ref[...], preferred_element_type=_F32)
    out = h + gate * pp
    if final:
        out = out * lax.rsqrt(jnp.mean(out * out, axis=-1, keepdims=True) + RMS_EPS) * fg_ref[...]
    o_ref[...] = out


def ple_residual(h, g, p, wg, wp, final_g, final, tm):
    T, D = h.shape
    P = p.shape[1]
    return pl.pallas_call(
        functools.partial(_ple_kernel, final=final),
        grid=(T // tm,),
        in_specs=[pl.BlockSpec((tm, D), lambda i: (i, 0)),
                  pl.BlockSpec((1, D), lambda i: (0, 0)),
                  pl.BlockSpec((tm, P), lambda i: (i, 0)),
                  pl.BlockSpec((D, D), lambda i: (0, 0)),
                  pl.BlockSpec((P, D), lambda i: (0, 0)),
                  pl.BlockSpec((1, D), lambda i: (0, 0))],
        out_specs=pl.BlockSpec((tm, D), lambda i: (i, 0)),
        out_shape=jax.ShapeDtypeStruct((T, D), _F32),
        compiler_params=_params("parallel"),
        name="ple_residual",
    )(h, g.reshape(1, D), p, wg, wp, final_g.reshape(1, D))


def _cast_kernel(x_ref, o_ref):
    o_ref[...] = x_ref[...].astype(o_ref.dtype)


def cast_bf16(w):
    E, R, N = w.shape
    rows = R
    while rows * N * 4 > CAST_BLOCK_BYTES and rows % 2 == 0:
        rows //= 2
    return pl.pallas_call(
        _cast_kernel,
        grid=(E, R // rows),
        in_specs=[pl.BlockSpec((None, rows, N), lambda e, i: (e, i, 0))],
        out_specs=pl.BlockSpec((None, rows, N), lambda e, i: (e, i, 0)),
        out_shape=jax.ShapeDtypeStruct(w.shape, _BF16),
        compiler_params=_params("parallel", "parallel"),
        name="cast_bf16",
    )(w)


def rel_bucket(dist):
    max_exact = REL_BUCKETS // 2
    dist = jnp.maximum(dist, 0)
    d_f = jnp.maximum(dist, 1).astype(_F32)
    large = max_exact + (jnp.log(d_f / max_exact) / math.log(REL_MAX_DIST / max_exact)
                         * (REL_BUCKETS - max_exact)).astype(jnp.int32)
    large = jnp.minimum(large, REL_BUCKETS - 1)
    return jnp.where(dist < max_exact, dist, large)


def _sortable(x):
    b = lax.bitcast_convert_type(x, jnp.int32)
    return b ^ ((b >> 31) & jnp.int32(0x7FFFFFFF))


_INT_MIN = -2 ** 31
_KEY_NEG_INF = 0x807FFFFF - 2 ** 32


def _dsa_kernel(q_ref, qi_ref, wi_ref, k_ref, v_ref, ki_ref, bprev_ref, bdiag_ref, bfar_ref, o_ref,
                keys_ref, planes_ref, active_ref, qs_ref, qim_ref, m_ref, acc_ref, *, n_sel, idx_bits):
    n_grp = planes_ref.shape[0]
    qb = pl.program_id(1)
    t0 = qb * Q_BLOCK
    n_chunks = qb // BLOCKS_PER_CHUNK + 1
    t_idx = t0 + lax.broadcasted_iota(jnp.int32, (Q_BLOCK, 1), 0)
    lane_blk = lax.broadcasted_iota(jnp.int32, (1, Q_BLOCK), 1)
    lane_chunk = lax.broadcasted_iota(jnp.int32, (1, KEY_CHUNK), 1)

    for hh in range(ATT_HEADS):
        g, h = divmod(hh, GQA_GROUP)
        qs_ref[g, h * Q_BLOCK:(h + 1) * Q_BLOCK, :] = q_ref[:, hh * ATT_HEAD_DIM:(hh + 1) * ATT_HEAD_DIM]
    low = lax.broadcasted_iota(jnp.int32, (Q_BLOCK, LANES), 1) < IDX_DIM
    for hh in range(IDX_HEADS):
        tile = qi_ref[:, (hh // 2) * LANES:(hh // 2 + 1) * LANES]
        keep = low if hh % 2 == 0 else jnp.logical_not(low)
        qim_ref[hh] = jnp.where(keep, tile, jnp.zeros_like(tile))
    m_ref[...] = jnp.full(m_ref.shape, MASKED_LOGIT, _F32)

    @pl.when(qb == 0)
    def _():
        keys_ref[BLOCKS_PER_CHUNK:] = jnp.zeros((keys_ref.shape[0] - BLOCKS_PER_CHUNK, Q_BLOCK, Q_BLOCK), jnp.int32)
        planes_ref[...] = jnp.zeros(planes_ref.shape, jnp.int32)
    acc_ref[...] = jnp.zeros(acc_ref.shape, _F32)

    w_all = wi_ref[...]
    idx_scale = (IDX_HEADS ** -0.5) * (IDX_DIM ** -0.5)

    def score_body(c, carry):
        ks = pl.multiple_of(c * KEY_CHUNK, KEY_CHUNK)
        kic = ki_ref[pl.ds(ks, KEY_CHUNK), :]
        d = lax.dot_general(qim_ref[...].reshape(IDX_HEADS * Q_BLOCK, LANES), kic, _NT,
                            preferred_element_type=_F32)
        sc = jnp.zeros((Q_BLOCK, KEY_CHUNK), _F32)
        for hh in range(IDX_HEADS):
            sc = sc + w_all[:, IDX_DIM + hh:IDX_DIM + hh + 1] * jnp.maximum(d[hh * Q_BLOCK:(hh + 1) * Q_BLOCK], 0.0)
        sc = sc * idx_scale
        sc = jnp.where(sc == 0.0, 0.0, sc)
        key = jnp.where(ks + lane_chunk <= t_idx, _sortable(sc), _KEY_NEG_INF)
        for j in range(BLOCKS_PER_CHUNK):
            keys_ref[c * BLOCKS_PER_CHUNK + j] = key[:, j * Q_BLOCK:(j + 1) * Q_BLOCK]
        return carry

    lax.fori_loop(0, n_chunks, score_body, 0)

    def count(pred):
        def body(c, acc):
            for j in range(BLOCKS_PER_CHUNK):
                blk = c * BLOCKS_PER_CHUNK + j
                acc = acc + jnp.where(pred(keys_ref[blk], blk * Q_BLOCK + lane_blk), 1.0, 0.0)
            return acc

        acc = lax.fori_loop(0, n_chunks, body, jnp.zeros((Q_BLOCK, Q_BLOCK), _F32))
        return jnp.sum(acc, axis=1, keepdims=True)

    n_blocks = n_chunks * BLOCKS_PER_CHUNK
    n_groups = (n_blocks + BLOCKS_PER_GROUP - 1) // BLOCKS_PER_GROUP

    def transpose_group(g, carry):
        def rows(rg, carry_rows):
            r0 = pl.multiple_of(rg * SUBLANES, SUBLANES)
            x = [keys_ref[g * BLOCKS_PER_GROUP + j, pl.ds(r0, SUBLANES), :] for j in range(BLOCKS_PER_GROUP)]
            j, m = 16, 0x0000FFFF
            while j:
                k = 0
                while k < BLOCKS_PER_GROUP:
                    t = (x[k] ^ lax.shift_right_logical(x[k + j], jnp.int32(j))) & jnp.int32(m)
                    x[k] = x[k] ^ t
                    x[k + j] = x[k + j] ^ lax.shift_left(t, jnp.int32(j))
                    k = (k + j + 1) & ~j
                j >>= 1
                m ^= m << j
            planes_ref[g, 0, pl.ds(r0, SUBLANES), :] = ~x[0]
            for i in range(1, 32):
                planes_ref[g, i, pl.ds(r0, SUBLANES), :] = x[i]
            return carry_rows

        lax.fori_loop(0, Q_BLOCK // SUBLANES, rows, 0)
        n_live = jnp.minimum(n_blocks - g * BLOCKS_PER_GROUP, BLOCKS_PER_GROUP)
        active_ref[g] = lax.shift_left(jnp.full((Q_BLOCK, Q_BLOCK), -1, jnp.int32), BLOCKS_PER_GROUP - n_live)
        return carry

    active_ref[...] = jnp.zeros(active_ref.shape, jnp.int32)
    lax.fori_loop(0, n_groups, transpose_group, 0)

    def lane_total(words):
        return jnp.sum(sum(lax.population_count(w) for w in words).astype(_F32), axis=1, keepdims=True)

    def split_active(g, i):
        act = active_ref[g]
        hi = act & planes_ref[g, 2 * i]
        lo = act ^ hi
        hi1 = hi & planes_ref[g, 2 * i + 1]
        lo1 = lo & planes_ref[g, 2 * i + 1]
        return hi1, hi ^ hi1, lo1, lo ^ lo1

    def radix_pass(i, carry):
        prefix, above = carry
        parts = [split_active(g, i) for g in range(n_grp)]
        c11, c10, c01 = (lane_total([parts[g][q] for g in range(n_grp)]) for q in range(3))
        upto10 = above + c11 + c10
        t11 = above + c11 >= n_sel
        t10 = upto10 >= n_sel
        t01 = upto10 + c01 >= n_sel
        for g in range(n_grp):
            s11, s10, s01, s00 = split_active(g, i)
            active_ref[g] = jnp.where(t11, s11, jnp.where(t10, s10, jnp.where(t01, s01, s00)))
        bit1 = jnp.where(t10, lax.shift_left(jnp.int32(1), 31 - 2 * i), 0)
        bit0 = jnp.where(jnp.logical_or(t11, jnp.logical_and(jnp.logical_not(t10), t01)),
                         lax.shift_left(jnp.int32(1), 30 - 2 * i), 0)
        above = jnp.where(t11, above, jnp.where(t10, above + c11, jnp.where(t01, upto10, upto10 + c01)))
        return prefix | bit1 | bit0, above

    prefix, n_gt = lax.fori_loop(0, 16, radix_pass, (jnp.zeros((Q_BLOCK, 1), jnp.int32),
                                                     jnp.zeros((Q_BLOCK, 1), _F32)))
    tau = prefix ^ jnp.int32(_INT_MIN)
    n_eq = lane_total([active_ref[g] for g in range(n_grp)])
    tied = jnp.logical_and(n_gt + n_eq > n_sel, tau > _KEY_NEG_INF)

    def tie_search():
        need = n_sel - count(lambda key, sidx: key > tau)

        def index_bit(i, x):
            cand = x | lax.shift_left(jnp.int32(1), idx_bits - 1 - i)
            cnt = count(lambda key, sidx: jnp.logical_and(key == tau, sidx < cand))
            return jnp.where(cnt < need, cand, x)

        return lax.fori_loop(0, idx_bits, index_bit, jnp.zeros((Q_BLOCK, 1), jnp.int32))

    last_tie = lax.cond(jnp.max(jnp.where(tied, 1.0, 0.0)) > 0.0, tie_search,
                        lambda: jnp.full((Q_BLOCK, 1), 2 ** idx_bits, jnp.int32))

    def attend(key, k_tile, v_tile, s0, lane, bias_ref, enable=None, const_bias=False):
        width = key.shape[1]
        sidx = s0 + lane
        sel = jnp.logical_or(key > tau, jnp.logical_and(key == tau, sidx <= last_tie))
        sel = jnp.logical_and(sel, key > _KEY_NEG_INF)
        if enable is not None:
            sel = jnp.logical_and(sel, jnp.zeros_like(sidx) + enable > 0)
        ones = jnp.ones((width, ATT_HEAD_DIM), _BF16)
        groups = range(ATT_KV_HEADS)
        cols = [slice(g * ATT_HEAD_DIM, (g + 1) * ATT_HEAD_DIM) for g in groups]

        def logits(g, h0, nh):
            return lax.dot_general(qs_ref[g, h0 * Q_BLOCK:(h0 + nh) * Q_BLOCK], k_tile[:, cols[g]], _NT,
                                   preferred_element_type=_F32).reshape(nh, Q_BLOCK, width)

        def softmax(g, h0, nh, s):
            bias = bias_ref[g * GQA_GROUP + h0:g * GQA_GROUP + h0 + nh]
            m_old = m_ref[g, h0:h0 + nh]
            if const_bias:
                sh = jnp.where(sel[None], s, MASKED_LOGIT)
                m_new = jnp.maximum(m_old, jnp.max(sh, axis=2, keepdims=True) + bias)
                shift = m_new - bias
            else:
                sh = jnp.where(sel[None], s + bias, MASKED_LOGIT)
                m_new = jnp.maximum(m_old, jnp.max(sh, axis=2, keepdims=True))
                shift = m_new
            m_ref[g, h0:h0 + nh] = m_new
            return (jnp.exp2(sh - shift).reshape(nh * Q_BLOCK, width).astype(_BF16),
                    jnp.exp2(m_old - m_new).reshape(nh * Q_BLOCK, 1))

        def accumulate(g, h0, nh, p, alpha):
            rows = slice(h0 * Q_BLOCK, (h0 + nh) * Q_BLOCK)
            pv = jnp.dot(p, jnp.concatenate([v_tile[:, cols[g]], ones], axis=1), preferred_element_type=_F32)
            acc_ref[g, rows] = acc_ref[g, rows] * alpha + pv

        if width <= Q_BLOCK:
            units = [(g, 0, GQA_GROUP) for g in groups]
            s = [logits(*u) for u in units]
            pa = [softmax(*u, s[i]) for i, u in enumerate(units)]
            for i, u in enumerate(units):
                accumulate(*u, *pa[i])
        else:
            for u in [(g, h0, HEADS_PER_UNIT) for g in groups for h0 in range(0, GQA_GROUP, HEADS_PER_UNIT)]:
                accumulate(*u, *softmax(*u, logits(*u)))

    n_far = jnp.maximum(qb - 1, 0) // BLOCKS_PER_CHUNK

    def far_body(c, carry):
        ks = pl.multiple_of(c * KEY_CHUNK, KEY_CHUNK)
        key = jnp.concatenate([keys_ref[c * BLOCKS_PER_CHUNK + j] for j in range(BLOCKS_PER_CHUNK)], axis=1)
        attend(key, k_ref[pl.ds(ks, KEY_CHUNK), :], v_ref[pl.ds(ks, KEY_CHUNK), :], ks, lane_chunk, bfar_ref,
               const_bias=True)
        return carry

    lax.fori_loop(0, n_far, far_body, 0)

    def block_step(j, bias_ref, enable=None, const_bias=False):
        ks = pl.multiple_of(j * Q_BLOCK, Q_BLOCK)
        attend(keys_ref[j], k_ref[pl.ds(ks, Q_BLOCK), :], v_ref[pl.ds(ks, Q_BLOCK), :], ks, lane_blk, bias_ref,
               enable, const_bias)

    def tail_body(j, carry):
        block_step(j, bfar_ref, const_bias=True)
        return carry

    lax.fori_loop(n_far * BLOCKS_PER_CHUNK, qb - 1, tail_body, 0)
    block_step(jnp.maximum(qb - 1, 0), bprev_ref, enable=jnp.minimum(qb, 1))
    block_step(qb, bdiag_ref)

    for hh in range(ATT_HEADS):
        g, h = divmod(hh, GQA_GROUP)
        rows = slice(h * Q_BLOCK, (h + 1) * Q_BLOCK)
        o_ref[:, hh * ATT_HEAD_DIM:(hh + 1) * ATT_HEAD_DIM] = (
            acc_ref[g, rows, 0:ATT_HEAD_DIM] / acc_ref[g, rows, ATT_HEAD_DIM:2 * ATT_HEAD_DIM]).astype(o_ref.dtype)


def dsa_attention(zb, zf, rel_bias, B, S):
    nq = S // Q_BLOCK
    n_grp = -(-nq // BLOCKS_PER_GROUP)
    n_sel = min(TOPK_MAX, S // 4)
    tl = jnp.arange(Q_BLOCK, dtype=jnp.int32)[:, None]
    sr = jnp.arange(Q_BLOCK, dtype=jnp.int32)[None, :]
    rel_bias = rel_bias.astype(_F32) * LOG2_E

    def bias_table(dist):
        hit = rel_bucket(dist)[None, :, :, None] == jnp.arange(REL_BUCKETS, dtype=jnp.int32)
        return jnp.sum(jnp.where(hit, rel_bias.T[:, None, None, :], 0.0), axis=-1)

    bprev = bias_table(tl + Q_BLOCK - sr)
    bdiag = bias_table(tl - sr)
    bfar = rel_bias[rel_bucket(jnp.int32(REL_MAX_DIST + 1))].reshape(ATT_HEADS, 1, 1)
    kernel = functools.partial(_dsa_kernel, n_sel=float(n_sel), idx_bits=int(S - 1).bit_length())
    c_k = ATT_WIDTH // KV_WIDTH
    c_qi = (ATT_WIDTH + 2 * KV_WIDTH) // (IDX_HEADS * IDX_DIM)
    c_ki = (ATT_WIDTH + 2 * KV_WIDTH + IDX_HEADS * IDX_DIM) // LANES
    return pl.pallas_call(
        kernel,
        grid=(B, nq),
        in_specs=[pl.BlockSpec((Q_BLOCK, ATT_WIDTH), lambda b, i: (b * nq + i, 0)),
                  pl.BlockSpec((Q_BLOCK, IDX_HEADS * IDX_DIM), lambda b, i: (b * nq + i, c_qi)),
                  pl.BlockSpec((Q_BLOCK, LANES), lambda b, i: (b * nq + i, ZF_WIDX // LANES)),
                  pl.BlockSpec((S, KV_WIDTH), lambda b, i: (b, c_k)),
                  pl.BlockSpec((S, KV_WIDTH), lambda b, i: (b, c_k + 1)),
                  pl.BlockSpec((S, LANES), lambda b, i: (b, c_ki)),
                  pl.BlockSpec((ATT_HEADS, Q_BLOCK, Q_BLOCK), lambda b, i: (0, 0, 0)),
                  pl.BlockSpec((ATT_HEADS, Q_BLOCK, Q_BLOCK), lambda b, i: (0, 0, 0)),
                  pl.BlockSpec((ATT_HEADS, 1, 1), lambda b, i: (0, 0, 0))],
        out_specs=pl.BlockSpec((Q_BLOCK, ATT_WIDTH), lambda b, i: (b * nq + i, 0)),
        out_shape=jax.ShapeDtypeStruct((B * S, ATT_WIDTH), _BF16),
        scratch_shapes=[pltpu.VMEM((n_grp * BLOCKS_PER_GROUP, Q_BLOCK, Q_BLOCK), jnp.int32),
                        pltpu.VMEM((n_grp, 32, Q_BLOCK, Q_BLOCK), jnp.int32),
                        pltpu.VMEM((n_grp, Q_BLOCK, Q_BLOCK), jnp.int32),
                        pltpu.VMEM((ATT_KV_HEADS, GQA_GROUP * Q_BLOCK, ATT_HEAD_DIM), _BF16),
                        pltpu.VMEM((IDX_HEADS, Q_BLOCK, LANES), _BF16),
                        pltpu.VMEM((ATT_KV_HEADS, GQA_GROUP, Q_BLOCK, 1), _F32),
                        pltpu.VMEM((ATT_KV_HEADS, GQA_GROUP * Q_BLOCK, 2 * ATT_HEAD_DIM), _F32)],
        compiler_params=_params("parallel", "arbitrary"),
        name="dsa_attention",
    )(zb, zb, zf, zb, zb, zb, bprev, bdiag, bfar)


def _split_dot(x, w):
    hi = x.astype(_BF16)
    lo = (x - hi.astype(_F32)).astype(_BF16)
    return jnp.dot(hi, w, preferred_element_type=_F32) + jnp.dot(lo, w, preferred_element_type=_F32)


def _split_dot_left(w, x):
    hi = x.astype(_BF16)
    lo = (x - hi.astype(_F32)).astype(_BF16)
    return jnp.dot(w, hi, preferred_element_type=_F32) + jnp.dot(w, lo, preferred_element_type=_F32)


def _rwkv_kernel(slab_ref, mu_ref, vec_ref, wup_ref, aup_ref, gup_ref, o_ref, xs_ref, state_ref):
    C = RWKV_CHUNK
    W = RWKV_WIDTH
    n_batch = slab_ref.shape[0]
    c = pl.program_id(0)

    @pl.when(c == 0)
    def _():
        xs_ref[:, 0:SUBLANES, :] = jnp.zeros((n_batch, SUBLANES, ZF_SLAB_PAD), _F32)
        state_ref[...] = jnp.zeros(state_ref.shape, _F32)

    w0, a0, k_k, k_a, r_k, gn_w, gn_b = (vec_ref[i:i + 1, :] for i in range(7))
    ti = lax.broadcasted_iota(jnp.int32, (C, C), 0)
    tj = lax.broadcasted_iota(jnp.int32, (C, C), 1)
    tri = jnp.where(tj <= ti, 1.0, 0.0).astype(_BF16)

    def prepare(bi):
        slab = slab_ref[bi]
        xs_ref[bi, SUBLANES:SUBLANES + C, :] = slab
        shifted = xs_ref[bi, SUBLANES - 1:SUBLANES - 1 + C, :]
        xs = slab + (shifted - slab) * mu_ref[...]
        xs_ref[bi, 0:SUBLANES, :] = slab[C - SUBLANES:C, :]

        r = xs[:, 0:W]
        k = xs[:, W:2 * W]
        lora = xs[:, 3 * W:3 * W + LANES]
        xg = xs[:, 3 * W + LANES:3 * W + 3 * LANES]
        d = w0 + jnp.dot(jnp.tanh(lora).astype(_BF16), wup_ref[...], preferred_element_type=_F32)
        logw = -math.exp(-0.5) * _sigmoid(d)
        a = _sigmoid(a0 + jnp.dot(lora.astype(_BF16), aup_ref[...], preferred_element_type=_F32))
        kt = k * (1.0 + (a - 1.0) * k_a)
        cum = _split_dot_left(tri, logw)
        cum_end = cum[C - 1:C, :]
        return dict(r=r, v=xs[:, 2 * W:3 * W], a=a, kt=kt, kk_raw=k * k_k, bonus_raw=r * kt * r_k,
                    g=jnp.dot(_sigmoid(xg).astype(_BF16), gup_ref[...], preferred_element_type=_F32),
                    g_incl=jnp.exp(cum), g_excl=jnp.exp(cum - logw), g_inv=jnp.exp(-cum),
                    g_end=jnp.exp(cum_end - cum), g_total=jnp.exp(cum_end))

    prepared = [prepare(bi) for bi in range(n_batch)]

    lo = lax.broadcasted_iota(jnp.int32, (C, LANES), 1) < RWKV_HEAD
    ri = lax.broadcasted_iota(jnp.int32, (LANES, LANES), 0)
    ci = lax.broadcasted_iota(jnp.int32, (LANES, LANES), 1)
    same_head = (ri // RWKV_HEAD) == (ci // RWKV_HEAD)
    ones_bd = jnp.where(same_head, 1.0, 0.0).astype(_BF16)
    eye = ri == ci
    same_blk = (ri // C) == (ci // C)
    strict = jnp.logical_and(same_blk, ci < ri)
    incl = jnp.logical_and(same_blk, ci <= ri)

    def stack(x):
        return jnp.concatenate([jnp.where(lo, x, 0.0), jnp.where(lo, 0.0, x)], axis=0)

    def dot(x, y):
        return jnp.dot(x.astype(_BF16), y.astype(_BF16), preferred_element_type=_F32)

    pairs = range(n_batch * HEAD_PAIRS)
    sls = [slice((n % HEAD_PAIRS) * LANES, (n % HEAD_PAIRS + 1) * LANES) for n in pairs]

    def col(name, n):
        return prepared[n // HEAD_PAIRS][name][:, sls[n]]

    h_bd = [state_ref[p] for p in pairs]
    kk = [col('kk_raw', p) * lax.rsqrt(jnp.maximum(_split_dot(col('kk_raw', p) * col('kk_raw', p), ones_bd), 1e-24))
          for p in pairs]
    b = [kk[p] * col('a', p) for p in pairs]
    lhs4 = [jnp.concatenate([stack(kk[p] * col('g_excl', p)), stack(col('r', p) * col('g_incl', p))],
                            axis=0).astype(_BF16) for p in pairs]
    rhs4 = []
    for p in pairs:
        bi = b[p] * col('g_inv', p)
        ki = col('kt', p) * col('g_inv', p)
        rhs4.append(jnp.concatenate([bi, bi, ki, ki], axis=0).astype(_BF16))
    gm = [lax.dot_general(lhs4[p], rhs4[p], _NT, preferred_element_type=_F32) for p in pairs]
    a_bd = [jnp.where(strict, gm[p][:2 * C, :2 * C], 0.0) for p in pairs]
    bp_bd = [jnp.concatenate([jnp.where(strict, gm[p][:2 * C, 2 * C:], 0.0),
                              jnp.where(incl, gm[p][2 * C:, 2 * C:], 0.0)], axis=0) for p in pairs]
    pb_bd = [jnp.where(incl, gm[p][2 * C:, :2 * C], 0.0) for p in pairs]

    inv = [jnp.where(eye, 1.0, 0.0) - jnp.where((ri // 2) == (ci // 2), a_bd[p], 0.0) for p in pairs]
    s = 2
    while s < C:
        off = jnp.logical_and((ri // (2 * s)) == (ci // (2 * s)), (ri // s) != (ci // s))
        tm = [dot(inv[p], jnp.where(off, a_bd[p], 0.0)) for p in pairs]
        inv = [inv[p] - dot(tm[p], inv[p]) for p in pairs]
        s *= 2

    v_st = [stack(col('v', p)) for p in pairs]
    hw = [dot(lhs4[p], h_bd[p]) for p in pairs]
    bv = [dot(bp_bd[p], v_st[p]) for p in pairs]
    u_st = [dot(inv[p], hw[p][:2 * C] + bv[p][:2 * C]) for p in pairs]
    y_st = [hw[p][2 * C:] + bv[p][2 * C:] - dot(pb_bd[p], u_st[p]) for p in pairs]
    y = [y_st[p][:C] + y_st[p][C:] for p in pairs]
    u = [u_st[p][:C] + u_st[p][C:] for p in pairs]

    for p in pairs:
        m_t = jnp.concatenate([col('kt', p) * col('g_end', p), -(b[p] * col('g_end', p))], axis=0).astype(_BF16)
        n_t = jnp.concatenate([col('v', p), u[p]], axis=0).astype(_BF16)
        dh = lax.dot_general(m_t, n_t, _TN, preferred_element_type=_F32)
        gt_col = jnp.sum(jnp.where(eye, jnp.broadcast_to(col('g_total', p), (LANES, LANES)), 0.0),
                         axis=1, keepdims=True)
        state_ref[p] = jnp.where(same_head, gt_col * h_bd[p] + dh, 0.0)

    mean = [_split_dot(y[p], ones_bd) * (1.0 / RWKV_HEAD) for p in pairs]
    yc = [y[p] - mean[p] for p in pairs]
    var = [_split_dot(yc[p] * yc[p], ones_bd) * (1.0 / RWKV_HEAD) for p in pairs]
    for p in pairs:
        sl = sls[p]
        yn = yc[p] * lax.rsqrt(var[p] + GN_EPS) * gn_w[:, sl] + gn_b[:, sl]
        bonus = _split_dot(col('bonus_raw', p), ones_bd) * col('v', p)
        o_ref[p // HEAD_PAIRS, :, sl] = ((yn + bonus) * col('g', p)).astype(o_ref.dtype)


def rwkv7_mix(zf, mu, w0, w_up, a0, a_up, g_up, k_k, k_a, r_k, gn_w, gn_b, B, S):
    C = RWKV_CHUNK
    nc = S // C
    W = RWKV_WIDTH
    mu_p = jnp.pad(mu, (0, ZF_SLAB_PAD - RWKV_SLAB)).reshape(1, ZF_SLAB_PAD)
    vecs = jnp.stack([w0, a0, k_k, k_a, r_k, gn_w, gn_b, jnp.zeros_like(w0)]).astype(_F32)
    wup = jnp.concatenate([w_up, jnp.zeros((LANES - DECAY_RANK, W), w_up.dtype)], axis=0).astype(_BF16)
    aup = jnp.concatenate([jnp.zeros((DECAY_RANK, W), a_up.dtype), a_up], axis=0).astype(_BF16)
    gup = jnp.concatenate([g_up, jnp.zeros((2 * LANES - GATE_RANK, W), g_up.dtype)], axis=0).astype(_BF16)
    out = pl.pallas_call(
        _rwkv_kernel,
        grid=(nc,),
        in_specs=[pl.BlockSpec((B, C, ZF_SLAB_PAD), lambda c: (0, c, 0)),
                  pl.BlockSpec((1, ZF_SLAB_PAD), lambda c: (0, 0)),
                  pl.BlockSpec((SUBLANES, W), lambda c: (0, 0)),
                  pl.BlockSpec((LANES, W), lambda c: (0, 0)),
                  pl.BlockSpec((LANES, W), lambda c: (0, 0)),
                  pl.BlockSpec((2 * LANES, W), lambda c: (0, 0))],
        out_specs=pl.BlockSpec((B, C, W), lambda c: (0, c, 0)),
        out_shape=jax.ShapeDtypeStruct((B, S, W), _BF16),
        scratch_shapes=[pltpu.VMEM((B, C + SUBLANES, ZF_SLAB_PAD), _F32),
                        pltpu.VMEM((B * HEAD_PAIRS, LANES, LANES), _F32)],
        compiler_params=_params("arbitrary"),
        name="rwkv7_mix",
    )(zf.reshape(B, S, ZF_WIDTH), mu_p, vecs, wup, aup, gup)
    return out.reshape(B * S, W)


def _router_kernel(h_ref, g_ref, wr_ref, xn_ref, route_ref):
    xn = _rms_to_bf16(h_ref[...], g_ref[...])
    xn_ref[...] = xn
    logits = jnp.dot(xn, wr_ref[...], preferred_element_type=_F32)
    lane = lax.broadcasted_iota(jnp.int32, logits.shape, 1).astype(_F32)
    l1 = jnp.where(lane < N_EXPERTS, logits, -jnp.inf)
    m1 = jnp.max(l1, axis=1, keepdims=True)
    i1 = jnp.min(jnp.where(l1 == m1, lane, float(LANES)), axis=1, keepdims=True)
    l2 = jnp.where(lane == i1, -jnp.inf, l1)
    m2 = jnp.max(l2, axis=1, keepdims=True)
    i2 = jnp.min(jnp.where(l2 == m2, lane, float(LANES)), axis=1, keepdims=True)
    e2 = jnp.exp(m2 - m1)
    g1 = 1.0 / (1.0 + e2)
    g2 = e2 / (1.0 + e2)
    route_ref[...] = jnp.where(lane == 0.0, i1, jnp.where(lane == 1.0, i2,
                               jnp.where(lane == 2.0, g1, jnp.where(lane == 3.0, g2, 0.0))))


def moe_route(h, g, router, tm):
    T, D = h.shape
    wr = jnp.pad(router, ((0, 0), (0, LANES - N_EXPERTS))).astype(_BF16)
    return pl.pallas_call(
        _router_kernel,
        grid=(T // tm,),
        in_specs=[pl.BlockSpec((tm, D), lambda i: (i, 0)),
                  pl.BlockSpec((1, D), lambda i: (0, 0)),
                  pl.BlockSpec((D, LANES), lambda i: (0, 0))],
        out_specs=[pl.BlockSpec((tm, D), lambda i: (i, 0)),
                   pl.BlockSpec((tm, LANES), lambda i: (i, 0))],
        out_shape=[jax.ShapeDtypeStruct((T, D), _BF16), jax.ShapeDtypeStruct((T, LANES), _F32)],
        compiler_params=_params("parallel"),
        name="moe_route",
    )(h, g.reshape(1, D), wr)


def _moe_ffn_kernel(te_ref, nv_ref, x_ref, w1_ref, w3_ref, w2_ref, o_ref):
    n = pl.program_id(0)
    f = pl.program_id(1)

    @pl.when(jnp.logical_and(n >= nv_ref[0], f == 0))
    def _():
        o_ref[...] = jnp.zeros(o_ref.shape, o_ref.dtype)

    @pl.when(n < nv_ref[0])
    def _():
        x = x_ref[...]
        a = jnp.dot(x, w1_ref[...], preferred_element_type=_F32)
        b = jnp.dot(x, w3_ref[...], preferred_element_type=_F32)
        mid = (a * _sigmoid(a) * b).astype(_BF16)
        contrib = jnp.dot(mid, w2_ref[...], preferred_element_type=_F32)

        @pl.when(f == 0)
        def _():
            o_ref[...] = contrib

        @pl.when(f > 0)
        def _():
            o_ref[...] += contrib


def moe_grouped_ffn(xs, tile_expert, n_valid, w1, w3, w2):
    R, D = xs.shape
    F = w1.shape[2]
    nf = F // MOE_TF

    def f_idx(n, f, nv):
        return jnp.where(n < nv[0], f, nf - 1)

    return pl.pallas_call(
        _moe_ffn_kernel,
        grid_spec=pltpu.PrefetchScalarGridSpec(
            num_scalar_prefetch=2,
            grid=(R // MOE_TM, nf),
            in_specs=[pl.BlockSpec((MOE_TM, D), lambda n, f, te, nv: (n, 0)),
                      pl.BlockSpec((None, D, MOE_TF), lambda n, f, te, nv: (te[n], 0, f_idx(n, f, nv))),
                      pl.BlockSpec((None, D, MOE_TF), lambda n, f, te, nv: (te[n], 0, f_idx(n, f, nv))),
                      pl.BlockSpec((None, MOE_TF, D), lambda n, f, te, nv: (te[n], f_idx(n, f, nv), 0))],
            out_specs=pl.BlockSpec((MOE_TM, D), lambda n, f, te, nv: (n, 0))),
        out_shape=jax.ShapeDtypeStruct((R, D), _F32),
        compiler_params=_params("parallel", "arbitrary"),
        name="moe_grouped_ffn",
    )(tile_expert, n_valid, xs, w1, w3, w2)


def moe_residual(h, g, router, w1, w3, w2):
    T, D = h.shape
    E = N_EXPERTS
    xn, route = moe_route(h, g, router, tm=ROUTE_TM)
    top = route[:, 0:TOP_K_EXPERTS].astype(jnp.int32)
    gates = route[:, TOP_K_EXPERTS:2 * TOP_K_EXPERTS]
    ef = top.T.reshape(TOP_K_EXPERTS * T)
    onehot = (ef[:, None] == jnp.arange(E, dtype=jnp.int32)[None, :]).astype(jnp.int32)
    csum = jnp.cumsum(onehot, axis=0)
    counts = csum[-1]
    padded = ((counts + MOE_TM - 1) // MOE_TM) * MOE_TM
    pends = jnp.cumsum(padded)
    pstarts = pends - padded
    starts = jnp.cumsum(counts) - counts
    pos = jnp.sum(onehot * (csum - 1 + pstarts[None, :]), axis=1)
    R = TOP_K_EXPERTS * T + E * MOE_TM
    n_tiles = R // MOE_TM
    tile_start = jnp.arange(n_tiles, dtype=jnp.int32) * MOE_TM
    tile_expert = jnp.minimum(jnp.searchsorted(pends, tile_start, side='right'), E - 1).astype(jnp.int32)
    n_valid = (pends[-1:] // MOE_TM).astype(jnp.int32)
    order = jnp.argsort(ef, stable=True).astype(jnp.int32)
    first = starts[tile_expert] + tile_start - pstarts[tile_expert]
    idx = (first[:, None] + jnp.arange(MOE_TM, dtype=jnp.int32)[None, :]).reshape(R)
    src = order[jnp.clip(idx, 0, TOP_K_EXPERTS * T - 1)] % T
    xs = jnp.take(xn, src, axis=0, mode='clip')
    ys = moe_grouped_ffn(xs, tile_expert, n_valid, w1, w3, w2)
    mix = sum(gates[:, slot:slot + 1] * jnp.take(ys, pos[slot * T:(slot + 1) * T], axis=0, mode='clip')
              for slot in range(TOP_K_EXPERTS))
    return h + mix


def _pack_w_in(w):
    wq, wk, wv, wqi, wki, wwi, wslab, wga, wgb = _split(w, IN_SIZES)
    D = w.shape[0]
    wb = jnp.concatenate([wq * (ATT_HEAD_DIM ** -0.5 * LOG2_E), wk, wv, wqi, wki, wki], axis=1)
    wf = jnp.concatenate([wslab, jnp.zeros((D, ZF_SLAB_PAD - RWKV_SLAB), w.dtype),
                          wki, wwi, jnp.zeros((D, LANES - IDX_DIM - IDX_HEADS), w.dtype), wga, wgb], axis=1)
    return wb.astype(_BF16), wf.astype(_BF16)


def kernel(x, p, w_in, att_up, rwkv_up, w_out, rel_bias, rwkv_mu, rwkv_w0, rwkv_w_up,
           rwkv_a0, rwkv_a_up, rwkv_g_up, rwkv_k_k, rwkv_k_a, rwkv_r_k, rwkv_gn_w, rwkv_gn_b,
           norm_mix, norm_ffn, norm_ple, ple_proj, ple_gate, ffn_w1, ffn_w3, ffn_w2,
           moe_router, moe_w1, moe_w3, moe_w2, final_norm):
    B, S, D = x.shape
    T = B * S
    depth = w_in.shape[0]
    h = x.reshape(T, D)
    for i in range(depth):
        wb, wf = _pack_w_in(w_in[i])
        zb = norm_matmul(h, norm_mix[i], wb, _BF16, tm=IN_PROJ_TM, tn=ZB_WIDTH)
        zf = norm_matmul(h, norm_mix[i], wf, _F32, tm=IN_PROJ_F32_TM, tn=IN_PROJ_F32_TN)
        att = dsa_attention(zb, zf, rel_bias, B, S)
        rw = rwkv7_mix(zf, rwkv_mu[i], rwkv_w0[i], rwkv_w_up[i], rwkv_a0[i], rwkv_a_up[i], rwkv_g_up[i],
                       rwkv_k_k[i], rwkv_k_a[i], rwkv_r_k[i], rwkv_gn_w[i], rwkv_gn_b[i], B, S)
        merged = gated_merge(att, rw, zf, att_up[i].astype(_BF16), rwkv_up[i].astype(_BF16),
                             tm=MERGE_TM, tn=MERGE_TN)
        h = matmul_residual(merged, w_out[i].astype(_BF16), h, tm=OUT_PROJ_TM, tn=D)
        if i % 2 == 0:
            j = i // 2
            h = ffn_residual(h, norm_ffn[i], ffn_w1[j].astype(_BF16), ffn_w3[j].astype(_BF16),
                             ffn_w2[j].astype(_BF16), tm=FFN_TM, tf=FFN_TF)
        else:
            j = i // 2
            h = moe_residual(h, norm_ffn[i], moe_router[j], cast_bf16(moe_w1[j]), cast_bf16(moe_w3[j]),
                             cast_bf16(moe_w2[j]))
        h = ple_residual(h, norm_ple[i], p[i].reshape(T, PLE_DIM), ple_gate[i].astype(_BF16),
                         ple_proj[i].astype(_BF16), final_norm, final=(i == depth - 1), tm=PLE_TM)
    return h.reshape(B, S, D)
```
